```python
import math
import jax, jax.numpy as jnp
from jax import lax
import numpy as np

D_MODEL = 1024
BATCH = 16
SEQ = 4096
DEPTH = 2

HEAD_DIM = 64
N_Q_HEADS = 8
N_KV_HEADS = 2
GQA_GROUP = N_Q_HEADS // N_KV_HEADS
ATTN_WIDTH = N_Q_HEADS * HEAD_DIM
KV_WIDTH = N_KV_HEADS * HEAD_DIM
N_BRANCH = 3
SSM_WIDTH = D_MODEL // 4
SSM_GROUP = 16
SSM_N_GROUPS = SSM_WIDTH // SSM_GROUP
SSM_STATE = 64
CONV_WIDTH = D_MODEL // 4
CONV_K = 31
MIX_WIDTH = ATTN_WIDTH + SSM_WIDTH + CONV_WIDTH
CMP_LEN = 32
CMP_STRIDE = 16
CMP_HIDDEN = 256
SLC_BLOCK = 64
SLC_TOPN = 16
WINDOW = 512
Q_BLOCK = 32
ROPE_THETA = 10000.0
D_FF = ((8 * D_MODEL // 3 + 255) // 256) * 256
EPS = 1e-6
NEG_INF = -1e30
FORCE = 1e9

_SIZES = (ATTN_WIDTH, KV_WIDTH, KV_WIDTH, KV_WIDTH, KV_WIDTH, KV_WIDTH, KV_WIDTH,
          N_BRANCH * N_Q_HEADS, SSM_WIDTH, 2 * CONV_WIDTH)
IN_WIDTH = sum(_SIZES)
SPLIT_POINTS = tuple(sum(_SIZES[:i + 1]) for i in range(len(_SIZES) - 1))

kernel_name = "hymba_nsa_s5_conformer_block"


def rmsnorm(x, g):
    xf = x.astype(jnp.float32)
    y = xf * lax.rsqrt(jnp.mean(xf * xf, axis=-1, keepdims=True) + EPS)
    return (y * g.astype(jnp.float32)).astype(x.dtype)


def layernorm(x, g, b):
    xf = x.astype(jnp.float32)
    mu = jnp.mean(xf, axis=-1, keepdims=True)
    var = jnp.mean(jnp.square(xf - mu), axis=-1, keepdims=True)
    y = (xf - mu) * lax.rsqrt(var + EPS) * g.astype(jnp.float32) + b.astype(jnp.float32)
    return y.astype(x.dtype)


def rope(x, pos):
    half = x.shape[-1] // 2
    freqs = ROPE_THETA ** (-jnp.arange(half, dtype=jnp.float32) / half)
    ang = pos.astype(jnp.float32)[:, None] * freqs[None, :]
    cos = jnp.cos(ang)[:, None, :]
    sin = jnp.sin(ang)[:, None, :]
    xf = x.astype(jnp.float32)
    x1, x2 = xf[..., :half], xf[..., half:]
    return jnp.concatenate([x1 * cos - x2 * sin, x2 * cos + x1 * sin], axis=-1).astype(x.dtype)


def masked_softmax(s, mask):
    s = jnp.where(mask, s, NEG_INF)
    m = jnp.max(s, axis=-1, keepdims=True)
    p = jnp.where(mask, jnp.exp(s - m), 0.0)
    return p / jnp.maximum(jnp.sum(p, axis=-1, keepdims=True), 1e-30)


def compress_blocks(x, pe, w1, w2):
    b, s, h, d = x.shape
    halves = x.reshape(b, s // CMP_STRIDE, CMP_STRIDE, h, d)
    blocks = jnp.concatenate([halves[:, :-1], halves[:, 1:]], axis=2)
    blocks = blocks + pe[None, None, :, None, :]
    nc = s // CMP_STRIDE - 1
    flat = blocks.transpose(0, 1, 3, 2, 4).reshape(b, nc, h, CMP_LEN * d)
    return jax.nn.gelu(flat @ w1) @ w2


def nsa_attention(q, kc, vc, ks, vs, kw, vw, gates):
    b, s, _, d = q.shape
    nc = kc.shape[1]
    n_sel = s // SLC_BLOCK
    top_n = min(SLC_TOPN, n_sel)
    qg = (q * (d ** -0.5)).reshape(b, s, N_KV_HEADS, GQA_GROUP, d)
    cmp_end = jnp.arange(nc) * CMP_STRIDE + CMP_LEN - 1
    ci = np.arange(nc)[:, None]
    sj = np.arange(n_sel)[None, :]
    overlap = jnp.asarray(((ci * CMP_STRIDE < (sj + 1) * SLC_BLOCK) &
                           (ci * CMP_STRIDE + CMP_LEN > sj * SLC_BLOCK)).astype(np.float32))
    ks_blocks = ks.reshape(b, n_sel, SLC_BLOCK, N_KV_HEADS, d).transpose(0, 3, 1, 2, 4)
    vs_blocks = vs.reshape(b, n_sel, SLC_BLOCK, N_KV_HEADS, d).transpose(0, 3, 1, 2, 4)
    kw_pad = jnp.pad(kw, ((0, 0), (WINDOW, 0), (0, 0), (0, 0)))
    vw_pad = jnp.pad(vw, ((0, 0), (WINDOW, 0), (0, 0), (0, 0)))
    gather = jax.vmap(jax.vmap(lambda blk, idx: blk[idx]))
    blk_ids = jnp.arange(n_sel)
    offs = jnp.arange(SLC_BLOCK)

    def block_fn(s0):
        t = s0 + jnp.arange(Q_BLOCK)
        qb = lax.dynamic_slice_in_dim(qg, s0, Q_BLOCK, axis=1)
        sc = jnp.einsum('bqgrd,bngd->bgrqn', qb, kc).astype(jnp.float32)
        pc = masked_softmax(sc, cmp_end[None, :] <= t[:, None])
        oc = jnp.einsum('bgrqn,bngd->bqgrd', pc, vc)
        imp = jnp.einsum('bgrqn,nj->bgqj', pc, overlap)
        cur = t // SLC_BLOCK
        forced = (blk_ids[None, :] == 0) | (blk_ids[None, :] == cur[:, None]) | (blk_ids[None, :] == cur[:, None] - 1)
        causal = blk_ids[None, :] * SLC_BLOCK <= t[:, None]
        imp = jnp.where(forced, FORCE, jnp.where(causal, imp, -FORCE))
        _, idx = lax.top_k(imp, top_n)
        kg = gather(ks_blocks, idx)
        vg = gather(vs_blocks, idx)
        ss = jnp.einsum('bqgrd,bgqnld->bgrqnl', qb, kg).astype(jnp.float32)
        ss = ss.reshape(b, N_KV_HEADS, GQA_GROUP, Q_BLOCK, top_n * SLC_BLOCK)
        kpos = idx[..., None] * SLC_BLOCK + offs
        ms = (kpos <= t[None, None, :, None, None]).reshape(b, N_KV_HEADS, 1, Q_BLOCK, top_n * SLC_BLOCK)
        ps = masked_softmax(ss, ms)
        os_ = jnp.einsum('bgrqm,bgqmd->bqgrd', ps,
                         vg.reshape(b, N_KV_HEADS, Q_BLOCK, top_n * SLC_BLOCK, d))
        kwb = lax.dynamic_slice_in_dim(kw_pad, s0, WINDOW + Q_BLOCK, axis=1)
        vwb = lax.dynamic_slice_in_dim(vw_pad, s0, WINDOW + Q_BLOCK, axis=1)
        sw = jnp.einsum('bqgrd,bkgd->bgrqk', qb, kwb).astype(jnp.float32)
        kp = s0 - WINDOW + jnp.arange(WINDOW + Q_BLOCK)
        mw = (kp[None, :] <= t[:, None]) & (kp[None, :] > t[:, None] - WINDOW) & (kp[None, :] >= 0)
        pw = masked_softmax(sw, mw)
        ow = jnp.einsum('bgrqk,bkgd->bqgrd', pw, vwb)
        gb = lax.dynamic_slice_in_dim(gates, s0, Q_BLOCK, axis=1).astype(jnp.float32)
        return gb[:, :, 0, :, :, None] * oc + gb[:, :, 1, :, :, None] * os_ + gb[:, :, 2, :, :, None] * ow

    starts = jnp.arange(s // Q_BLOCK) * Q_BLOCK
    outs = lax.map(block_fn, starts)
    return outs.transpose(1, 0, 2, 3, 4, 5).reshape(b, s, N_Q_HEADS * d).astype(q.dtype)


def s5_ssm(u, a_re, a_im, log_dt, b_re, b_im, c_re, c_im, d_skip, w_glu):
    bsz, s, _ = u.shape
    f32 = jnp.float32
    uf = u.astype(f32).reshape(bsz, s, SSM_N_GROUPS, SSM_GROUP)
    a_re, a_im = a_re.astype(f32), a_im.astype(f32)
    b_re, b_im = b_re.astype(f32), b_im.astype(f32)
    dt = jnp.exp(log_dt.astype(f32))[:, None]
    mag = jnp.exp(a_re * dt)
    lam_re = mag * jnp.cos(a_im * dt)
    lam_im = mag * jnp.sin(a_im * dt)
    den = a_re * a_re + a_im * a_im
    nr, ni = lam_re - 1.0, lam_im
    f_re = (nr * a_re + ni * a_im) / den
    f_im = (ni * a_re - nr * a_im) / den
    bb_re = f_re[..., None] * b_re - f_im[..., None] * b_im
    bb_im = f_re[..., None] * b_im + f_im[..., None] * b_re
    bu_re = jnp.einsum('gpc,bsgc->bsgp', bb_re, uf)
    bu_im = jnp.einsum('gpc,bsgc->bsgp', bb_im, uf)
    shape = (1, s, SSM_N_GROUPS, SSM_STATE)
    la_re = jnp.broadcast_to(lam_re[None, None], shape)
    la_im = jnp.broadcast_to(lam_im[None, None], shape)

    def combine(e1, e2):
        a1r, a1i, b1r, b1i = e1
        a2r, a2i, b2r, b2i = e2
        return (a2r * a1r - a2i * a1i, a2r * a1i + a2i * a1r,
                a2r * b1r - a2i * b1i + b2r, a2r * b1i + a2i * b1r + b2i)

    _, _, xr, xi = lax.associative_scan(combine, (la_re, la_im, bu_re, bu_im), axis=1)
    y = jnp.einsum('gcp,bsgp->bsgc', c_re.astype(f32), xr) - jnp.einsum('gcp,bsgp->bsgc', c_im.astype(f32), xi)
    y = (y + d_skip.astype(f32).reshape(SSM_N_GROUPS, SSM_GROUP) * uf).reshape(bsz, s, SSM_WIDTH)
    y = jax.nn.gelu(y)
    a, g = jnp.split(y @ w_glu.astype(f32), 2, axis=-1)
    return (a * jax.nn.sigmoid(g)).astype(u.dtype)


def conformer_conv(h, w_dw, b_dw, ln_g, ln_b, w_pw, b_pw):
    a, g = jnp.split(h, 2, axis=-1)
    u = a * jax.nn.sigmoid(g)
    y = lax.conv_general_dilated(u, w_dw[:, None, :], window_strides=(1,),
                                 padding=[(CONV_K - 1, 0)],
                                 dimension_numbers=('NWC', 'WIO', 'NWC'),
                                 feature_group_count=CONV_WIDTH) + b_dw
    y = jax.nn.silu(layernorm(y, ln_g, ln_b))
    return y @ w_pw + b_pw


def hybrid_layer(x, norm_mix, w_in, cmp_pe_k, cmp_k_w1, cmp_k_w2, cmp_pe_v, cmp_v_w1, cmp_v_w2,
                 ssm_a_re, ssm_a_im, ssm_log_dt, ssm_b_re, ssm_b_im, ssm_c_re, ssm_c_im, ssm_d, ssm_w_glu,
                 conv_w_dw, conv_b_dw, conv_ln_g, conv_ln_b, conv_w_pw, conv_b_pw,
                 norm_out_attn, norm_out_ssm, norm_out_conv, w_out, norm_ffn, w_gate, w_up, w_down):
    b, s, _ = x.shape
    h = rmsnorm(x, norm_mix)
    proj = h @ w_in
    q, kc, vc, ks, vs, kw, vw, g_cols, u_ssm, u_conv = jnp.split(proj, SPLIT_POINTS, axis=-1)
    pos = jnp.arange(s)
    heads = lambda t, n: t.reshape(b, s, n, HEAD_DIM)
    q = rope(heads(q, N_Q_HEADS), pos)
    ks = rope(heads(ks, N_KV_HEADS), pos)
    kw = rope(heads(kw, N_KV_HEADS), pos)
    vs = heads(vs, N_KV_HEADS)
    vw = heads(vw, N_KV_HEADS)
    kc = compress_blocks(heads(kc, N_KV_HEADS), cmp_pe_k, cmp_k_w1, cmp_k_w2)
    vc = compress_blocks(heads(vc, N_KV_HEADS), cmp_pe_v, cmp_v_w1, cmp_v_w2)
    nc = kc.shape[1]
    kc = rope(kc, jnp.arange(nc) * CMP_STRIDE + CMP_LEN - 1)
    gates = jax.nn.sigmoid(g_cols).reshape(b, s, N_BRANCH, N_KV_HEADS, GQA_GROUP)
    y_attn = nsa_attention(q, kc, vc, ks, vs, kw, vw, gates)
    y_ssm = s5_ssm(u_ssm, ssm_a_re, ssm_a_im, ssm_log_dt, ssm_b_re, ssm_b_im, ssm_c_re, ssm_c_im, ssm_d, ssm_w_glu)
    y_conv = conformer_conv(u_conv, conv_w_dw, conv_b_dw, conv_ln_g, conv_ln_b, conv_w_pw, conv_b_pw)
    mixed = jnp.concatenate([rmsnorm(y_attn, norm_out_attn), rmsnorm(y_ssm, norm_out_ssm),
                             rmsnorm(y_conv, norm_out_conv)], axis=-1)
    x = x + mixed @ w_out
    h = rmsnorm(x, norm_ffn)
    x = x + (jax.nn.silu(h @ w_gate) * (h @ w_up)) @ w_down
    return x


def _normal(k, shape, scale):
    return scale * jax.random.normal(k, shape, jnp.float32)


def setup_inputs(seed: int = 0) -> dict:
    key = jax.random.key(seed)
    ks = jax.random.split(key, 40)
    L = DEPTH
    gain = lambda k, n: 1.0 + _normal(k, (L, n), 0.02)
    return {
        "x": _normal(ks[0], (BATCH, SEQ, D_MODEL), 1.0),
        "norm_mix": gain(ks[1], D_MODEL),
        "w_in": _normal(ks[2], (L, D_MODEL, IN_WIDTH), D_MODEL ** -0.5),
        "cmp_pe_k": _normal(ks[3], (L, CMP_LEN, HEAD_DIM), 0.02),
        "cmp_k_w1": _normal(ks[4], (L, CMP_LEN * HEAD_DIM, CMP_HIDDEN), (CMP_LEN * HEAD_DIM) ** -0.5),
        "cmp_k_w2": _normal(ks[5], (L, CMP_HIDDEN, HEAD_DIM), CMP_HIDDEN ** -0.5),
        "cmp_pe_v": _normal(ks[6], (L, CMP_LEN, HEAD_DIM), 0.02),
        "cmp_v_w1": _normal(ks[7], (L, CMP_LEN * HEAD_DIM, CMP_HIDDEN), (CMP_LEN * HEAD_DIM) ** -0.5),
        "cmp_v_w2": _normal(ks[8], (L, CMP_HIDDEN, HEAD_DIM), CMP_HIDDEN ** -0.5),
        "ssm_a_re": -0.5 + _normal(ks[9], (L, SSM_N_GROUPS, SSM_STATE), 0.01),
        "ssm_a_im": jnp.pi * jnp.arange(SSM_STATE, dtype=jnp.float32)[None, None, :]
                    + _normal(ks[10], (L, SSM_N_GROUPS, SSM_STATE), 0.01),
        "ssm_log_dt": jax.random.uniform(ks[11], (L, SSM_N_GROUPS), jnp.float32,
                                         minval=math.log(0.001), maxval=math.log(0.1)),
        "ssm_b_re": _normal(ks[12], (L, SSM_N_GROUPS, SSM_STATE, SSM_GROUP), (2 * SSM_GROUP) ** -0.5),
        "ssm_b_im": _normal(ks[13], (L, SSM_N_GROUPS, SSM_STATE, SSM_GROUP), (2 * SSM_GROUP) ** -0.5),
        "ssm_c_re": _normal(ks[14], (L, SSM_N_GROUPS, SSM_GROUP, SSM_STATE), SSM_STATE ** -0.5),
        "ssm_c_im": _normal(ks[15], (L, SSM_N_GROUPS, SSM_GROUP, SSM_STATE), SSM_STATE ** -0.5),
        "ssm_d": _normal(ks[16], (L, SSM_WIDTH), 1.0),
        "ssm_w_glu": _normal(ks[17], (L, SSM_WIDTH, 2 * SSM_WIDTH), SSM_WIDTH ** -0.5),
        "conv_w_dw": _normal(ks[18], (L, CONV_K, CONV_WIDTH), CONV_K ** -0.5),
        "conv_b_dw": _normal(ks[19], (L, CONV_WIDTH), 0.02),
        "conv_ln_g": gain(ks[20], CONV_WIDTH),
        "conv_ln_b": _normal(ks[21], (L, CONV_WIDTH), 0.02),
        "conv_w_pw": _normal(ks[22], (L, CONV_WIDTH, CONV_WIDTH), CONV_WIDTH ** -0.5),
        "conv_b_pw": _normal(ks[23], (L, CONV_WIDTH), 0.02),
        "norm_out_attn": gain(ks[24], ATTN_WIDTH),
        "norm_out_ssm": gain(ks[25], SSM_WIDTH),
        "norm_out_conv": gain(ks[26], CONV_WIDTH),
        "w_out": _normal(ks[27], (L, MIX_WIDTH, D_MODEL), MIX_WIDTH ** -0.5),
        "norm_ffn": gain(ks[28], D_MODEL),
        "w_gate": _normal(ks[29], (L, D_MODEL, D_FF), D_MODEL ** -0.5),
        "w_up": _normal(ks[30], (L, D_MODEL, D_FF), D_MODEL ** -0.5),
        "w_down": _normal(ks[31], (L, D_FF, D_MODEL), D_FF ** -0.5),
        "norm_final": 1.0 + _normal(ks[32], (D_MODEL,), 0.02),
    }


def reference(x, norm_mix, w_in, cmp_pe_k, cmp_k_w1, cmp_k_w2, cmp_pe_v, cmp_v_w1, cmp_v_w2,
              ssm_a_re, ssm_a_im, ssm_log_dt, ssm_b_re, ssm_b_im, ssm_c_re, ssm_c_im, ssm_d, ssm_w_glu,
              conv_w_dw, conv_b_dw, conv_ln_g, conv_ln_b, conv_w_pw, conv_b_pw,
              norm_out_attn, norm_out_ssm, norm_out_conv, w_out, norm_ffn, w_gate, w_up, w_down,
              norm_final):
    for l in range(DEPTH):
        x = hybrid_layer(x, norm_mix[l], w_in[l], cmp_pe_k[l], cmp_k_w1[l], cmp_k_w2[l],
                         cmp_pe_v[l], cmp_v_w1[l], cmp_v_w2[l],
                         ssm_a_re[l], ssm_a_im[l], ssm_log_dt[l], ssm_b_re[l], ssm_b_im[l],
                         ssm_c_re[l], ssm_c_im[l], ssm_d[l], ssm_w_glu[l],
                         conv_w_dw[l], conv_b_dw[l], conv_ln_g[l], conv_ln_b[l], conv_w_pw[l], conv_b_pw[l],
                         norm_out_attn[l], norm_out_ssm[l], norm_out_conv[l], w_out[l],
                         norm_ffn[l], w_gate[l], w_up[l], w_down[l])
    return rmsnorm(x, norm_final)
```

```python
import functools
import math

import numpy as np
import jax
import jax.numpy as jnp
from jax import lax
from jax.experimental import pallas as pl
from jax.experimental.pallas import tpu as pltpu

F32 = jnp.float32
BF16 = jnp.bfloat16

D_MODEL = 1024
HEAD_DIM = 64
N_Q_HEADS = 8
N_KV_HEADS = 2
GQA_GROUP = N_Q_HEADS // N_KV_HEADS
ATTN_WIDTH = N_Q_HEADS * HEAD_DIM
KV_WIDTH = N_KV_HEADS * HEAD_DIM
N_BRANCH = 3
SSM_WIDTH = D_MODEL // 4
SSM_GROUP = 16
SSM_N_GROUPS = SSM_WIDTH // SSM_GROUP
SSM_STATE = 64
SSM_STATES_ALL = SSM_N_GROUPS * SSM_STATE
CONV_WIDTH = D_MODEL // 4
CONV_K = 31
CMP_LEN = 32
CMP_STRIDE = 16
CMP_HIDDEN = 256
SLC_BLOCK = 64
SLC_TOPN = 16
WINDOW = 512
ROPE_THETA = 10000.0
D_FF = ((8 * D_MODEL // 3 + 255) // 256) * 256
EPS = 1e-6
NEG_INF = -1e30
FORCE = 1e9

LANES = 128
SUBLANES = 8
VMEM_LIMIT = 56 * 1024 * 1024

GATE_PAD = LANES
C_Q = 0
C_KC = C_Q + ATTN_WIDTH
C_VC = C_KC + KV_WIDTH
C_KS = C_VC + KV_WIDTH
C_VS = C_KS + KV_WIDTH
C_KW = C_VS + KV_WIDTH
C_VW = C_KW + KV_WIDTH
C_GATE = C_VW + KV_WIDTH
C_SSM = C_GATE + GATE_PAD
C_CONV = C_SSM + SSM_WIDTH
IN_PACKED = C_CONV + 2 * CONV_WIDTH

TM_PROJ = 512
TQ = 128
CK = 512
SSM_LC = 64
CONV_TC = 128
CONV_PAD = 32


def _gelu_tanh(x):
    return 0.5 * x * (1.0 + jnp.tanh(math.sqrt(2.0 / math.pi) * (x + 0.044715 * (x * x * x))))


def _sigmoid(x):
    return 1.0 / (1.0 + jnp.exp(-x))


def _rms(x, g):
    return x * lax.rsqrt(jnp.mean(x * x, axis=-1, keepdims=True) + EPS) * g


def _dot(a, b):
    return jnp.dot(a, b, preferred_element_type=F32)


def _dot_nt(a, b):
    return lax.dot_general(a, b, (((1,), (1,)), ((), ())), preferred_element_type=F32)


def _rope_rotate(v, cos, sin_signed):
    lane = lax.broadcasted_iota(jnp.int32, v.shape, 1)
    first_half = (lane % HEAD_DIM) < (HEAD_DIM // 2)
    rot = jnp.where(first_half, pltpu.roll(v, LANES - HEAD_DIM // 2, 1), pltpu.roll(v, HEAD_DIM // 2, 1))
    return v * cos + rot * sin_signed


def _in_proj_kernel(x_ref, g_ref, w_ref, cos_ref, sin_ref,
                    q_ref, kc_ref, vc_ref, ksa_ref, vsa_ref, kwa_ref, vwa_ref, gate_ref, ussm_ref, uconv_ref,
                    *, tiles_per_seq):
    tm = x_ref.shape[0]
    h = _rms(x_ref[...], g_ref[...]).astype(BF16)
    cos = cos_ref[...]
    sin = sin_ref[...]

    def proj(c0, n):
        return _dot(h, w_ref[:, c0:c0 + n])

    for j in range(ATTN_WIDTH // LANES):
        qj = _rope_rotate(proj(C_Q + j * LANES, LANES), cos, sin) * (HEAD_DIM ** -0.5)
        q_ref[:, j * LANES:(j + 1) * LANES] = qj.astype(BF16)
    kc_ref[...] = proj(C_KC, KV_WIDTH).astype(BF16)
    vc_ref[...] = proj(C_VC, KV_WIDTH).astype(BF16)

    lane = lax.broadcasted_iota(jnp.int32, (tm, LANES), 1)
    row = lax.broadcasted_iota(jnp.int32, (tm, LANES), 0)
    t = (pl.program_id(0) % tiles_per_seq) * tm + row
    low = lane < HEAD_DIM
    blk_onehot = jnp.where((t // SLC_BLOCK) == (lane - HEAD_DIM), 1.0, 0.0)
    ones_col = jnp.where(lane == HEAD_DIM, 1.0, 0.0)

    def split_groups(v, upper, ref):
        ref[0, 0] = jnp.where(low, v, upper).astype(BF16)
        ref[0, 1] = jnp.where(low, pltpu.roll(v, HEAD_DIM, 1), upper).astype(BF16)

    split_groups(_rope_rotate(proj(C_KS, KV_WIDTH), cos, sin), blk_onehot, ksa_ref)
    split_groups(proj(C_VS, KV_WIDTH), ones_col, vsa_ref)
    split_groups(_rope_rotate(proj(C_KW, KV_WIDTH), cos, sin), 0.0, kwa_ref)
    split_groups(proj(C_VW, KV_WIDTH), ones_col, vwa_ref)

    gate_ref[...] = _sigmoid(proj(C_GATE, GATE_PAD))
    ussm_ref[...] = proj(C_SSM, SSM_WIDTH).astype(BF16)
    a = proj(C_CONV, CONV_WIDTH)
    g = proj(C_CONV + CONV_WIDTH, CONV_WIDTH)
    uconv_ref[...] = a * _sigmoid(g)


def _in_proj(x2d, g, w_packed, cos, sin, batch, seq):
    tm = TM_PROJ
    tiles_per_seq = seq // tm
    tokens = batch * seq
    row_blk = lambda n, dt: (pl.BlockSpec((tm, n), lambda i: (i, 0)), jax.ShapeDtypeStruct((tokens, n), dt))
    grp_blk = (pl.BlockSpec((1, N_KV_HEADS, tm, LANES), lambda i: (i // tiles_per_seq, 0, i % tiles_per_seq, 0)),
               jax.ShapeDtypeStruct((batch, N_KV_HEADS, seq, LANES), BF16))
    outs = [row_blk(ATTN_WIDTH, BF16), row_blk(KV_WIDTH, BF16), row_blk(KV_WIDTH, BF16),
            grp_blk, grp_blk, grp_blk, grp_blk, row_blk(GATE_PAD, F32),
            (pl.BlockSpec((tm, SSM_WIDTH), lambda i: (i % tiles_per_seq, i // tiles_per_seq)),
             jax.ShapeDtypeStruct((seq, batch * SSM_WIDTH), BF16)),
            row_blk(CONV_WIDTH, F32)]
    return pl.pallas_call(
        functools.partial(_in_proj_kernel, tiles_per_seq=tiles_per_seq),
        grid=(tokens // tm,),
        in_specs=[pl.BlockSpec((tm, D_MODEL), lambda i: (i, 0)),
                  pl.BlockSpec((1, D_MODEL), lambda i: (0, 0)),
                  pl.BlockSpec((D_MODEL, IN_PACKED), lambda i: (0, 0)),
                  pl.BlockSpec((tm, LANES), lambda i: (i % tiles_per_seq, 0)),
                  pl.BlockSpec((tm, LANES), lambda i: (i % tiles_per_seq, 0))],
        out_specs=[o[0] for o in outs],
        out_shape=[o[1] for o in outs],
        compiler_params=pltpu.CompilerParams(dimension_semantics=("parallel",), vmem_limit_bytes=VMEM_LIMIT),
        name="in_proj",
    )(x2d, g, w_packed, cos, sin)


def _compress_kernel(kh_ref, vh_ref, wk_ref, wv_ref, pek_ref, pev_ref, w2k_ref, w2v_ref, cos_ref, sin_ref,
                     kc_ref, vc_ref):
    n_half = kh_ref.shape[1]
    hw = N_KV_HEADS * CMP_HIDDEN

    def compress(h_ref, w_ref, pe_ref, w2_ref):
        ab = _dot(h_ref[0], w_ref[...])
        pe = _dot(pe_ref[...], w_ref[...])
        bias = pe[0:1, 0:hw] + pe[4:5, hw:2 * hw]
        nxt = pltpu.roll(ab[:, hw:2 * hw], n_half - 1, 0)
        hid = _gelu_tanh(ab[:, 0:hw] + nxt + bias).astype(BF16)
        return jnp.concatenate([_dot(hid[:, j * CMP_HIDDEN:(j + 1) * CMP_HIDDEN], w2_ref[...])
                                for j in range(N_KV_HEADS)], axis=1)

    kc = compress(kh_ref, wk_ref, pek_ref, w2k_ref)
    kc_ref[0] = _rope_rotate(kc, cos_ref[...], sin_ref[...]).astype(BF16)
    vc_ref[0] = compress(vh_ref, wv_ref, pev_ref, w2v_ref).astype(BF16)


def _compress(kh, vh, wk, wv, pek, pev, w2k, w2v, cos_c, sin_c):
    batch, n_half, width = kh.shape
    full = lambda a: pl.BlockSpec(a.shape, lambda b: (0,) * a.ndim)
    per_b = pl.BlockSpec((1, n_half, width), lambda b: (b, 0, 0))
    out_blk = pl.BlockSpec((1, n_half, KV_WIDTH), lambda b: (b, 0, 0))
    out_shape = jax.ShapeDtypeStruct((batch, n_half, KV_WIDTH), BF16)
    return pl.pallas_call(
        _compress_kernel,
        grid=(batch,),
        in_specs=[per_b, per_b, full(wk), full(wv), full(pek), full(pev), full(w2k), full(w2v),
                  full(cos_c), full(sin_c)],
        out_specs=[out_blk, out_blk],
        out_shape=[out_shape, out_shape],
        compiler_params=pltpu.CompilerParams(dimension_semantics=("parallel",), vmem_limit_bytes=VMEM_LIMIT),
        name="compress",
    )(kh, vh, wk, wv, pek, pev, w2k, w2v, cos_c, sin_c)


def _attention_kernel(q_ref, kc_ref, vc_ref, ksa_ref, vsa_ref, kwa_ref, vwa_ref, gate_ref, ovl_ref, y_ref,
                      m_ref, acc_ref):
    tq = q_ref.shape[0]
    rows = GQA_GROUP * tq
    n_cmp = kc_ref.shape[1]
    seq = ksa_ref.shape[2]
    n_sel = seq // SLC_BLOCK
    t0 = pl.program_id(1) * tq
    row_q = lax.broadcasted_iota(jnp.int32, (rows, 1), 0) % tq
    t_row = t0 + row_q
    gates = gate_ref[...]

    for g in range(N_KV_HEADS):
        qg = q_ref[:, g * GQA_GROUP * HEAD_DIM:(g + 1) * GQA_GROUP * HEAD_DIM]
        q_rows = jnp.concatenate([qg[:, r * HEAD_DIM:(r + 1) * HEAD_DIM] for r in range(GQA_GROUP)], axis=0)

        kc = kc_ref[0, :, g * HEAD_DIM:(g + 1) * HEAD_DIM]
        vc = vc_ref[0, :, g * HEAD_DIM:(g + 1) * HEAD_DIM]
        sc = _dot_nt(q_rows, kc)
        cmp_end = lax.broadcasted_iota(jnp.int32, (rows, n_cmp), 1) * CMP_STRIDE + (CMP_LEN - 1)
        mc = cmp_end <= t_row
        sc = jnp.where(mc, sc, NEG_INF)
        pc = jnp.where(mc, jnp.exp(sc - jnp.max(sc, axis=-1, keepdims=True)), 0.0)
        pc = pc / jnp.maximum(jnp.sum(pc, axis=-1, keepdims=True), 1e-30)
        oc = _dot(pc.astype(BF16), vc)

        pc_sum = pc[0:tq]
        for r in range(1, GQA_GROUP):
            pc_sum = pc_sum + pc[r * tq:(r + 1) * tq]
        p_hi = pc_sum.astype(BF16)
        rem = pc_sum - p_hi.astype(F32)
        p_mid = rem.astype(BF16)
        p_lo = (rem - p_mid.astype(F32)).astype(BF16)
        ovl = ovl_ref[...]
        imp = _dot_nt(ovl, p_hi) + _dot_nt(ovl, p_mid) + _dot_nt(ovl, p_lo)

        blk = lax.broadcasted_iota(jnp.int32, (n_sel, tq), 0)
        cur = (t0 + lax.broadcasted_iota(jnp.int32, (n_sel, tq), 1)) // SLC_BLOCK
        forced = (blk == 0) | (blk == cur) | (blk == cur - 1)
        causal = blk <= cur
        val = jnp.where(forced, FORCE, jnp.where(causal, imp, -FORCE))
        rank = jnp.zeros((n_sel, tq), jnp.int32)
        for i in range(n_sel):
            vi = val[i:i + 1, :]
            before = (vi > val) | ((vi == val) & (blk > i))
            rank = rank + jnp.where(before, 1, 0)
        selected = causal & (rank < SLC_TOPN)
        bias_t = jnp.where(selected, 0.0, NEG_INF)
        if n_sel < LANES:
            bias_t = jnp.concatenate([bias_t, jnp.full((LANES - n_sel, tq), NEG_INF, F32)], axis=0)
        bias = bias_t.T[:, 0:HEAD_DIM].astype(BF16)
        q_aug = jnp.concatenate([q_rows, jnp.concatenate([bias] * GQA_GROUP, axis=0)], axis=1)

        m_ref[...] = jnp.full(m_ref.shape, NEG_INF, F32)
        acc_ref[...] = jnp.zeros(acc_ref.shape, F32)

        def chunk(c, carry):
            k0 = pl.multiple_of(c * CK, CK)
            s = _dot_nt(q_aug, ksa_ref[0, g, pl.ds(k0, CK), :])
            kpos = k0 + lax.broadcasted_iota(jnp.int32, (rows, CK), 1)
            s = jnp.where(kpos <= t_row, s, NEG_INF)
            m_old = m_ref[...]
            m_new = jnp.maximum(m_old, jnp.max(s, axis=-1, keepdims=True))
            p = jnp.exp(s - m_new).astype(BF16)
            acc_ref[...] = jnp.exp(m_old - m_new) * acc_ref[...] + _dot(p, vsa_ref[0, g, pl.ds(k0, CK), :])
            m_ref[...] = m_new
            return carry

        lax.fori_loop(0, (t0 + tq + CK - 1) // CK, chunk, 0)
        acc = acc_ref[...]
        o_sel = acc[:, 0:HEAD_DIM] / jnp.maximum(acc[:, HEAD_DIM:HEAD_DIM + 1], 1e-30)

        w0 = pl.multiple_of(jnp.maximum(t0 - WINDOW, 0), tq)
        sw = _dot_nt(q_aug, kwa_ref[0, g, pl.ds(w0, WINDOW + tq), :])
        kp = w0 + lax.broadcasted_iota(jnp.int32, (rows, WINDOW + tq), 1)
        mw = (kp <= t_row) & (kp > t_row - WINDOW)
        sw = jnp.where(mw, sw, NEG_INF)
        pw = jnp.where(mw, jnp.exp(sw - jnp.max(sw, axis=-1, keepdims=True)), 0.0)
        aw = _dot(pw.astype(BF16), vwa_ref[0, g, pl.ds(w0, WINDOW + tq), :])
        o_win = aw[:, 0:HEAD_DIM] / jnp.maximum(aw[:, HEAD_DIM:HEAD_DIM + 1], 1e-30)

        outs = []
        for r in range(GQA_GROUP):
            sl = slice(r * tq, (r + 1) * tq)
            col = g * GQA_GROUP + r
            gate_c = gates[:, col:col + 1]
            gate_s = gates[:, N_Q_HEADS + col:N_Q_HEADS + col + 1]
            gate_w = gates[:, 2 * N_Q_HEADS + col:2 * N_Q_HEADS + col + 1]
            outs.append(gate_c * oc[sl] + gate_s * o_sel[sl] + gate_w * o_win[sl])
        y_ref[:, g * GQA_GROUP * HEAD_DIM:(g + 1) * GQA_GROUP * HEAD_DIM] = jnp.concatenate(outs, axis=1).astype(BF16)


def _attention(q, kc, vc, ksa, vsa, kwa, vwa, gates, ovl, batch, seq):
    tq = TQ
    nq = seq // tq
    n_cmp = kc.shape[1]
    grp = pl.BlockSpec((1, N_KV_HEADS, seq, LANES), lambda b, i: (b, 0, 0, 0))
    cmp_blk = pl.BlockSpec((1, n_cmp, KV_WIDTH), lambda b, i: (b, 0, 0))
    return pl.pallas_call(
        _attention_kernel,
        grid=(batch, nq),
        in_specs=[pl.BlockSpec((tq, ATTN_WIDTH), lambda b, i: (b * nq + i, 0)),
                  cmp_blk, cmp_blk, grp, grp, grp, grp,
                  pl.BlockSpec((tq, GATE_PAD), lambda b, i: (b * nq + i, 0)),
                  pl.BlockSpec(ovl.shape, lambda b, i: (0, 0))],
        out_specs=pl.BlockSpec((tq, ATTN_WIDTH), lambda b, i: (b * nq + i, 0)),
        out_shape=jax.ShapeDtypeStruct((batch * seq, ATTN_WIDTH), BF16),
        scratch_shapes=[pltpu.VMEM((GQA_GROUP * tq, 1), F32), pltpu.VMEM((GQA_GROUP * tq, LANES), F32)],
        compiler_params=pltpu.CompilerParams(dimension_semantics=("parallel", "parallel"),
                                             vmem_limit_bytes=VMEM_LIMIT),
        name="attention",
    )(q, kc, vc, ksa, vsa, kwa, vwa, gates, ovl)


def _ssm_prep_kernel(a_re_ref, a_im_ref, log_dt_ref, b_re_ref, b_im_ref,
                     lam_re_ref, lam_im_ref, bb_re_ref, bb_im_ref):
    a_re = a_re_ref[...]
    a_im = a_im_ref[...]
    dt = jnp.exp(log_dt_ref[...])
    mag = jnp.exp(a_re * dt)
    lam_re = mag * jnp.cos(a_im * dt)
    lam_im = mag * jnp.sin(a_im * dt)
    den = a_re * a_re + a_im * a_im
    nr, ni = lam_re - 1.0, lam_im
    f_re = (nr * a_re + ni * a_im) / den
    f_im = (ni * a_re - nr * a_im) / den
    lam_re_ref[...] = lam_re
    lam_im_ref[...] = lam_im
    b_re = b_re_ref[...]
    b_im = b_im_ref[...]
    bb_re_ref[...] = f_re[:, None, :] * b_re - f_im[:, None, :] * b_im
    bb_im_ref[...] = f_re[:, None, :] * b_im + f_im[:, None, :] * b_re


def _ssm_prep(a_re, a_im, log_dt, b_re_t, b_im_t):
    gp = jax.ShapeDtypeStruct(a_re.shape, F32)
    gcp = jax.ShapeDtypeStruct(b_re_t.shape, F32)
    return pl.pallas_call(_ssm_prep_kernel, out_shape=[gp, gp, gcp, gcp], name="ssm_prep")(
        a_re, a_im, log_dt, b_re_t, b_im_t)


def _ssm_kernel(u_ref, wb_ref, lam_re_ref, lam_im_ref, wc_ref, d_ref, wglu_ref, y_ref, st_ref, xr_ref, xi_ref,
                *, batch):
    n_st = SSM_STATES_ALL
    steps = u_ref.shape[0] // batch

    @pl.when(pl.program_id(0) == 0)
    def _():
        xr_ref[...] = jnp.zeros(xr_ref.shape, F32)
        xi_ref[...] = jnp.zeros(xi_ref.shape, F32)

    u = u_ref[...]
    st_ref[...] = _dot(u, wb_ref[...])

    def step(t, carry):
        xr, xi = carry
        r0 = pl.multiple_of(t * batch, batch)
        lr = lam_re_ref[...]
        li = lam_im_ref[...]
        nr = lr * xr - li * xi + st_ref[pl.ds(r0, batch), 0:n_st]
        ni = lr * xi + li * xr + st_ref[pl.ds(r0, batch), n_st:2 * n_st]
        st_ref[pl.ds(r0, batch), 0:n_st] = nr
        st_ref[pl.ds(r0, batch), n_st:2 * n_st] = ni
        return nr, ni

    xr, xi = lax.fori_loop(0, steps, step, (xr_ref[...], xi_ref[...]))
    xr_ref[...] = xr
    xi_ref[...] = xi

    y = _dot(st_ref[...].astype(BF16), wc_ref[...]) + d_ref[...] * u.astype(F32)
    z = _dot(_gelu_tanh(y).astype(BF16), wglu_ref[...])
    y_ref[...] = (z[:, 0:SSM_WIDTH] * _sigmoid(z[:, SSM_WIDTH:2 * SSM_WIDTH])).astype(BF16)


def _ssm(u_tb, wb, lam_re_b, lam_im_b, wc, d_skip, w_glu, batch, seq):
    rows = SSM_LC * batch
    full = lambda a: pl.BlockSpec(a.shape, lambda i: (0,) * a.ndim)
    return pl.pallas_call(
        functools.partial(_ssm_kernel, batch=batch),
        grid=(seq // SSM_LC,),
        in_specs=[pl.BlockSpec((rows, SSM_WIDTH), lambda i: (i, 0)),
                  full(wb), full(lam_re_b), full(lam_im_b), full(wc), full(d_skip), full(w_glu)],
        out_specs=pl.BlockSpec((rows, SSM_WIDTH), lambda i: (i, 0)),
        out_shape=jax.ShapeDtypeStruct((seq * batch, SSM_WIDTH), BF16),
        scratch_shapes=[pltpu.VMEM((rows, 2 * SSM_STATES_ALL), F32),
                        pltpu.VMEM((batch, SSM_STATES_ALL), F32),
                        pltpu.VMEM((batch, SSM_STATES_ALL), F32)],
        compiler_params=pltpu.CompilerParams(dimension_semantics=("arbitrary",), vmem_limit_bytes=VMEM_LIMIT),
        name="ssm",
    )(u_tb, wb, lam_re_b, lam_im_b, wc, d_skip, w_glu)


def _conv_kernel(u_ref, wdw_ref, bdw_ref, lng_ref, lnb_ref, wpw_ref, bpw_ref, y_ref, pad_ref):
    seq = u_ref.shape[0]
    pad_ref[0:CONV_PAD, :] = jnp.zeros((CONV_PAD, CONV_WIDTH), F32)
    pad_ref[CONV_PAD:CONV_PAD + seq, :] = u_ref[...]
    wdw = wdw_ref[...]

    def tile(i, carry):
        t0 = pl.multiple_of(i * CONV_TC, CONV_TC)
        win_rows = CONV_TC + CONV_PAD
        win = pad_ref[pl.ds(t0, win_rows), :]
        acc = jnp.zeros((CONV_TC, CONV_WIDTH), F32)
        for b in range(SUBLANES):
            rolled = win if b == 0 else pltpu.roll(win, win_rows - b, 0)
            for a in range(win_rows // SUBLANES):
                k = a * SUBLANES + b - (CONV_PAD - (CONV_K - 1))
                if 0 <= k < CONV_K:
                    acc = acc + wdw[k:k + 1, :] * rolled[a * SUBLANES:a * SUBLANES + CONV_TC]
        y = acc + bdw_ref[...]
        mu = jnp.mean(y, axis=-1, keepdims=True)
        yc = y - mu
        var = jnp.mean(yc * yc, axis=-1, keepdims=True)
        yn = yc * lax.rsqrt(var + EPS) * lng_ref[...] + lnb_ref[...]
        act = yn * _sigmoid(yn)
        y_ref[pl.ds(t0, CONV_TC), :] = (_dot(act.astype(BF16), wpw_ref[...]) + bpw_ref[...]).astype(BF16)
        return carry

    lax.fori_loop(0, seq // CONV_TC, tile, 0)


def _conv(u, wdw, bdw, lng, lnb, wpw, bpw, batch, seq):
    full = lambda a: pl.BlockSpec(a.shape, lambda b: (0,) * a.ndim)
    return pl.pallas_call(
        _conv_kernel,
        grid=(batch,),
        in_specs=[pl.BlockSpec((seq, CONV_WIDTH), lambda b: (b, 0)),
                  full(wdw), full(bdw), full(lng), full(lnb), full(wpw), full(bpw)],
        out_specs=pl.BlockSpec((seq, CONV_WIDTH), lambda b: (b, 0)),
        out_shape=jax.ShapeDtypeStruct((batch * seq, CONV_WIDTH), BF16),
        scratch_shapes=[pltpu.VMEM((CONV_PAD + seq, CONV_WIDTH), F32)],
        compiler_params=pltpu.CompilerParams(dimension_semantics=("parallel",), vmem_limit_bytes=VMEM_LIMIT),
        name="conv",
    )(u, wdw, bdw, lng, lnb, wpw, bpw)


def _out_ffn_kernel(x_ref, ya_ref, ys_ref, yc_ref, ga_ref, gs_ref, gc_ref, wo_ref, gf_ref,
                    wg_ref, wu_ref, wd_ref, gfin_ref, o_ref, *, final_norm):
    na = _rms(ya_ref[...].astype(F32), ga_ref[...]).astype(BF16)
    ns = _rms(ys_ref[...].astype(F32), gs_ref[...]).astype(BF16)
    nc = _rms(yc_ref[...].astype(F32), gc_ref[...]).astype(BF16)
    o1, o2 = ATTN_WIDTH, ATTN_WIDTH + SSM_WIDTH
    x = x_ref[...] + _dot(na, wo_ref[0:o1, :]) + _dot(ns, wo_ref[o1:o2, :]) + _dot(nc, wo_ref[o2:, :])
    o_ref[...] = x
    h = _rms(x, gf_ref[...]).astype(BF16)
    gate = _dot(h, wg_ref[...])
    up = _dot(h, wu_ref[...])
    x = o_ref[...] + _dot((gate * _sigmoid(gate) * up).astype(BF16), wd_ref[...])
    if final_norm:
        x = _rms(x, gfin_ref[...])
    o_ref[...] = x


def _out_ffn(x2d, y_attn, y_ssm_tm, y_conv, ga, gs, gc, wo, gf, wg, wu, wd, gfin, batch, seq, final_norm):
    tm = TM_PROJ
    tiles_per_seq = seq // tm
    tokens = batch * seq
    row = lambda n: pl.BlockSpec((tm, n), lambda i: (i, 0))
    resident = lambda a: pl.BlockSpec(a.shape, lambda i: (0,) * a.ndim)
    return pl.pallas_call(
        functools.partial(_out_ffn_kernel, final_norm=final_norm),
        grid=(tokens // tm,),
        in_specs=[row(D_MODEL), row(ATTN_WIDTH),
                  pl.BlockSpec((tm, SSM_WIDTH), lambda i: (i % tiles_per_seq, i // tiles_per_seq)),
                  row(CONV_WIDTH),
                  resident(ga), resident(gs), resident(gc), resident(wo), resident(gf),
                  resident(wg), resident(wu), resident(wd), resident(gfin)],
        out_specs=row(D_MODEL),
        out_shape=jax.ShapeDtypeStruct((tokens, D_MODEL), F32),
        compiler_params=pltpu.CompilerParams(dimension_semantics=("parallel",), vmem_limit_bytes=VMEM_LIMIT),
        name="out_ffn",
    )(x2d, y_attn, y_ssm_tm, y_conv, ga, gs, gc, wo, gf, wg, wu, wd, gfin)


def _rope_tables(pos):
    half = HEAD_DIM // 2
    freqs = ROPE_THETA ** (-jnp.arange(half, dtype=F32) / half)
    ang = pos.astype(F32)[:, None] * freqs[None, :]
    cos = jnp.tile(jnp.cos(ang), (1, LANES // half))
    sin = jnp.sin(ang)
    sin_signed = jnp.tile(jnp.concatenate([-sin, sin], axis=1), (1, LANES // HEAD_DIM))
    return cos, sin_signed


def _pack_w_in(w_in):
    sizes = (ATTN_WIDTH,) + (KV_WIDTH,) * 6 + (N_BRANCH * N_Q_HEADS, SSM_WIDTH, 2 * CONV_WIDTH)
    parts = jnp.split(w_in, np.cumsum(sizes)[:-1].tolist(), axis=1)
    parts[7] = jnp.pad(parts[7], ((0, 0), (0, GATE_PAD - N_BRANCH * N_Q_HEADS)))
    return jnp.concatenate(parts, axis=1).astype(BF16)


def _pack_cmp_w1(w1):
    w = w1.reshape(2, CMP_STRIDE, HEAD_DIM, CMP_HIDDEN)
    eye = jnp.eye(N_KV_HEADS, dtype=w1.dtype)
    big = jnp.einsum('rpdh,ab->padrbh', w, eye)
    return big.reshape(CMP_STRIDE * N_KV_HEADS * HEAD_DIM, 2 * N_KV_HEADS * CMP_HIDDEN).astype(BF16)


def _pe_operand(pe):
    lo = jnp.tile(pe[:CMP_STRIDE, None, :], (1, N_KV_HEADS, 1)).reshape(1, -1)
    hi = jnp.tile(pe[CMP_STRIDE:, None, :], (1, N_KV_HEADS, 1)).reshape(1, -1)
    return jnp.concatenate([jnp.tile(lo, (4, 1)), jnp.tile(hi, (4, 1))], axis=0).astype(BF16)


def _block_diag(blocks):
    g, a, b = blocks.shape
    eye = jnp.eye(g, dtype=blocks.dtype)
    return jnp.einsum('gab,gh->gahb', blocks, eye).reshape(g * a, g * b)


def _overlap_matrix(seq):
    n_cmp = seq // CMP_STRIDE
    n_sel = seq // SLC_BLOCK
    ci = np.arange(n_cmp)[None, :]
    sj = np.arange(n_sel)[:, None]
    ovl = (ci * CMP_STRIDE < (sj + 1) * SLC_BLOCK) & (ci * CMP_STRIDE + CMP_LEN > sj * SLC_BLOCK) & (ci < n_cmp - 1)
    return jnp.asarray(ovl.astype(np.float32), dtype=BF16)


def _layer(x2d, p, batch, seq, tables, final_norm, norm_final):
    cos, sin, cos_c, sin_c, ovl = tables
    row = lambda v: v.reshape(1, -1).astype(F32)

    q, kc_raw, vc_raw, ksa, vsa, kwa, vwa, gates, u_ssm, u_conv = _in_proj(
        x2d, row(p['norm_mix']), _pack_w_in(p['w_in']), cos, sin, batch, seq)

    n_half = seq // CMP_STRIDE
    kh = kc_raw.reshape(batch, n_half, CMP_STRIDE * KV_WIDTH)
    vh = vc_raw.reshape(batch, n_half, CMP_STRIDE * KV_WIDTH)

    wk = _pack_cmp_w1(p['cmp_k_w1'])
    wv = _pack_cmp_w1(p['cmp_v_w1'])
    pek = _pe_operand(p['cmp_pe_k'])
    pev = _pe_operand(p['cmp_pe_v'])
    kc, vc = _compress(kh, vh, wk, wv, pek, pev, p['cmp_k_w2'].astype(BF16), p['cmp_v_w2'].astype(BF16),
                       cos_c, sin_c)

    y_attn = _attention(q, kc, vc, ksa, vsa, kwa, vwa, gates, ovl, batch, seq)

    lam_re, lam_im, bb_re, bb_im = _ssm_prep(
        p['ssm_a_re'], p['ssm_a_im'], p['ssm_log_dt'].reshape(-1, 1),
        p['ssm_b_re'].transpose(0, 2, 1), p['ssm_b_im'].transpose(0, 2, 1))
    wb = jnp.concatenate([_block_diag(bb_re), _block_diag(bb_im)], axis=1).astype(BF16)
    wc = jnp.concatenate([_block_diag(p['ssm_c_re'].transpose(0, 2, 1)),
                          -_block_diag(p['ssm_c_im'].transpose(0, 2, 1))], axis=0).astype(BF16)
    lam_re_b = jnp.broadcast_to(lam_re.reshape(1, -1), (batch, SSM_STATES_ALL))
    lam_im_b = jnp.broadcast_to(lam_im.reshape(1, -1), (batch, SSM_STATES_ALL))
    y_ssm = _ssm(u_ssm.reshape(seq * batch, SSM_WIDTH), wb, lam_re_b, lam_im_b, wc, row(p['ssm_d']),
                 p['ssm_w_glu'].astype(BF16), batch, seq)
    y_ssm_tm = y_ssm.reshape(seq, batch * SSM_WIDTH)

    wdw = jnp.pad(p['conv_w_dw'], ((0, CONV_PAD - CONV_K), (0, 0)))
    y_conv = _conv(u_conv, wdw, row(p['conv_b_dw']), row(p['conv_ln_g']), row(p['conv_ln_b']),
                   p['conv_w_pw'].astype(BF16), row(p['conv_b_pw']), batch, seq)

    return _out_ffn(x2d, y_attn, y_ssm_tm, y_conv, row(p['norm_out_attn']), row(p['norm_out_ssm']),
                    row(p['norm_out_conv']), p['w_out'].astype(BF16), row(p['norm_ffn']),
                    p['w_gate'].astype(BF16), p['w_up'].astype(BF16), p['w_down'].astype(BF16),
                    row(norm_final), batch, seq, final_norm)


_LAYER_KEYS = ('norm_mix', 'w_in', 'cmp_pe_k', 'cmp_k_w1', 'cmp_k_w2', 'cmp_pe_v', 'cmp_v_w1', 'cmp_v_w2',
               'ssm_a_re', 'ssm_a_im', 'ssm_log_dt', 'ssm_b_re', 'ssm_b_im', 'ssm_c_re', 'ssm_c_im', 'ssm_d',
               'ssm_w_glu', 'conv_w_dw', 'conv_b_dw', 'conv_ln_g', 'conv_ln_b', 'conv_w_pw', 'conv_b_pw',
               'norm_out_attn', 'norm_out_ssm', 'norm_out_conv', 'w_out', 'norm_ffn', 'w_gate', 'w_up', 'w_down')


def kernel(x, norm_mix, w_in, cmp_pe_k, cmp_k_w1, cmp_k_w2, cmp_pe_v, cmp_v_w1, cmp_v_w2, ssm_a_re, ssm_a_im, ssm_log_dt, ssm_b_re, ssm_b_im, ssm_c_re, ssm_c_im, ssm_d, ssm_w_glu, conv_w_dw, conv_b_dw, conv_ln_g, conv_ln_b, conv_w_pw, conv_b_pw, norm_out_attn, norm_out_ssm, norm_out_conv, w_out, norm_ffn, w_gate, w_up, w_down, norm_final):
    stacked = dict(zip(_LAYER_KEYS, (norm_mix, w_in, cmp_pe_k, cmp_k_w1, cmp_k_w2, cmp_pe_v, cmp_v_w1, cmp_v_w2,
                                     ssm_a_re, ssm_a_im, ssm_log_dt, ssm_b_re, ssm_b_im, ssm_c_re, ssm_c_im, ssm_d,
                                     ssm_w_glu, conv_w_dw, conv_b_dw, conv_ln_g, conv_ln_b, conv_w_pw, conv_b_pw,
                                     norm_out_attn, norm_out_ssm, norm_out_conv, w_out, norm_ffn, w_gate, w_up,
                                     w_down)))
    batch, seq, _ = x.shape
    depth = norm_mix.shape[0]
    cos, sin = _rope_tables(jnp.arange(seq))
    cos_c, sin_c = _rope_tables(jnp.arange(seq // CMP_STRIDE) * CMP_STRIDE + CMP_LEN - 1)
    tables = (cos, sin, cos_c, sin_c, _overlap_matrix(seq))
    x2d = x.reshape(batch * seq, D_MODEL)
    for l in range(depth):
        p = {k: v[l] for k, v in stacked.items()}
        x2d = _layer(x2d, p, batch, seq, tables, l == depth - 1, norm_final)
    return x2d.reshape(batch, seq, D_MODEL)
```

```python
import functools
import math

import numpy as np
import jax
import jax.numpy as jnp
from jax import lax
from jax.experimental import pallas as pl
from jax.experimental.pallas import tpu as pltpu

F32 = jnp.float32
BF16 = jnp.bfloat16

D_MODEL = 1024
HEAD_DIM = 64
N_Q_HEADS = 8
N_KV_HEADS = 2
GQA_GROUP = N_Q_HEADS // N_KV_HEADS
ATTN_WIDTH = N_Q_HEADS * HEAD_DIM
KV_WIDTH = N_KV_HEADS * HEAD_DIM
N_BRANCH = 3
SSM_WIDTH = D_MODEL // 4
SSM_GROUP = 16
SSM_N_GROUPS = SSM_WIDTH // SSM_GROUP
SSM_STATE = 64
SSM_STATES_ALL = SSM_N_GROUPS * SSM_STATE
CONV_WIDTH = D_MODEL // 4
CONV_K = 31
CMP_LEN = 32
CMP_STRIDE = 16
CMP_HIDDEN = 256
SLC_BLOCK = 64
SLC_TOPN = 16
WINDOW = 512
ROPE_THETA = 10000.0
D_FF = ((8 * D_MODEL // 3 + 255) // 256) * 256
EPS = 1e-6
NEG_INF = -1e30
FORCE = 1e9

LANES = 128
SUBLANES = 8
VMEM_LIMIT = 56 * 1024 * 1024

GATE_PAD = LANES
C_Q = 0
C_KC = C_Q + ATTN_WIDTH
C_VC = C_KC + KV_WIDTH
C_KS = C_VC + KV_WIDTH
C_VS = C_KS + KV_WIDTH
C_KW = C_VS + KV_WIDTH
C_VW = C_KW + KV_WIDTH
C_GATE = C_VW + KV_WIDTH
C_SSM = C_GATE + GATE_PAD
C_CONV = C_SSM + SSM_WIDTH
IN_PACKED = C_CONV + 2 * CONV_WIDTH

TM_PROJ = 512
TQ = 128
CK = 256
RANK_GROUP = 16
SSM_LC = 64
CONV_TC = 128
CONV_PAD = 32


def _gelu_tanh(x):
    return 0.5 * x * (1.0 + jnp.tanh(math.sqrt(2.0 / math.pi) * (x + 0.044715 * (x * x * x))))


def _sigmoid(x):
    return 1.0 / (1.0 + jnp.exp(-x))


def _rms(x, g):
    return x * lax.rsqrt(jnp.mean(x * x, axis=-1, keepdims=True) + EPS) * g


def _dot(a, b):
    return jnp.dot(a, b, preferred_element_type=F32)


def _dot_nt(a, b):
    return lax.dot_general(a, b, (((1,), (1,)), ((), ())), preferred_element_type=F32)


def _rope_rotate(v, cos, sin_signed):
    lane = lax.broadcasted_iota(jnp.int32, v.shape, 1)
    first_half = (lane % HEAD_DIM) < (HEAD_DIM // 2)
    rot = jnp.where(first_half, pltpu.roll(v, LANES - HEAD_DIM // 2, 1), pltpu.roll(v, HEAD_DIM // 2, 1))
    return v * cos + rot * sin_signed


def _in_proj_kernel(x_ref, g_ref, w_ref, cos_ref, sin_ref,
                    q_ref, kc_ref, vc_ref, ksa_ref, vsa_ref, kwa_ref, vwa_ref, gate_ref, ussm_ref, uconv_ref,
                    *, tiles_per_seq):
    tm = x_ref.shape[0]
    h = _rms(x_ref[...], g_ref[...]).astype(BF16)
    cos = cos_ref[...]
    sin = sin_ref[...]

    def proj(c0, n):
        return _dot(h, w_ref[:, c0:c0 + n])

    lane = lax.broadcasted_iota(jnp.int32, (tm, LANES), 1)
    row = lax.broadcasted_iota(jnp.int32, (tm, LANES), 0)
    t = (pl.program_id(0) % tiles_per_seq) * tm + row
    low = lane < HEAD_DIM
    for j in range(ATTN_WIDTH // LANES):
        qj = _rope_rotate(proj(C_Q + j * LANES, LANES), cos, sin) * (HEAD_DIM ** -0.5)
        q_ref[:, (2 * j) * LANES:(2 * j + 1) * LANES] = jnp.where(low, qj, 0.0).astype(BF16)
        q_ref[:, (2 * j + 1) * LANES:(2 * j + 2) * LANES] = jnp.where(low, pltpu.roll(qj, HEAD_DIM, 1), 0.0).astype(BF16)
    kc_ref[...] = proj(C_KC, KV_WIDTH).astype(BF16)
    vc_ref[...] = proj(C_VC, KV_WIDTH).astype(BF16)
    blk_onehot = jnp.where((t // SLC_BLOCK) == (lane - HEAD_DIM), 1.0, 0.0)
    ones_up = jnp.where(low, 0.0, 1.0)

    def split_groups(v, upper, ref):
        ref[0, 0] = jnp.where(low, v, upper).astype(BF16)
        ref[0, 1] = jnp.where(low, pltpu.roll(v, HEAD_DIM, 1), upper).astype(BF16)

    split_groups(_rope_rotate(proj(C_KS, KV_WIDTH), cos, sin), blk_onehot, ksa_ref)
    split_groups(proj(C_VS, KV_WIDTH), ones_up, vsa_ref)
    split_groups(_rope_rotate(proj(C_KW, KV_WIDTH), cos, sin), 0.0, kwa_ref)
    split_groups(proj(C_VW, KV_WIDTH), ones_up, vwa_ref)

    gate_ref[...] = _sigmoid(proj(C_GATE, GATE_PAD))
    ussm_ref[...] = proj(C_SSM, SSM_WIDTH).astype(BF16)
    a = proj(C_CONV, CONV_WIDTH)
    g = proj(C_CONV + CONV_WIDTH, CONV_WIDTH)
    uconv_ref[...] = a * _sigmoid(g)


def _in_proj(x2d, g, w_packed, cos, sin, batch, seq):
    tm = TM_PROJ
    tiles_per_seq = seq // tm
    tokens = batch * seq
    row_blk = lambda n, dt: (pl.BlockSpec((tm, n), lambda i: (i, 0)), jax.ShapeDtypeStruct((tokens, n), dt))
    grp_blk = (pl.BlockSpec((1, N_KV_HEADS, tm, LANES), lambda i: (i // tiles_per_seq, 0, i % tiles_per_seq, 0)),
               jax.ShapeDtypeStruct((batch, N_KV_HEADS, seq, LANES), BF16))
    outs = [row_blk(N_Q_HEADS * LANES, BF16), row_blk(KV_WIDTH, BF16), row_blk(KV_WIDTH, BF16),
            grp_blk, grp_blk, grp_blk, grp_blk, row_blk(GATE_PAD, F32),
            (pl.BlockSpec((tm, SSM_WIDTH), lambda i: (i % tiles_per_seq, i // tiles_per_seq)),
             jax.ShapeDtypeStruct((seq, batch * SSM_WIDTH), BF16)),
            row_blk(CONV_WIDTH, F32)]
    return pl.pallas_call(
        functools.partial(_in_proj_kernel, tiles_per_seq=tiles_per_seq),
        grid=(tokens // tm,),
        in_specs=[pl.BlockSpec((tm, D_MODEL), lambda i: (i, 0)),
                  pl.BlockSpec((1, D_MODEL), lambda i: (0, 0)),
                  pl.BlockSpec((D_MODEL, IN_PACKED), lambda i: (0, 0)),
                  pl.BlockSpec((tm, LANES), lambda i: (i % tiles_per_seq, 0)),
                  pl.BlockSpec((tm, LANES), lambda i: (i % tiles_per_seq, 0))],
        out_specs=[o[0] for o in outs],
        out_shape=[o[1] for o in outs],
        compiler_params=pltpu.CompilerParams(dimension_semantics=("parallel",), vmem_limit_bytes=VMEM_LIMIT),
        name="in_proj",
    )(x2d, g, w_packed, cos, sin)


def _compress_kernel(kh_ref, vh_ref, wk_ref, wv_ref, pek_ref, pev_ref, w2k_ref, w2v_ref, cos_ref, sin_ref,
                     kca_ref, vca_ref):
    n_half = kh_ref.shape[1]
    hw = N_KV_HEADS * CMP_HIDDEN

    def compress(h_ref, w_ref, pe_ref, w2_ref):
        ab = _dot(h_ref[0], w_ref[...])
        pe = _dot(pe_ref[...], w_ref[...])
        bias = pe[0:1, 0:hw] + pe[4:5, hw:2 * hw]
        nxt = pltpu.roll(ab[:, hw:2 * hw], n_half - 1, 0)
        hid = _gelu_tanh(ab[:, 0:hw] + nxt + bias).astype(BF16)
        return jnp.concatenate([_dot(hid[:, j * CMP_HIDDEN:(j + 1) * CMP_HIDDEN], w2_ref[...])
                                for j in range(N_KV_HEADS)], axis=1)

    def split_groups(v, upper, ref):
        low = lax.broadcasted_iota(jnp.int32, v.shape, 1) < HEAD_DIM
        ref[0, 0] = jnp.where(low, v, upper).astype(BF16)
        ref[0, 1] = jnp.where(low, pltpu.roll(v, HEAD_DIM, 1), upper).astype(BF16)

    kc = compress(kh_ref, wk_ref, pek_ref, w2k_ref)
    split_groups(_rope_rotate(kc, cos_ref[...], sin_ref[...]), 0.0, kca_ref)
    split_groups(compress(vh_ref, wv_ref, pev_ref, w2v_ref), 1.0, vca_ref)


def _compress(kh, vh, wk, wv, pek, pev, w2k, w2v, cos_c, sin_c):
    batch, n_half, width = kh.shape
    full = lambda a: pl.BlockSpec(a.shape, lambda b: (0,) * a.ndim)
    per_b = pl.BlockSpec((1, n_half, width), lambda b: (b, 0, 0))
    out_blk = pl.BlockSpec((1, N_KV_HEADS, n_half, LANES), lambda b: (b, 0, 0, 0))
    out_shape = jax.ShapeDtypeStruct((batch, N_KV_HEADS, n_half, LANES), BF16)
    return pl.pallas_call(
        _compress_kernel,
        grid=(batch,),
        in_specs=[per_b, per_b, full(wk), full(wv), full(pek), full(pev), full(w2k), full(w2v),
                  full(cos_c), full(sin_c)],
        out_specs=[out_blk, out_blk],
        out_shape=[out_shape, out_shape],
        compiler_params=pltpu.CompilerParams(dimension_semantics=("parallel",), vmem_limit_bytes=VMEM_LIMIT),
        name="compress",
    )(kh, vh, wk, wv, pek, pev, w2k, w2v, cos_c, sin_c)


def _attention_kernel(q_ref, kca_ref, vca_ref, ksa_ref, vsa_ref, kwa_ref, vwa_ref, gate_ref, ovl_ref, gexp_ref,
                      y_ref,
                      qaug_ref, oc_ref, s_ref, mx_ref, m_ref, acc_ref, rank_ref):
    tq = q_ref.shape[0]
    rows = GQA_GROUP * tq
    n_cmp = kca_ref.shape[2]
    seq = ksa_ref.shape[2]
    n_sel = seq // SLC_BLOCK
    t0 = pl.program_id(1) * tq
    row_q = lax.broadcasted_iota(jnp.int32, (rows, 1), 0) % tq
    t_row = t0 + row_q
    t_rep = t0 + lax.broadcasted_iota(jnp.int32, (rows, LANES), 0) % tq
    lane_iota = lax.broadcasted_iota(jnp.int32, (rows, LANES), 1)
    last_blk = (t0 + tq - 1) // SLC_BLOCK

    def row_max_rep(x):
        return jnp.broadcast_to(jnp.max(x, axis=-1, keepdims=True), x.shape)

    def sums_on_all_lanes(a):
        return jnp.where(lane_iota < HEAD_DIM, pltpu.roll(a, HEAD_DIM, 1), a)

    for g in range(N_KV_HEADS):
        heads = [q_ref[:, (g * GQA_GROUP + r) * LANES:(g * GQA_GROUP + r + 1) * LANES] for r in range(GQA_GROUP)]
        q_rows = jnp.concatenate(heads, axis=0)

        sc = _dot_nt(q_rows, kca_ref[0, g])
        slabs = []
        for j in range(n_cmp // LANES):
            cmp_end = (lane_iota + j * LANES) * CMP_STRIDE + (CMP_LEN - 1)
            slabs.append(jnp.where(cmp_end <= t_rep, sc[:, j * LANES:(j + 1) * LANES], NEG_INF))
        mc = slabs[0]
        for sj in slabs[1:]:
            mc = jnp.maximum(mc, sj)
        mc = row_max_rep(mc)
        pc = jnp.concatenate([jnp.exp(sj - mc) for sj in slabs], axis=1)
        oc = _dot(pc.astype(BF16), vca_ref[0, g])
        inv = jnp.where(t_rep >= CMP_LEN - 1, 1.0 / jnp.maximum(sums_on_all_lanes(oc), 1e-30), 0.0)
        oc_ref[g] = oc * inv
        pc = pc * jnp.concatenate([inv] * (n_cmp // LANES), axis=1)

        pc_sum = pc[0:tq]
        for r in range(1, GQA_GROUP):
            pc_sum = pc_sum + pc[r * tq:(r + 1) * tq]
        p_hi = pc_sum.astype(BF16)
        rem = pc_sum - p_hi.astype(F32)
        p_mid = rem.astype(BF16)
        p_lo = (rem - p_mid.astype(F32)).astype(BF16)
        ovl = ovl_ref[...]
        imp = _dot_nt(ovl, p_hi) + _dot_nt(ovl, p_mid) + _dot_nt(ovl, p_lo)

        blk = lax.broadcasted_iota(jnp.int32, (n_sel, tq), 0)
        cur = (t0 + lax.broadcasted_iota(jnp.int32, (n_sel, tq), 1)) // SLC_BLOCK
        causal = blk <= cur
        val = jnp.where(blk == 0, FORCE, jnp.where(blk >= cur - 1, FORCE, imp))
        val = jnp.where(causal, val, -FORCE)

        rank_ref[...] = jnp.zeros(rank_ref.shape, jnp.int32)
        grp_rows = lax.broadcasted_iota(jnp.int32, (RANK_GROUP, tq), 0)
        for ig in range(n_sel // RANK_GROUP):
            for jg in range(n_sel // RANK_GROUP):
                @pl.when(max(ig, jg) * RANK_GROUP <= last_blk)
                def _(ig=ig, jg=jg):
                    vj = val[jg * RANK_GROUP:(jg + 1) * RANK_GROUP]
                    cnt = jnp.zeros((RANK_GROUP, tq), jnp.int32)
                    for ii in range(RANK_GROUP):
                        i = ig * RANK_GROUP + ii
                        vi = val[i:i + 1, :]
                        if jg > ig:
                            cnt = cnt + jnp.where(vi >= vj, 1, 0)
                        elif jg < ig:
                            cnt = cnt + jnp.where(vi > vj, 1, 0)
                        else:
                            tie = jnp.where(grp_rows > ii, 1, 0)
                            cnt = cnt + jnp.where(vi > vj, 1, jnp.where(vi == vj, tie, 0))
                    sl = pl.ds(jg * RANK_GROUP, RANK_GROUP)
                    rank_ref[sl, :] = rank_ref[sl, :] + cnt
        bias_t = jnp.where(causal, jnp.where(rank_ref[...] < SLC_TOPN, 0.0, NEG_INF), NEG_INF)
        pieces = [jnp.zeros((HEAD_DIM, tq), F32), bias_t]
        if n_sel < LANES - HEAD_DIM:
            pieces.append(jnp.zeros((LANES - HEAD_DIM - n_sel, tq), F32))
        bias = jnp.concatenate(pieces, axis=0).T.astype(BF16)
        qaug_ref[g] = q_rows + jnp.concatenate([bias] * GQA_GROUP, axis=0)

    n_pairs = (t0 + tq + 2 * CK - 1) // (2 * CK)
    mx_ref[...] = jnp.full(mx_ref.shape, NEG_INF, F32)
    acc_ref[...] = jnp.zeros(acc_ref.shape, F32)

    def score_chunk(g, c):
        k0 = pl.multiple_of(c * CK, CK)
        s = _dot_nt(qaug_ref[g], ksa_ref[0, g, pl.ds(k0, CK), :])
        mx = mx_ref[g]
        slabs = []
        for j in range(CK // LANES):
            sj = jnp.where(lane_iota <= t_rep - (k0 + j * LANES), s[:, j * LANES:(j + 1) * LANES], NEG_INF)
            mx = jnp.maximum(mx, sj)
            slabs.append(sj)
        s_ref[g, c] = jnp.concatenate(slabs, axis=1)
        mx_ref[g] = mx

    def value_chunk(g, c):
        k0 = pl.multiple_of(c * CK, CK)
        s = s_ref[g, c]
        m = m_ref[g]
        p = jnp.concatenate([jnp.exp(s[:, j * LANES:(j + 1) * LANES] - m) for j in range(CK // LANES)], axis=1)
        acc_ref[g] = acc_ref[g] + _dot(p.astype(BF16), vsa_ref[0, g, pl.ds(k0, CK), :])

    def loop(body):
        def wrapped(i, carry):
            body(2 * i)
            body(2 * i + 1)
            return carry
        lax.fori_loop(0, n_pairs, wrapped, 0)

    loop(lambda c: score_chunk(0, c))
    m_ref[0] = row_max_rep(mx_ref[0])

    def mixed(c):
        value_chunk(0, c)
        score_chunk(1, c)

    loop(mixed)
    m_ref[1] = row_max_rep(mx_ref[1])
    loop(lambda c: value_chunk(1, c))

    gates = gate_ref[...]
    g_hi = gates.astype(BF16)
    g_lo = (gates - g_hi.astype(F32)).astype(BF16)
    w0 = pl.multiple_of(jnp.maximum(t0 - WINDOW, 0), tq)
    n_win = (WINDOW + tq) // LANES
    lane = lax.broadcasted_iota(jnp.int32, (tq, LANES), 1)
    for g in range(N_KV_HEADS):
        sw = _dot_nt(qaug_ref[g], kwa_ref[0, g, pl.ds(w0, WINDOW + tq), :])
        slabs = []
        for j in range(n_win):
            rel = t_rep - (w0 + j * LANES)
            sj = sw[:, j * LANES:(j + 1) * LANES]
            sj = jnp.where(lane_iota <= rel, sj, NEG_INF)
            if j == 0:
                sj = jnp.where(lane_iota > rel - WINDOW, sj, NEG_INF)
            slabs.append(sj)
        mw = slabs[0]
        for sj in slabs[1:]:
            mw = jnp.maximum(mw, sj)
        mw = row_max_rep(mw)
        pw = jnp.concatenate([jnp.exp(sj - mw) for sj in slabs], axis=1)
        aw = _dot(pw.astype(BF16), vwa_ref[0, g, pl.ds(w0, WINDOW + tq), :])
        def gate_rep(branch):
            e = gexp_ref[branch, g]
            spread = _dot(g_hi, e) + _dot(g_lo, e)
            return jnp.concatenate([spread[:, r * LANES:(r + 1) * LANES] for r in range(GQA_GROUP)], axis=0)

        asel = acc_ref[g]
        o = (gate_rep(0) * oc_ref[g]
             + gate_rep(1) * asel / jnp.maximum(pltpu.roll(asel, HEAD_DIM, 1), 1e-30)
             + gate_rep(2) * aw / jnp.maximum(pltpu.roll(aw, HEAD_DIM, 1), 1e-30))
        for pair in range(GQA_GROUP // 2):
            lo = o[(2 * pair) * tq:(2 * pair + 1) * tq]
            hi = o[(2 * pair + 1) * tq:(2 * pair + 2) * tq]
            col = g * GQA_GROUP * HEAD_DIM + pair * LANES
            y_ref[:, col:col + LANES] = jnp.where(lane < HEAD_DIM, lo, pltpu.roll(hi, HEAD_DIM, 1)).astype(BF16)


def _attention(q, kca, vca, ksa, vsa, kwa, vwa, gates, ovl, gexp, batch, seq):
    tq = TQ
    nq = seq // tq
    n_cmp = kca.shape[2]
    rows = GQA_GROUP * tq
    grp = pl.BlockSpec((1, N_KV_HEADS, seq, LANES), lambda b, i: (b, 0, 0, 0))
    cmp_grp = pl.BlockSpec((1, N_KV_HEADS, n_cmp, LANES), lambda b, i: (b, 0, 0, 0))
    grp_rows = lambda dt: pltpu.VMEM((N_KV_HEADS, rows, LANES), dt)
    return pl.pallas_call(
        _attention_kernel,
        grid=(batch, nq),
        in_specs=[pl.BlockSpec((tq, N_Q_HEADS * LANES), lambda b, i: (b * nq + i, 0)),
                  cmp_grp, cmp_grp, grp, grp, grp, grp,
                  pl.BlockSpec((tq, GATE_PAD), lambda b, i: (b * nq + i, 0)),
                  pl.BlockSpec(ovl.shape, lambda b, i: (0, 0)),
                  pl.BlockSpec(gexp.shape, lambda b, i: (0, 0, 0, 0))],
        out_specs=pl.BlockSpec((tq, ATTN_WIDTH), lambda b, i: (b * nq + i, 0)),
        out_shape=jax.ShapeDtypeStruct((batch * seq, ATTN_WIDTH), BF16),
        scratch_shapes=[grp_rows(BF16),
                        grp_rows(F32),
                        pltpu.VMEM((N_KV_HEADS, seq // CK, rows, CK), F32),
                        grp_rows(F32),
                        grp_rows(F32),
                        grp_rows(F32),
                        pltpu.VMEM((seq // SLC_BLOCK, tq), jnp.int32)],
        compiler_params=pltpu.CompilerParams(dimension_semantics=("parallel", "parallel"),
                                             vmem_limit_bytes=VMEM_LIMIT),
        name="attention",
    )(q, kca, vca, ksa, vsa, kwa, vwa, gates, ovl, gexp)


def _ssm_prep_kernel(a_re_ref, a_im_ref, log_dt_ref, b_re_ref, b_im_ref,
                     lam_re_ref, lam_im_ref, bb_re_ref, bb_im_ref):
    a_re = a_re_ref[...]
    a_im = a_im_ref[...]
    dt = jnp.exp(log_dt_ref[...])
    mag = jnp.exp(a_re * dt)
    lam_re = mag * jnp.cos(a_im * dt)
    lam_im = mag * jnp.sin(a_im * dt)
    den = a_re * a_re + a_im * a_im
    nr, ni = lam_re - 1.0, lam_im
    f_re = (nr * a_re + ni * a_im) / den
    f_im = (ni * a_re - nr * a_im) / den
    lam_re_ref[...] = lam_re
    lam_im_ref[...] = lam_im
    b_re = b_re_ref[...]
    b_im = b_im_ref[...]
    bb_re_ref[...] = f_re[:, None, :] * b_re - f_im[:, None, :] * b_im
    bb_im_ref[...] = f_re[:, None, :] * b_im + f_im[:, None, :] * b_re


def _ssm_prep(a_re, a_im, log_dt, b_re_t, b_im_t):
    gp = jax.ShapeDtypeStruct(a_re.shape, F32)
    gcp = jax.ShapeDtypeStruct(b_re_t.shape, F32)
    return pl.pallas_call(_ssm_prep_kernel, out_shape=[gp, gp, gcp, gcp], name="ssm_prep")(
        a_re, a_im, log_dt, b_re_t, b_im_t)


def _ssm_kernel(u_ref, wb_ref, lam_re_ref, lam_im_ref, wc_ref, d_ref, wglu_ref, y_ref, st_ref, xr_ref, xi_ref,
                *, batch):
    n_st = SSM_STATES_ALL
    steps = u_ref.shape[0] // batch

    @pl.when(pl.program_id(0) == 0)
    def _():
        xr_ref[...] = jnp.zeros(xr_ref.shape, F32)
        xi_ref[...] = jnp.zeros(xi_ref.shape, F32)

    u = u_ref[...]
    st_ref[...] = _dot(u, wb_ref[...])

    def step(t, carry):
        xr, xi = carry
        r0 = pl.multiple_of(t * batch, batch)
        lr = lam_re_ref[...]
        li = lam_im_ref[...]
        nr = lr * xr - li * xi + st_ref[pl.ds(r0, batch), 0:n_st]
        ni = lr * xi + li * xr + st_ref[pl.ds(r0, batch), n_st:2 * n_st]
        st_ref[pl.ds(r0, batch), 0:n_st] = nr
        st_ref[pl.ds(r0, batch), n_st:2 * n_st] = ni
        return nr, ni

    xr, xi = lax.fori_loop(0, steps, step, (xr_ref[...], xi_ref[...]))
    xr_ref[...] = xr
    xi_ref[...] = xi

    y = _dot(st_ref[...].astype(BF16), wc_ref[...]) + d_ref[...] * u.astype(F32)
    z = _dot(_gelu_tanh(y).astype(BF16), wglu_ref[...])
    y_ref[...] = (z[:, 0:SSM_WIDTH] * _sigmoid(z[:, SSM_WIDTH:2 * SSM_WIDTH])).astype(BF16)


def _ssm(u_tb, wb, lam_re_b, lam_im_b, wc, d_skip, w_glu, batch, seq):
    rows = SSM_LC * batch
    full = lambda a: pl.BlockSpec(a.shape, lambda i: (0,) * a.ndim)
    return pl.pallas_call(
        functools.partial(_ssm_kernel, batch=batch),
        grid=(seq // SSM_LC,),
        in_specs=[pl.BlockSpec((rows, SSM_WIDTH), lambda i: (i, 0)),
                  full(wb), full(lam_re_b), full(lam_im_b), full(wc), full(d_skip), full(w_glu)],
        out_specs=pl.BlockSpec((rows, SSM_WIDTH), lambda i: (i, 0)),
        out_shape=jax.ShapeDtypeStruct((seq * batch, SSM_WIDTH), BF16),
        scratch_shapes=[pltpu.VMEM((rows, 2 * SSM_STATES_ALL), F32),
                        pltpu.VMEM((batch, SSM_STATES_ALL), F32),
                        pltpu.VMEM((batch, SSM_STATES_ALL), F32)],
        compiler_params=pltpu.CompilerParams(dimension_semantics=("arbitrary",), vmem_limit_bytes=VMEM_LIMIT),
        name="ssm",
    )(u_tb, wb, lam_re_b, lam_im_b, wc, d_skip, w_glu)


def _conv_kernel(u_ref, wdw_ref, bdw_ref, lng_ref, lnb_ref, wpw_ref, bpw_ref, y_ref, pad_ref):
    seq = u_ref.shape[0]
    pad_ref[0:CONV_PAD, :] = jnp.zeros((CONV_PAD, CONV_WIDTH), F32)
    pad_ref[CONV_PAD:CONV_PAD + seq, :] = u_ref[...]
    wdw = wdw_ref[...]

    def tile(i, carry):
        t0 = pl.multiple_of(i * CONV_TC, CONV_TC)
        win_rows = CONV_TC + CONV_PAD
        win = pad_ref[pl.ds(t0, win_rows), :]
        acc = jnp.zeros((CONV_TC, CONV_WIDTH), F32)
        for b in range(SUBLANES):
            rolled = win if b == 0 else pltpu.roll(win, win_rows - b, 0)
            for a in range(win_rows // SUBLANES):
                k = a * SUBLANES + b - (CONV_PAD - (CONV_K - 1))
                if 0 <= k < CONV_K:
                    acc = acc + wdw[k:k + 1, :] * rolled[a * SUBLANES:a * SUBLANES + CONV_TC]
        y = acc + bdw_ref[...]
        mu = jnp.mean(y, axis=-1, keepdims=True)
        yc = y - mu
        var = jnp.mean(yc * yc, axis=-1, keepdims=True)
        yn = yc * lax.rsqrt(var + EPS) * lng_ref[...] + lnb_ref[...]
        act = yn * _sigmoid(yn)
        y_ref[pl.ds(t0, CONV_TC), :] = (_dot(act.astype(BF16), wpw_ref[...]) + bpw_ref[...]).astype(BF16)
        return carry

    lax.fori_loop(0, seq // CONV_TC, tile, 0)


def _conv(u, wdw, bdw, lng, lnb, wpw, bpw, batch, seq):
    full = lambda a: pl.BlockSpec(a.shape, lambda b: (0,) * a.ndim)
    return pl.pallas_call(
        _conv_kernel,
        grid=(batch,),
        in_specs=[pl.BlockSpec((seq, CONV_WIDTH), lambda b: (b, 0)),
                  full(wdw), full(bdw), full(lng), full(lnb), full(wpw), full(bpw)],
        out_specs=pl.BlockSpec((seq, CONV_WIDTH), lambda b: (b, 0)),
        out_shape=jax.ShapeDtypeStruct((batch * seq, CONV_WIDTH), BF16),
        scratch_shapes=[pltpu.VMEM((CONV_PAD + seq, CONV_WIDTH), F32)],
        compiler_params=pltpu.CompilerParams(dimension_semantics=("parallel",), vmem_limit_bytes=VMEM_LIMIT),
        name="conv",
    )(u, wdw, bdw, lng, lnb, wpw, bpw)


def _out_ffn_kernel(x_ref, ya_ref, ys_ref, yc_ref, ga_ref, gs_ref, gc_ref, wo_ref, gf_ref,
                    wg_ref, wu_ref, wd_ref, gfin_ref, o_ref, *, final_norm):
    na = _rms(ya_ref[...].astype(F32), ga_ref[...]).astype(BF16)
    ns = _rms(ys_ref[...].astype(F32), gs_ref[...]).astype(BF16)
    nc = _rms(yc_ref[...].astype(F32), gc_ref[...]).astype(BF16)
    o1, o2 = ATTN_WIDTH, ATTN_WIDTH + SSM_WIDTH
    x = x_ref[...] + _dot(na, wo_ref[0:o1, :]) + _dot(ns, wo_ref[o1:o2, :]) + _dot(nc, wo_ref[o2:, :])
    o_ref[...] = x
    h = _rms(x, gf_ref[...]).astype(BF16)
    gate = _dot(h, wg_ref[...])
    up = _dot(h, wu_ref[...])
    x = o_ref[...] + _dot((gate * _sigmoid(gate) * up).astype(BF16), wd_ref[...])
    if final_norm:
        x = _rms(x, gfin_ref[...])
    o_ref[...] = x


def _out_ffn(x2d, y_attn, y_ssm_tm, y_conv, ga, gs, gc, wo, gf, wg, wu, wd, gfin, batch, seq, final_norm):
    tm = TM_PROJ
    tiles_per_seq = seq // tm
    tokens = batch * seq
    row = lambda n: pl.BlockSpec((tm, n), lambda i: (i, 0))
    resident = lambda a: pl.BlockSpec(a.shape, lambda i: (0,) * a.ndim)
    return pl.pallas_call(
        functools.partial(_out_ffn_kernel, final_norm=final_norm),
        grid=(tokens // tm,),
        in_specs=[row(D_MODEL), row(ATTN_WIDTH),
                  pl.BlockSpec((tm, SSM_WIDTH), lambda i: (i % tiles_per_seq, i // tiles_per_seq)),
                  row(CONV_WIDTH),
                  resident(ga), resident(gs), resident(gc), resident(wo), resident(gf),
                  resident(wg), resident(wu), resident(wd), resident(gfin)],
        out_specs=row(D_MODEL),
        out_shape=jax.ShapeDtypeStruct((tokens, D_MODEL), F32),
        compiler_params=pltpu.CompilerParams(dimension_semantics=("parallel",), vmem_limit_bytes=VMEM_LIMIT),
        name="out_ffn",
    )(x2d, y_attn, y_ssm_tm, y_conv, ga, gs, gc, wo, gf, wg, wu, wd, gfin)


def _rope_tables(pos):
    half = HEAD_DIM // 2
    freqs = ROPE_THETA ** (-jnp.arange(half, dtype=F32) / half)
    ang = pos.astype(F32)[:, None] * freqs[None, :]
    cos = jnp.tile(jnp.cos(ang), (1, LANES // half))
    sin = jnp.sin(ang)
    sin_signed = jnp.tile(jnp.concatenate([-sin, sin], axis=1), (1, LANES // HEAD_DIM))
    return cos, sin_signed


def _pack_w_in(w_in):
    sizes = (ATTN_WIDTH,) + (KV_WIDTH,) * 6 + (N_BRANCH * N_Q_HEADS, SSM_WIDTH, 2 * CONV_WIDTH)
    parts = jnp.split(w_in, np.cumsum(sizes)[:-1].tolist(), axis=1)
    parts[7] = jnp.pad(parts[7], ((0, 0), (0, GATE_PAD - N_BRANCH * N_Q_HEADS)))
    return jnp.concatenate(parts, axis=1).astype(BF16)


def _pack_cmp_w1(w1):
    w = w1.reshape(2, CMP_STRIDE, HEAD_DIM, CMP_HIDDEN)
    eye = jnp.eye(N_KV_HEADS, dtype=w1.dtype)
    big = jnp.einsum('rpdh,ab->padrbh', w, eye)
    return big.reshape(CMP_STRIDE * N_KV_HEADS * HEAD_DIM, 2 * N_KV_HEADS * CMP_HIDDEN).astype(BF16)


def _pe_operand(pe):
    lo = jnp.tile(pe[:CMP_STRIDE, None, :], (1, N_KV_HEADS, 1)).reshape(1, -1)
    hi = jnp.tile(pe[CMP_STRIDE:, None, :], (1, N_KV_HEADS, 1)).reshape(1, -1)
    return jnp.concatenate([jnp.tile(lo, (4, 1)), jnp.tile(hi, (4, 1))], axis=0).astype(BF16)


def _block_diag(blocks):
    g, a, b = blocks.shape
    eye = jnp.eye(g, dtype=blocks.dtype)
    return jnp.einsum('gab,gh->gahb', blocks, eye).reshape(g * a, g * b)


def _overlap_matrix(seq):
    n_cmp = seq // CMP_STRIDE
    n_sel = seq // SLC_BLOCK
    ci = np.arange(n_cmp)[None, :]
    sj = np.arange(n_sel)[:, None]
    ovl = (ci * CMP_STRIDE < (sj + 1) * SLC_BLOCK) & (ci * CMP_STRIDE + CMP_LEN > sj * SLC_BLOCK) & (ci < n_cmp - 1)
    return jnp.asarray(ovl.astype(np.float32), dtype=BF16)


def _gate_spread_matrix():
    e = np.zeros((N_BRANCH, N_KV_HEADS, GATE_PAD, GQA_GROUP * LANES), np.float32)
    for br in range(N_BRANCH):
        for g in range(N_KV_HEADS):
            for r in range(GQA_GROUP):
                e[br, g, br * N_Q_HEADS + g * GQA_GROUP + r, r * LANES:(r + 1) * LANES] = 1.0
    return jnp.asarray(e, dtype=BF16)


def _layer(x2d, p, batch, seq, tables, final_norm, norm_final):
    cos, sin, cos_c, sin_c, ovl, gexp = tables
    row = lambda v: v.reshape(1, -1).astype(F32)

    q, kc_raw, vc_raw, ksa, vsa, kwa, vwa, gates, u_ssm, u_conv = _in_proj(
        x2d, row(p['norm_mix']), _pack_w_in(p['w_in']), cos, sin, batch, seq)

    n_half = seq // CMP_STRIDE
    kh = kc_raw.reshape(batch, n_half, CMP_STRIDE * KV_WIDTH)
    vh = vc_raw.reshape(batch, n_half, CMP_STRIDE * KV_WIDTH)

    wk = _pack_cmp_w1(p['cmp_k_w1'])
    wv = _pack_cmp_w1(p['cmp_v_w1'])
    pek = _pe_operand(p['cmp_pe_k'])
    pev = _pe_operand(p['cmp_pe_v'])
    kca, vca = _compress(kh, vh, wk, wv, pek, pev, p['cmp_k_w2'].astype(BF16), p['cmp_v_w2'].astype(BF16),
                       cos_c, sin_c)

    y_attn = _attention(q, kca, vca, ksa, vsa, kwa, vwa, gates, ovl, gexp, batch, seq)

    lam_re, lam_im, bb_re, bb_im = _ssm_prep(
        p['ssm_a_re'], p['ssm_a_im'], p['ssm_log_dt'].reshape(-1, 1),
        p['ssm_b_re'].transpose(0, 2, 1), p['ssm_b_im'].transpose(0, 2, 1))
    wb = jnp.concatenate([_block_diag(bb_re), _block_diag(bb_im)], axis=1).astype(BF16)
    wc = jnp.concatenate([_block_diag(p['ssm_c_re'].transpose(0, 2, 1)),
                          -_block_diag(p['ssm_c_im'].transpose(0, 2, 1))], axis=0).astype(BF16)
    lam_re_b = jnp.broadcast_to(lam_re.reshape(1, -1), (batch, SSM_STATES_ALL))
    lam_im_b = jnp.broadcast_to(lam_im.reshape(1, -1), (batch, SSM_STATES_ALL))
    y_ssm = _ssm(u_ssm.reshape(seq * batch, SSM_WIDTH), wb, lam_re_b, lam_im_b, wc, row(p['ssm_d']),
                 p['ssm_w_glu'].astype(BF16), batch, seq)
    y_ssm_tm = y_ssm.reshape(seq, batch * SSM_WIDTH)

    wdw = jnp.pad(p['conv_w_dw'], ((0, CONV_PAD - CONV_K), (0, 0)))
    y_conv = _conv(u_conv, wdw, row(p['conv_b_dw']), row(p['conv_ln_g']), row(p['conv_ln_b']),
                   p['conv_w_pw'].astype(BF16), row(p['conv_b_pw']), batch, seq)

    return _out_ffn(x2d, y_attn, y_ssm_tm, y_conv, row(p['norm_out_attn']), row(p['norm_out_ssm']),
                    row(p['norm_out_conv']), p['w_out'].astype(BF16), row(p['norm_ffn']),
                    p['w_gate'].astype(BF16), p['w_up'].astype(BF16), p['w_down'].astype(BF16),
                    row(norm_final), batch, seq, final_norm)


_LAYER_KEYS = ('norm_mix', 'w_in', 'cmp_pe_k', 'cmp_k_w1', 'cmp_k_w2', 'cmp_pe_v', 'cmp_v_w1', 'cmp_v_w2',
               'ssm_a_re', 'ssm_a_im', 'ssm_log_dt', 'ssm_b_re', 'ssm_b_im', 'ssm_c_re', 'ssm_c_im', 'ssm_d',
               'ssm_w_glu', 'conv_w_dw', 'conv_b_dw', 'conv_ln_g', 'conv_ln_b', 'conv_w_pw', 'conv_b_pw',
               'norm_out_attn', 'norm_out_ssm', 'norm_out_conv', 'w_out', 'norm_ffn', 'w_gate', 'w_up', 'w_down')


def kernel(x, norm_mix, w_in, cmp_pe_k, cmp_k_w1, cmp_k_w2, cmp_pe_v, cmp_v_w1, cmp_v_w2, ssm_a_re, ssm_a_im, ssm_log_dt, ssm_b_re, ssm_b_im, ssm_c_re, ssm_c_im, ssm_d, ssm_w_glu, conv_w_dw, conv_b_dw, conv_ln_g, conv_ln_b, conv_w_pw, conv_b_pw, norm_out_attn, norm_out_ssm, norm_out_conv, w_out, norm_ffn, w_gate, w_up, w_down, norm_final):
    stacked = dict(zip(_LAYER_KEYS, (norm_mix, w_in, cmp_pe_k, cmp_k_w1, cmp_k_w2, cmp_pe_v, cmp_v_w1, cmp_v_w2,
                                     ssm_a_re, ssm_a_im, ssm_log_dt, ssm_b_re, ssm_b_im, ssm_c_re, ssm_c_im, ssm_d,
                                     ssm_w_glu, conv_w_dw, conv_b_dw, conv_ln_g, conv_ln_b, conv_w_pw, conv_b_pw,
                                     norm_out_attn, norm_out_ssm, norm_out_conv, w_out, norm_ffn, w_gate, w_up,
                                     w_down)))
    batch, seq, _ = x.shape
    depth = norm_mix.shape[0]
    cos, sin = _rope_tables(jnp.arange(seq))
    cos_c, sin_c = _rope_tables(jnp.arange(seq // CMP_STRIDE) * CMP_STRIDE + CMP_LEN - 1)
    tables = (cos, sin, cos_c, sin_c, _overlap_matrix(seq), _gate_spread_matrix())
    x2d = x.reshape(batch * seq, D_MODEL)
    for l in range(depth):
        p = {k: v[l] for k, v in stacked.items()}
        x2d = _layer(x2d, p, batch, seq, tables, l == depth - 1, norm_final)
    return x2d.reshape(batch, seq, D_MODEL)
```

```python
import functools
import math

import numpy as np
import jax
import jax.numpy as jnp
from jax import lax
from jax.experimental import pallas as pl
from jax.experimental.pallas import tpu as pltpu

F32 = jnp.float32
BF16 = jnp.bfloat16

D_MODEL = 1024
HEAD_DIM = 64
N_Q_HEADS = 8
N_KV_HEADS = 2
GQA_GROUP = N_Q_HEADS // N_KV_HEADS
ATTN_WIDTH = N_Q_HEADS * HEAD_DIM
KV_WIDTH = N_KV_HEADS * HEAD_DIM
N_BRANCH = 3
SSM_WIDTH = D_MODEL // 4
SSM_GROUP = 16
SSM_N_GROUPS = SSM_WIDTH // SSM_GROUP
SSM_STATE = 64
SSM_STATES_ALL = SSM_N_GROUPS * SSM_STATE
CONV_WIDTH = D_MODEL // 4
CONV_K = 31
CMP_LEN = 32
CMP_STRIDE = 16
CMP_HIDDEN = 256
SLC_BLOCK = 64
SLC_TOPN = 16
WINDOW = 512
ROPE_THETA = 10000.0
D_FF = ((8 * D_MODEL // 3 + 255) // 256) * 256
EPS = 1e-6
NEG_INF = -1e30
FORCE = 1e9

LANES = 128
SUBLANES = 8
VMEM_LIMIT = 56 * 1024 * 1024

GATE_PAD = LANES
PROJ_BLOCK = 512
C_Q = 0
C_KC = C_Q + ATTN_WIDTH
C_VC = C_KC + KV_WIDTH
C_KS = C_VC + KV_WIDTH
C_VS = C_KS + KV_WIDTH
C_KW = C_VS + KV_WIDTH
C_VW = C_KW + KV_WIDTH
C_SSM = C_VW + KV_WIDTH
C_CONV = C_SSM + SSM_WIDTH
C_GATE = C_CONV + 2 * CONV_WIDTH
IN_PACKED = C_GATE + GATE_PAD

TM_PROJ = 512
TQ = 128
CK = 512
RANK_GROUP = 16
SSM_LC = 64
CONV_TC = 128
CONV_PAD = 32


def _gelu_tanh(x):
    return 0.5 * x * (1.0 + jnp.tanh(math.sqrt(2.0 / math.pi) * (x + 0.044715 * (x * x * x))))


def _sigmoid(x):
    return 1.0 / (1.0 + jnp.exp(-x))


def _rms(x, g):
    return x * lax.rsqrt(jnp.mean(x * x, axis=-1, keepdims=True) + EPS) * g


def _dot(a, b):
    return jnp.dot(a, b, preferred_element_type=F32)


def _dot_nt(a, b):
    return lax.dot_general(a, b, (((1,), (1,)), ((), ())), preferred_element_type=F32)


def _rope_rotate(v, cos, sin_signed):
    lane = lax.broadcasted_iota(jnp.int32, v.shape, 1)
    first_half = (lane % HEAD_DIM) < (HEAD_DIM // 2)
    rot = jnp.where(first_half, pltpu.roll(v, LANES - HEAD_DIM // 2, 1), pltpu.roll(v, HEAD_DIM // 2, 1))
    return v * cos + rot * sin_signed


def _in_proj_kernel(x_ref, g_ref, w_ref, cos_ref, sin_ref,
                    q_ref, kc_ref, vc_ref, ksa_ref, vsa_ref, kwa_ref, vwa_ref, gate_ref, ussm_ref, uconv_ref,
                    *, tiles_per_seq):
    tm = x_ref.shape[0]
    h = _rms(x_ref[...], g_ref[...]).astype(BF16)
    cos = cos_ref[...]
    sin = sin_ref[...]

    blocks = {}

    def proj(c0, n):
        b0 = (c0 // PROJ_BLOCK) * PROJ_BLOCK
        if b0 not in blocks:
            blocks[b0] = _dot(h, w_ref[:, b0:min(b0 + PROJ_BLOCK, IN_PACKED)])
        return blocks[b0][:, c0 - b0:c0 - b0 + n]

    lane = lax.broadcasted_iota(jnp.int32, (tm, LANES), 1)
    row = lax.broadcasted_iota(jnp.int32, (tm, LANES), 0)
    t = (pl.program_id(0) % tiles_per_seq) * tm + row
    low = lane < HEAD_DIM
    for j in range(ATTN_WIDTH // LANES):
        qj = _rope_rotate(proj(C_Q + j * LANES, LANES), cos, sin) * (HEAD_DIM ** -0.5)
        q_ref[:, (2 * j) * LANES:(2 * j + 1) * LANES] = jnp.where(low, qj, 0.0).astype(BF16)
        q_ref[:, (2 * j + 1) * LANES:(2 * j + 2) * LANES] = jnp.where(low, pltpu.roll(qj, HEAD_DIM, 1), 0.0).astype(BF16)
    kc_ref[...] = proj(C_KC, KV_WIDTH).astype(BF16)
    vc_ref[...] = proj(C_VC, KV_WIDTH).astype(BF16)
    blk_onehot = jnp.where((t // SLC_BLOCK) == (lane - HEAD_DIM), 1.0, 0.0)
    ones_up = jnp.where(low, 0.0, 1.0)

    def split_groups(v, upper, ref):
        ref[0, 0] = jnp.where(low, v, upper).astype(BF16)
        ref[0, 1] = jnp.where(low, pltpu.roll(v, HEAD_DIM, 1), upper).astype(BF16)

    split_groups(_rope_rotate(proj(C_KS, KV_WIDTH), cos, sin), blk_onehot, ksa_ref)
    split_groups(proj(C_VS, KV_WIDTH), ones_up, vsa_ref)
    split_groups(_rope_rotate(proj(C_KW, KV_WIDTH), cos, sin), 0.0, kwa_ref)
    split_groups(proj(C_VW, KV_WIDTH), ones_up, vwa_ref)

    gate_ref[...] = _sigmoid(proj(C_GATE, GATE_PAD))
    ussm_ref[...] = proj(C_SSM, SSM_WIDTH).astype(BF16)
    a = proj(C_CONV, CONV_WIDTH)
    g = proj(C_CONV + CONV_WIDTH, CONV_WIDTH)
    uconv_ref[...] = a * _sigmoid(g)


def _in_proj(x2d, g, w_packed, cos, sin, batch, seq):
    tm = TM_PROJ
    tiles_per_seq = seq // tm
    tokens = batch * seq
    row_blk = lambda n, dt: (pl.BlockSpec((tm, n), lambda i: (i, 0)), jax.ShapeDtypeStruct((tokens, n), dt))
    grp_blk = (pl.BlockSpec((1, N_KV_HEADS, tm, LANES), lambda i: (i // tiles_per_seq, 0, i % tiles_per_seq, 0)),
               jax.ShapeDtypeStruct((batch, N_KV_HEADS, seq, LANES), BF16))
    outs = [row_blk(N_Q_HEADS * LANES, BF16), row_blk(KV_WIDTH, BF16), row_blk(KV_WIDTH, BF16),
            grp_blk, grp_blk, grp_blk, grp_blk, row_blk(GATE_PAD, F32),
            (pl.BlockSpec((tm, SSM_WIDTH), lambda i: (i % tiles_per_seq, i // tiles_per_seq)),
             jax.ShapeDtypeStruct((seq, batch * SSM_WIDTH), BF16)),
            row_blk(CONV_WIDTH, F32)]
    return pl.pallas_call(
        functools.partial(_in_proj_kernel, tiles_per_seq=tiles_per_seq),
        grid=(tokens // tm,),
        in_specs=[pl.BlockSpec((tm, D_MODEL), lambda i: (i, 0)),
                  pl.BlockSpec((1, D_MODEL), lambda i: (0, 0)),
                  pl.BlockSpec((D_MODEL, IN_PACKED), lambda i: (0, 0)),
                  pl.BlockSpec((tm, LANES), lambda i: (i % tiles_per_seq, 0)),
                  pl.BlockSpec((tm, LANES), lambda i: (i % tiles_per_seq, 0))],
        out_specs=[o[0] for o in outs],
        out_shape=[o[1] for o in outs],
        compiler_params=pltpu.CompilerParams(dimension_semantics=("parallel",), vmem_limit_bytes=VMEM_LIMIT),
        name="in_proj",
    )(x2d, g, w_packed, cos, sin)


def _compress_kernel(kh_ref, vh_ref, wk_ref, wv_ref, pek_ref, pev_ref, w2k_ref, w2v_ref, cos_ref, sin_ref,
                     kca_ref, vca_ref):
    n_half = kh_ref.shape[1]
    hw = N_KV_HEADS * CMP_HIDDEN

    def compress(h_ref, w_ref, pe_ref, w2_ref):
        ab = _dot(h_ref[0], w_ref[...])
        pe = _dot(pe_ref[...], w_ref[...])
        bias = pe[0:1, 0:hw] + pe[4:5, hw:2 * hw]
        nxt = pltpu.roll(ab[:, hw:2 * hw], n_half - 1, 0)
        hid = _gelu_tanh(ab[:, 0:hw] + nxt + bias).astype(BF16)
        return jnp.concatenate([_dot(hid[:, j * CMP_HIDDEN:(j + 1) * CMP_HIDDEN], w2_ref[...])
                                for j in range(N_KV_HEADS)], axis=1)

    def split_groups(v, upper, ref):
        low = lax.broadcasted_iota(jnp.int32, v.shape, 1) < HEAD_DIM
        ref[0, 0] = jnp.where(low, v, upper).astype(BF16)
        ref[0, 1] = jnp.where(low, pltpu.roll(v, HEAD_DIM, 1), upper).astype(BF16)

    kc = compress(kh_ref, wk_ref, pek_ref, w2k_ref)
    split_groups(_rope_rotate(kc, cos_ref[...], sin_ref[...]), 0.0, kca_ref)
    split_groups(compress(vh_ref, wv_ref, pev_ref, w2v_ref), 1.0, vca_ref)


def _compress(kh, vh, wk, wv, pek, pev, w2k, w2v, cos_c, sin_c):
    batch, n_half, width = kh.shape
    full = lambda a: pl.BlockSpec(a.shape, lambda b: (0,) * a.ndim)
    per_b = pl.BlockSpec((1, n_half, width), lambda b: (b, 0, 0))
    out_blk = pl.BlockSpec((1, N_KV_HEADS, n_half, LANES), lambda b: (b, 0, 0, 0))
    out_shape = jax.ShapeDtypeStruct((batch, N_KV_HEADS, n_half, LANES), BF16)
    return pl.pallas_call(
        _compress_kernel,
        grid=(batch,),
        in_specs=[per_b, per_b, full(wk), full(wv), full(pek), full(pev), full(w2k), full(w2v),
                  full(cos_c), full(sin_c)],
        out_specs=[out_blk, out_blk],
        out_shape=[out_shape, out_shape],
        compiler_params=pltpu.CompilerParams(dimension_semantics=("parallel",), vmem_limit_bytes=VMEM_LIMIT),
        name="compress",
    )(kh, vh, wk, wv, pek, pev, w2k, w2v, cos_c, sin_c)


def _attention_kernel(q_ref, kca_ref, vca_ref, ksa_ref, vsa_ref, kwa_ref, vwa_ref, gate_ref, ovl_ref, gexp_ref,
                      y_ref,
                      qaug_ref, oc_ref, s0_ref, s1_ref, mx_ref, m_ref, acc_ref, rank_ref):
    tq = q_ref.shape[0]
    rows = GQA_GROUP * tq
    n_cmp = kca_ref.shape[2]
    seq = ksa_ref.shape[2]
    n_sel = seq // SLC_BLOCK
    t0 = pl.program_id(1) * tq
    row_q = lax.broadcasted_iota(jnp.int32, (rows, 1), 0) % tq
    t_row = t0 + row_q
    t_rep = t0 + lax.broadcasted_iota(jnp.int32, (rows, LANES), 0) % tq
    lane_iota = lax.broadcasted_iota(jnp.int32, (rows, LANES), 1)
    last_blk = (t0 + tq - 1) // SLC_BLOCK

    def row_max_rep(x):
        return jnp.broadcast_to(jnp.max(x, axis=-1, keepdims=True), x.shape)

    def sums_on_all_lanes(a):
        return jnp.where(lane_iota < HEAD_DIM, pltpu.roll(a, HEAD_DIM, 1), a)

    def group_queries(g):
        heads = [q_ref[:, (g * GQA_GROUP + r) * LANES:(g * GQA_GROUP + r + 1) * LANES] for r in range(GQA_GROUP)]
        return jnp.concatenate(heads, axis=0)

    vals = []
    for g in range(N_KV_HEADS):
        sc = _dot_nt(group_queries(g), kca_ref[0, g])
        slabs = []
        for j in range(n_cmp // LANES):
            cmp_end = (lane_iota + j * LANES) * CMP_STRIDE + (CMP_LEN - 1)
            slabs.append(jnp.where(cmp_end <= t_rep, sc[:, j * LANES:(j + 1) * LANES], NEG_INF))
        mc = slabs[0]
        for sj in slabs[1:]:
            mc = jnp.maximum(mc, sj)
        mc = row_max_rep(mc)
        pc = jnp.concatenate([jnp.exp(sj - mc) for sj in slabs], axis=1)
        oc = _dot(pc.astype(BF16), vca_ref[0, g])
        inv = jnp.where(t_rep >= CMP_LEN - 1, 1.0 / jnp.maximum(sums_on_all_lanes(oc), 1e-30), 0.0)
        oc_ref[g] = oc * inv
        pc = pc * jnp.concatenate([inv] * (n_cmp // LANES), axis=1)

        pc_sum = pc[0:tq]
        for r in range(1, GQA_GROUP):
            pc_sum = pc_sum + pc[r * tq:(r + 1) * tq]
        p_hi = pc_sum.astype(BF16)
        rem = pc_sum - p_hi.astype(F32)
        p_mid = rem.astype(BF16)
        p_lo = (rem - p_mid.astype(F32)).astype(BF16)
        ovl = ovl_ref[...]
        imp = _dot_nt(ovl, p_hi) + _dot_nt(ovl, p_mid) + _dot_nt(ovl, p_lo)

        blk = lax.broadcasted_iota(jnp.int32, (n_sel, tq), 0)
        cur = (t0 + lax.broadcasted_iota(jnp.int32, (n_sel, tq), 1)) // SLC_BLOCK
        causal = blk <= cur
        val = jnp.where(blk == 0, FORCE, jnp.where(blk >= cur - 1, FORCE, imp))
        vals.append(jnp.where(causal, val, -FORCE))

    val = jnp.concatenate(vals, axis=1)
    rank_ref[...] = jnp.zeros(rank_ref.shape, jnp.int32)
    grp_rows = lax.broadcasted_iota(jnp.int32, (RANK_GROUP, N_KV_HEADS * tq), 0)
    for ig in range(n_sel // RANK_GROUP):
        for jg in range(n_sel // RANK_GROUP):
            @pl.when(max(ig, jg) * RANK_GROUP <= last_blk)
            def _(ig=ig, jg=jg):
                vj = val[jg * RANK_GROUP:(jg + 1) * RANK_GROUP]
                cnt = jnp.zeros(vj.shape, jnp.int32)
                for ii in range(RANK_GROUP):
                    i = ig * RANK_GROUP + ii
                    vi = val[i:i + 1, :]
                    if jg > ig:
                        cnt = cnt + jnp.where(vi >= vj, 1, 0)
                    elif jg < ig:
                        cnt = cnt + jnp.where(vi > vj, 1, 0)
                    else:
                        tie = jnp.where(grp_rows > ii, 1, 0)
                        cnt = cnt + jnp.where(vi > vj, 1, jnp.where(vi == vj, tie, 0))
                sl = pl.ds(jg * RANK_GROUP, RANK_GROUP)
                rank_ref[sl, :] = rank_ref[sl, :] + cnt

    for g in range(N_KV_HEADS):
        rank = rank_ref[:, g * tq:(g + 1) * tq]
        bias_t = jnp.where(causal, jnp.where(rank < SLC_TOPN, 0.0, NEG_INF), NEG_INF)
        pieces = [jnp.zeros((HEAD_DIM, tq), F32), bias_t]
        if n_sel < LANES - HEAD_DIM:
            pieces.append(jnp.zeros((LANES - HEAD_DIM - n_sel, tq), F32))
        bias = jnp.concatenate(pieces, axis=0).T.astype(BF16)
        qaug_ref[g] = group_queries(g) + jnp.concatenate([bias] * GQA_GROUP, axis=0)

    n_chunks = (t0 + tq + CK - 1) // CK
    last = n_chunks - 1
    mx_ref[...] = jnp.full(mx_ref.shape, NEG_INF, F32)
    acc_ref[...] = jnp.zeros(acc_ref.shape, F32)
    diag_rep = t_rep - lane_iota
    s_refs = (s0_ref, s1_ref)

    def chunk_keys(ref, g, c):
        return ref[0, g, pl.ds(pl.multiple_of(c * CK, CK), CK), :]

    def score_chunk(g, c, diagonal):
        s = _dot_nt(qaug_ref[g], chunk_keys(ksa_ref, g, c))
        mx = mx_ref[g]
        slabs = []
        for j in range(CK // LANES):
            sj = s[:, j * LANES:(j + 1) * LANES]
            if diagonal:
                sj = jnp.where(diag_rep >= c * CK + j * LANES, sj, NEG_INF)
            mx = jnp.maximum(mx, sj)
            slabs.append(sj)
        s_refs[g][c] = jnp.concatenate(slabs, axis=1)
        mx_ref[g] = mx

    def value_chunk(g, c):
        s = s_refs[g][c]
        m = m_ref[g]
        p = jnp.concatenate([jnp.exp(s[:, j * LANES:(j + 1) * LANES] - m) for j in range(CK // LANES)], axis=1)
        acc_ref[g] = acc_ref[g] + _dot(p.astype(BF16), chunk_keys(vsa_ref, g, c))

    def sweep(body, final):
        def wrapped(c, carry):
            body(c)
            return carry
        lax.fori_loop(0, last, wrapped, 0)
        final()

    sweep(lambda c: score_chunk(0, c, False), lambda: score_chunk(0, last, True))
    m_ref[0] = row_max_rep(mx_ref[0])

    def both(c):
        value_chunk(0, c)
        score_chunk(1, c, False)

    def both_last():
        value_chunk(0, last)
        score_chunk(1, last, True)

    sweep(both, both_last)
    m_ref[1] = row_max_rep(mx_ref[1])
    sweep(lambda c: value_chunk(1, c), lambda: value_chunk(1, last))

    gates = gate_ref[...]
    g_hi = gates.astype(BF16)
    g_lo = (gates - g_hi.astype(F32)).astype(BF16)
    w0 = pl.multiple_of(jnp.maximum(t0 - WINDOW, 0), tq)
    n_win = (WINDOW + tq) // LANES
    lane = lax.broadcasted_iota(jnp.int32, (tq, LANES), 1)
    for g in range(N_KV_HEADS):
        sw = _dot_nt(qaug_ref[g], kwa_ref[0, g, pl.ds(w0, WINDOW + tq), :])
        slabs = []
        for j in range(n_win):
            rel = t_rep - (w0 + j * LANES)
            sj = sw[:, j * LANES:(j + 1) * LANES]
            sj = jnp.where(lane_iota <= rel, sj, NEG_INF)
            if j == 0:
                sj = jnp.where(lane_iota > rel - WINDOW, sj, NEG_INF)
            slabs.append(sj)
        mw = slabs[0]
        for sj in slabs[1:]:
            mw = jnp.maximum(mw, sj)
        mw = row_max_rep(mw)
        pw = jnp.concatenate([jnp.exp(sj - mw) for sj in slabs], axis=1)
        aw = _dot(pw.astype(BF16), vwa_ref[0, g, pl.ds(w0, WINDOW + tq), :])
        def gate_rep(branch):
            e = gexp_ref[branch, g]
            spread = _dot(g_hi, e) + _dot(g_lo, e)
            return jnp.concatenate([spread[:, r * LANES:(r + 1) * LANES] for r in range(GQA_GROUP)], axis=0)

        asel = acc_ref[g]
        o = (gate_rep(0) * oc_ref[g]
             + gate_rep(1) * asel / jnp.maximum(pltpu.roll(asel, HEAD_DIM, 1), 1e-30)
             + gate_rep(2) * aw / jnp.maximum(pltpu.roll(aw, HEAD_DIM, 1), 1e-30))
        for pair in range(GQA_GROUP // 2):
            lo = o[(2 * pair) * tq:(2 * pair + 1) * tq]
            hi = o[(2 * pair + 1) * tq:(2 * pair + 2) * tq]
            col = g * GQA_GROUP * HEAD_DIM + pair * LANES
            y_ref[:, col:col + LANES] = jnp.where(lane < HEAD_DIM, lo, pltpu.roll(hi, HEAD_DIM, 1)).astype(BF16)


def _attention(q, kca, vca, ksa, vsa, kwa, vwa, gates, ovl, gexp, batch, seq):
    tq = TQ
    nq = seq // tq
    n_cmp = kca.shape[2]
    rows = GQA_GROUP * tq
    grp = pl.BlockSpec((1, N_KV_HEADS, seq, LANES), lambda b, i: (b, 0, 0, 0))
    cmp_grp = pl.BlockSpec((1, N_KV_HEADS, n_cmp, LANES), lambda b, i: (b, 0, 0, 0))
    grp_rows = lambda dt: pltpu.VMEM((N_KV_HEADS, rows, LANES), dt)
    return pl.pallas_call(
        _attention_kernel,
        grid=(batch, nq),
        in_specs=[pl.BlockSpec((tq, N_Q_HEADS * LANES), lambda b, i: (b * nq + i, 0)),
                  cmp_grp, cmp_grp, grp, grp, grp, grp,
                  pl.BlockSpec((tq, GATE_PAD), lambda b, i: (b * nq + i, 0)),
                  pl.BlockSpec(ovl.shape, lambda b, i: (0, 0)),
                  pl.BlockSpec(gexp.shape, lambda b, i: (0, 0, 0, 0))],
        out_specs=pl.BlockSpec((tq, ATTN_WIDTH), lambda b, i: (b * nq + i, 0)),
        out_shape=jax.ShapeDtypeStruct((batch * seq, ATTN_WIDTH), BF16),
        scratch_shapes=[grp_rows(BF16),
                        grp_rows(F32),
                        pltpu.VMEM((seq // CK, rows, CK), F32),
                        pltpu.VMEM((seq // CK, rows, CK), F32),
                        grp_rows(F32),
                        grp_rows(F32),
                        grp_rows(F32),
                        pltpu.VMEM((seq // SLC_BLOCK, N_KV_HEADS * tq), jnp.int32)],
        compiler_params=pltpu.CompilerParams(dimension_semantics=("parallel", "parallel"),
                                             vmem_limit_bytes=VMEM_LIMIT),
        name="attention",
    )(q, kca, vca, ksa, vsa, kwa, vwa, gates, ovl, gexp)


def _ssm_prep_kernel(a_re_ref, a_im_ref, log_dt_ref, b_re_ref, b_im_ref,
                     lam_re_ref, lam_im_ref, bb_re_ref, bb_im_ref):
    a_re = a_re_ref[...]
    a_im = a_im_ref[...]
    dt = jnp.exp(log_dt_ref[...])
    mag = jnp.exp(a_re * dt)
    lam_re = mag * jnp.cos(a_im * dt)
    lam_im = mag * jnp.sin(a_im * dt)
    den = a_re * a_re + a_im * a_im
    nr, ni = lam_re - 1.0, lam_im
    f_re = (nr * a_re + ni * a_im) / den
    f_im = (ni * a_re - nr * a_im) / den
    lam_re_ref[...] = lam_re
    lam_im_ref[...] = lam_im
    b_re = b_re_ref[...]
    b_im = b_im_ref[...]
    bb_re_ref[...] = f_re[:, None, :] * b_re - f_im[:, None, :] * b_im
    bb_im_ref[...] = f_re[:, None, :] * b_im + f_im[:, None, :] * b_re


def _ssm_prep(a_re, a_im, log_dt, b_re_t, b_im_t):
    gp = jax.ShapeDtypeStruct(a_re.shape, F32)
    gcp = jax.ShapeDtypeStruct(b_re_t.shape, F32)
    return pl.pallas_call(_ssm_prep_kernel, out_shape=[gp, gp, gcp, gcp], name="ssm_prep")(
        a_re, a_im, log_dt, b_re_t, b_im_t)


def _ssm_kernel(u_ref, wb_ref, lam_re_ref, lam_im_ref, wc_ref, d_ref, wglu_ref, y_ref, st_ref, xr_ref, xi_ref,
                *, batch):
    n_st = SSM_STATES_ALL
    steps = u_ref.shape[0] // batch

    @pl.when(pl.program_id(0) == 0)
    def _():
        xr_ref[...] = jnp.zeros(xr_ref.shape, F32)
        xi_ref[...] = jnp.zeros(xi_ref.shape, F32)

    u = u_ref[...]
    st_ref[...] = _dot(u, wb_ref[...])

    def step(t, carry):
        xr, xi = carry
        r0 = pl.multiple_of(t * batch, batch)
        lr = lam_re_ref[...]
        li = lam_im_ref[...]
        nr = lr * xr - li * xi + st_ref[pl.ds(r0, batch), 0:n_st]
        ni = lr * xi + li * xr + st_ref[pl.ds(r0, batch), n_st:2 * n_st]
        st_ref[pl.ds(r0, batch), 0:n_st] = nr
        st_ref[pl.ds(r0, batch), n_st:2 * n_st] = ni
        return nr, ni

    xr, xi = lax.fori_loop(0, steps, step, (xr_ref[...], xi_ref[...]))
    xr_ref[...] = xr
    xi_ref[...] = xi

    y = _dot(st_ref[...].astype(BF16), wc_ref[...]) + d_ref[...] * u.astype(F32)
    z = _dot(_gelu_tanh(y).astype(BF16), wglu_ref[...])
    y_ref[...] = (z[:, 0:SSM_WIDTH] * _sigmoid(z[:, SSM_WIDTH:2 * SSM_WIDTH])).astype(BF16)


def _ssm(u_tb, wb, lam_re_b, lam_im_b, wc, d_skip, w_glu, batch, seq):
    rows = SSM_LC * batch
    full = lambda a: pl.BlockSpec(a.shape, lambda i: (0,) * a.ndim)
    return pl.pallas_call(
        functools.partial(_ssm_kernel, batch=batch),
        grid=(seq // SSM_LC,),
        in_specs=[pl.BlockSpec((rows, SSM_WIDTH), lambda i: (i, 0)),
                  full(wb), full(lam_re_b), full(lam_im_b), full(wc), full(d_skip), full(w_glu)],
        out_specs=pl.BlockSpec((rows, SSM_WIDTH), lambda i: (i, 0)),
        out_shape=jax.ShapeDtypeStruct((seq * batch, SSM_WIDTH), BF16),
        scratch_shapes=[pltpu.VMEM((rows, 2 * SSM_STATES_ALL), F32),
                        pltpu.VMEM((batch, SSM_STATES_ALL), F32),
                        pltpu.VMEM((batch, SSM_STATES_ALL), F32)],
        compiler_params=pltpu.CompilerParams(dimension_semantics=("arbitrary",), vmem_limit_bytes=VMEM_LIMIT),
        name="ssm",
    )(u_tb, wb, lam_re_b, lam_im_b, wc, d_skip, w_glu)


def _conv_kernel(u_ref, wdw_ref, bdw_ref, lng_ref, lnb_ref, wpw_ref, bpw_ref, y_ref, pad_ref):
    seq = u_ref.shape[0]
    pad_ref[0:CONV_PAD, :] = jnp.zeros((CONV_PAD, CONV_WIDTH), F32)
    pad_ref[CONV_PAD:CONV_PAD + seq, :] = u_ref[...]
    wdw = wdw_ref[...]

    def tile(i, carry):
        t0 = pl.multiple_of(i * CONV_TC, CONV_TC)
        win_rows = CONV_TC + CONV_PAD
        win = pad_ref[pl.ds(t0, win_rows), :]
        acc = jnp.zeros((CONV_TC, CONV_WIDTH), F32)
        for b in range(SUBLANES):
            rolled = win if b == 0 else pltpu.roll(win, win_rows - b, 0)
            for a in range(win_rows // SUBLANES):
                k = a * SUBLANES + b - (CONV_PAD - (CONV_K - 1))
                if 0 <= k < CONV_K:
                    acc = acc + wdw[k:k + 1, :] * rolled[a * SUBLANES:a * SUBLANES + CONV_TC]
        y = acc + bdw_ref[...]
        mu = jnp.mean(y, axis=-1, keepdims=True)
        yc = y - mu
        var = jnp.mean(yc * yc, axis=-1, keepdims=True)
        yn = yc * lax.rsqrt(var + EPS) * lng_ref[...] + lnb_ref[...]
        act = yn * _sigmoid(yn)
        y_ref[pl.ds(t0, CONV_TC), :] = (_dot(act.astype(BF16), wpw_ref[...]) + bpw_ref[...]).astype(BF16)
        return carry

    lax.fori_loop(0, seq // CONV_TC, tile, 0)


def _conv(u, wdw, bdw, lng, lnb, wpw, bpw, batch, seq):
    full = lambda a: pl.BlockSpec(a.shape, lambda b: (0,) * a.ndim)
    return pl.pallas_call(
        _conv_kernel,
        grid=(batch,),
        in_specs=[pl.BlockSpec((seq, CONV_WIDTH), lambda b: (b, 0)),
                  full(wdw), full(bdw), full(lng), full(lnb), full(wpw), full(bpw)],
        out_specs=pl.BlockSpec((seq, CONV_WIDTH), lambda b: (b, 0)),
        out_shape=jax.ShapeDtypeStruct((batch * seq, CONV_WIDTH), BF16),
        scratch_shapes=[pltpu.VMEM((CONV_PAD + seq, CONV_WIDTH), F32)],
        compiler_params=pltpu.CompilerParams(dimension_semantics=("parallel",), vmem_limit_bytes=VMEM_LIMIT),
        name="conv",
    )(u, wdw, bdw, lng, lnb, wpw, bpw)


def _out_ffn_kernel(x_ref, ya_ref, ys_ref, yc_ref, ga_ref, gs_ref, gc_ref, wo_ref, gf_ref,
                    wg_ref, wu_ref, wd_ref, gfin_ref, o_ref, *, final_norm):
    na = _rms(ya_ref[...].astype(F32), ga_ref[...]).astype(BF16)
    ns = _rms(ys_ref[...].astype(F32), gs_ref[...]).astype(BF16)
    nc = _rms(yc_ref[...].astype(F32), gc_ref[...]).astype(BF16)
    o1, o2 = ATTN_WIDTH, ATTN_WIDTH + SSM_WIDTH
    x = x_ref[...] + _dot(na, wo_ref[0:o1, :]) + _dot(ns, wo_ref[o1:o2, :]) + _dot(nc, wo_ref[o2:, :])
    o_ref[...] = x
    h = _rms(x, gf_ref[...]).astype(BF16)
    gate = _dot(h, wg_ref[...])
    up = _dot(h, wu_ref[...])
    x = o_ref[...] + _dot((gate * _sigmoid(gate) * up).astype(BF16), wd_ref[...])
    if final_norm:
        x = _rms(x, gfin_ref[...])
    o_ref[...] = x


def _out_ffn(x2d, y_attn, y_ssm_tm, y_conv, ga, gs, gc, wo, gf, wg, wu, wd, gfin, batch, seq, final_norm):
    tm = TM_PROJ
    tiles_per_seq = seq // tm
    tokens = batch * seq
    row = lambda n: pl.BlockSpec((tm, n), lambda i: (i, 0))
    resident = lambda a: pl.BlockSpec(a.shape, lambda i: (0,) * a.ndim)
    return pl.pallas_call(
        functools.partial(_out_ffn_kernel, final_norm=final_norm),
        grid=(tokens // tm,),
        in_specs=[row(D_MODEL), row(ATTN_WIDTH),
                  pl.BlockSpec((tm, SSM_WIDTH), lambda i: (i % tiles_per_seq, i // tiles_per_seq)),
                  row(CONV_WIDTH),
                  resident(ga), resident(gs), resident(gc), resident(wo), resident(gf),
                  resident(wg), resident(wu), resident(wd), resident(gfin)],
        out_specs=row(D_MODEL),
        out_shape=jax.ShapeDtypeStruct((tokens, D_MODEL), F32),
        compiler_params=pltpu.CompilerParams(dimension_semantics=("parallel",), vmem_limit_bytes=VMEM_LIMIT),
        name="out_ffn",
    )(x2d, y_attn, y_ssm_tm, y_conv, ga, gs, gc, wo, gf, wg, wu, wd, gfin)


def _rope_tables(pos):
    half = HEAD_DIM // 2
    freqs = ROPE_THETA ** (-jnp.arange(half, dtype=F32) / half)
    ang = pos.astype(F32)[:, None] * freqs[None, :]
    cos = jnp.tile(jnp.cos(ang), (1, LANES // half))
    sin = jnp.sin(ang)
    sin_signed = jnp.tile(jnp.concatenate([-sin, sin], axis=1), (1, LANES // HEAD_DIM))
    return cos, sin_signed


def _pack_w_in(w_in):
    sizes = (ATTN_WIDTH,) + (KV_WIDTH,) * 6 + (N_BRANCH * N_Q_HEADS, SSM_WIDTH, 2 * CONV_WIDTH)
    parts = jnp.split(w_in, np.cumsum(sizes)[:-1].tolist(), axis=1)
    gate = jnp.pad(parts[7], ((0, 0), (0, GATE_PAD - N_BRANCH * N_Q_HEADS)))
    return jnp.concatenate(parts[:7] + parts[8:] + [gate], axis=1).astype(BF16)


def _pack_cmp_w1(w1):
    w = w1.reshape(2, CMP_STRIDE, HEAD_DIM, CMP_HIDDEN)
    eye = jnp.eye(N_KV_HEADS, dtype=w1.dtype)
    big = jnp.einsum('rpdh,ab->padrbh', w, eye)
    return big.reshape(CMP_STRIDE * N_KV_HEADS * HEAD_DIM, 2 * N_KV_HEADS * CMP_HIDDEN).astype(BF16)


def _pe_operand(pe):
    lo = jnp.tile(pe[:CMP_STRIDE, None, :], (1, N_KV_HEADS, 1)).reshape(1, -1)
    hi = jnp.tile(pe[CMP_STRIDE:, None, :], (1, N_KV_HEADS, 1)).reshape(1, -1)
    return jnp.concatenate([jnp.tile(lo, (4, 1)), jnp.tile(hi, (4, 1))], axis=0).astype(BF16)


def _block_diag(blocks):
    g, a, b = blocks.shape
    eye = jnp.eye(g, dtype=blocks.dtype)
    return jnp.einsum('gab,gh->gahb', blocks, eye).reshape(g * a, g * b)


def _overlap_matrix(seq):
    n_cmp = seq // CMP_STRIDE
    n_sel = seq // SLC_BLOCK
    ci = np.arange(n_cmp)[None, :]
    sj = np.arange(n_sel)[:, None]
    ovl = (ci * CMP_STRIDE < (sj + 1) * SLC_BLOCK) & (ci * CMP_STRIDE + CMP_LEN > sj * SLC_BLOCK) & (ci < n_cmp - 1)
    return jnp.asarray(ovl.astype(np.float32), dtype=BF16)


def _gate_spread_matrix():
    e = np.zeros((N_BRANCH, N_KV_HEADS, GATE_PAD, GQA_GROUP * LANES), np.float32)
    for br in range(N_BRANCH):
        for g in range(N_KV_HEADS):
            for r in range(GQA_GROUP):
                e[br, g, br * N_Q_HEADS + g * GQA_GROUP + r, r * LANES:(r + 1) * LANES] = 1.0
    return jnp.asarray(e, dtype=BF16)


def _layer(x2d, p, batch, seq, tables, final_norm, norm_final):
    cos, sin, cos_c, sin_c, ovl, gexp = tables
    row = lambda v: v.reshape(1, -1).astype(F32)

    q, kc_raw, vc_raw, ksa, vsa, kwa, vwa, gates, u_ssm, u_conv = _in_proj(
        x2d, row(p['norm_mix']), _pack_w_in(p['w_in']), cos, sin, batch, seq)

    n_half = seq // CMP_STRIDE
    kh = kc_raw.reshape(batch, n_half, CMP_STRIDE * KV_WIDTH)
    vh = vc_raw.reshape(batch, n_half, CMP_STRIDE * KV_WIDTH)

    wk = _pack_cmp_w1(p['cmp_k_w1'])
    wv = _pack_cmp_w1(p['cmp_v_w1'])
    pek = _pe_operand(p['cmp_pe_k'])
    pev = _pe_operand(p['cmp_pe_v'])
    kca, vca = _compress(kh, vh, wk, wv, pek, pev, p['cmp_k_w2'].astype(BF16), p['cmp_v_w2'].astype(BF16),
                       cos_c, sin_c)

    y_attn = _attention(q, kca, vca, ksa, vsa, kwa, vwa, gates, ovl, gexp, batch, seq)

    lam_re, lam_im, bb_re, bb_im = _ssm_prep(
        p['ssm_a_re'], p['ssm_a_im'], p['ssm_log_dt'].reshape(-1, 1),
        p['ssm_b_re'].transpose(0, 2, 1), p['ssm_b_im'].transpose(0, 2, 1))
    wb = jnp.concatenate([_block_diag(bb_re), _block_diag(bb_im)], axis=1).astype(BF16)
    wc = jnp.concatenate([_block_diag(p['ssm_c_re'].transpose(0, 2, 1)),
                          -_block_diag(p['ssm_c_im'].transpose(0, 2, 1))], axis=0).astype(BF16)
    lam_re_b = jnp.broadcast_to(lam_re.reshape(1, -1), (batch, SSM_STATES_ALL))
    lam_im_b = jnp.broadcast_to(lam_im.reshape(1, -1), (batch, SSM_STATES_ALL))
    y_ssm = _ssm(u_ssm.reshape(seq * batch, SSM_WIDTH), wb, lam_re_b, lam_im_b, wc, row(p['ssm_d']),
                 p['ssm_w_glu'].astype(BF16), batch, seq)
    y_ssm_tm = y_ssm.reshape(seq, batch * SSM_WIDTH)

    wdw = jnp.pad(p['conv_w_dw'], ((0, CONV_PAD - CONV_K), (0, 0)))
    y_conv = _conv(u_conv, wdw, row(p['conv_b_dw']), row(p['conv_ln_g']), row(p['conv_ln_b']),
                   p['conv_w_pw'].astype(BF16), row(p['conv_b_pw']), batch, seq)

    return _out_ffn(x2d, y_attn, y_ssm_tm, y_conv, row(p['norm_out_attn']), row(p['norm_out_ssm']),
                    row(p['norm_out_conv']), p['w_out'].astype(BF16), row(p['norm_ffn']),
                    p['w_gate'].astype(BF16), p['w_up'].astype(BF16), p['w_down'].astype(BF16),
                    row(norm_final), batch, seq, final_norm)


_LAYER_KEYS = ('norm_mix', 'w_in', 'cmp_pe_k', 'cmp_k_w1', 'cmp_k_w2', 'cmp_pe_v', 'cmp_v_w1', 'cmp_v_w2',
               'ssm_a_re', 'ssm_a_im', 'ssm_log_dt', 'ssm_b_re', 'ssm_b_im', 'ssm_c_re', 'ssm_c_im', 'ssm_d',
               'ssm_w_glu', 'conv_w_dw', 'conv_b_dw', 'conv_ln_g', 'conv_ln_b', 'conv_w_pw', 'conv_b_pw',
               'norm_out_attn', 'norm_out_ssm', 'norm_out_conv', 'w_out', 'norm_ffn', 'w_gate', 'w_up', 'w_down')


def kernel(x, norm_mix, w_in, cmp_pe_k, cmp_k_w1, cmp_k_w2, cmp_pe_v, cmp_v_w1, cmp_v_w2, ssm_a_re, ssm_a_im, ssm_log_dt, ssm_b_re, ssm_b_im, ssm_c_re, ssm_c_im, ssm_d, ssm_w_glu, conv_w_dw, conv_b_dw, conv_ln_g, conv_ln_b, conv_w_pw, conv_b_pw, norm_out_attn, norm_out_ssm, norm_out_conv, w_out, norm_ffn, w_gate, w_up, w_down, norm_final):
    stacked = dict(zip(_LAYER_KEYS, (norm_mix, w_in, cmp_pe_k, cmp_k_w1, cmp_k_w2, cmp_pe_v, cmp_v_w1, cmp_v_w2,
                                     ssm_a_re, ssm_a_im, ssm_log_dt, ssm_b_re, ssm_b_im, ssm_c_re, ssm_c_im, ssm_d,
                                     ssm_w_glu, conv_w_dw, conv_b_dw, conv_ln_g, conv_ln_b, conv_w_pw, conv_b_pw,
                                     norm_out_attn, norm_out_ssm, norm_out_conv, w_out, norm_ffn, w_gate, w_up,
                                     w_down)))
    batch, seq, _ = x.shape
    depth = norm_mix.shape[0]
    cos, sin = _rope_tables(jnp.arange(seq))
    cos_c, sin_c = _rope_tables(jnp.arange(seq // CMP_STRIDE) * CMP_STRIDE + CMP_LEN - 1)
    tables = (cos, sin, cos_c, sin_c, _overlap_matrix(seq), _gate_spread_matrix())
    x2d = x.reshape(batch * seq, D_MODEL)
    for l in range(depth):
        p = {k: v[l] for k, v in stacked.items()}
        x2d = _layer(x2d, p, batch, seq, tables, l == depth - 1, norm_final)
    return x2d.reshape(batch, seq, D_MODEL)
```

```python
import functools
import math

import numpy as np
import jax
import jax.numpy as jnp
from jax import lax
from jax.experimental import pallas as pl
from jax.experimental.pallas import tpu as pltpu

F32 = jnp.float32
BF16 = jnp.bfloat16

D_MODEL = 1024
HEAD_DIM = 64
N_Q_HEADS = 8
N_KV_HEADS = 2
GQA_GROUP = N_Q_HEADS // N_KV_HEADS
ATTN_WIDTH = N_Q_HEADS * HEAD_DIM
KV_WIDTH = N_KV_HEADS * HEAD_DIM
N_BRANCH = 3
SSM_WIDTH = D_MODEL // 4
SSM_GROUP = 16
SSM_N_GROUPS = SSM_WIDTH // SSM_GROUP
SSM_STATE = 64
SSM_STATES_ALL = SSM_N_GROUPS * SSM_STATE
CONV_WIDTH = D_MODEL // 4
CONV_K = 31
CMP_LEN = 32
CMP_STRIDE = 16
CMP_HIDDEN = 256
SLC_BLOCK = 64
SLC_TOPN = 16
WINDOW = 512
ROPE_THETA = 10000.0
D_FF = ((8 * D_MODEL // 3 + 255) // 256) * 256
EPS = 1e-6
NEG_INF = -1e30
FORCE = 1e9

LANES = 128
SUBLANES = 8
VMEM_LIMIT = 56 * 1024 * 1024

GATE_PAD = LANES
PROJ_BLOCK = 512
C_Q = 0
C_KC = C_Q + ATTN_WIDTH
C_VC = C_KC + KV_WIDTH
C_KS = C_VC + KV_WIDTH
C_VS = C_KS + KV_WIDTH
C_KW = C_VS + KV_WIDTH
C_VW = C_KW + KV_WIDTH
C_SSM = C_VW + KV_WIDTH
C_CONV = C_SSM + SSM_WIDTH
C_GATE = C_CONV + 2 * CONV_WIDTH
IN_PACKED = C_GATE + GATE_PAD

TM_PROJ = 512
TQ = 256
CK = 512
RANK_GROUP = 16
SSM_LC = 64
CONV_TC = 128
CONV_PAD = 32


def _gelu_tanh(x):
    return 0.5 * x * (1.0 + jnp.tanh(math.sqrt(2.0 / math.pi) * (x + 0.044715 * (x * x * x))))


def _sigmoid(x):
    return 1.0 / (1.0 + jnp.exp(-x))


def _rms(x, g):
    return x * lax.rsqrt(jnp.mean(x * x, axis=-1, keepdims=True) + EPS) * g


def _dot(a, b):
    return jnp.dot(a, b, preferred_element_type=F32)


def _dot_nt(a, b):
    return lax.dot_general(a, b, (((1,), (1,)), ((), ())), preferred_element_type=F32)


def _rope_rotate(v, cos, sin_signed):
    lane = lax.broadcasted_iota(jnp.int32, v.shape, 1)
    first_half = (lane % HEAD_DIM) < (HEAD_DIM // 2)
    rot = jnp.where(first_half, pltpu.roll(v, LANES - HEAD_DIM // 2, 1), pltpu.roll(v, HEAD_DIM // 2, 1))
    return v * cos + rot * sin_signed


def _in_proj_kernel(x_ref, g_ref, w_ref, cos_ref, sin_ref,
                    q_ref, kc_ref, vc_ref, ksa_ref, vsa_ref, kwa_ref, vwa_ref, gate_ref, ussm_ref, uconv_ref,
                    *, tiles_per_seq):
    tm = x_ref.shape[0]
    h = _rms(x_ref[...], g_ref[...]).astype(BF16)
    cos = cos_ref[...]
    sin = sin_ref[...]

    blocks = {}

    def proj(c0, n):
        b0 = (c0 // PROJ_BLOCK) * PROJ_BLOCK
        if b0 not in blocks:
            blocks[b0] = _dot(h, w_ref[:, b0:min(b0 + PROJ_BLOCK, IN_PACKED)])
        return blocks[b0][:, c0 - b0:c0 - b0 + n]

    lane = lax.broadcasted_iota(jnp.int32, (tm, LANES), 1)
    row = lax.broadcasted_iota(jnp.int32, (tm, LANES), 0)
    t = (pl.program_id(0) % tiles_per_seq) * tm + row
    low = lane < HEAD_DIM
    for j in range(ATTN_WIDTH // LANES):
        qj = _rope_rotate(proj(C_Q + j * LANES, LANES), cos, sin) * (HEAD_DIM ** -0.5)
        q_ref[:, (2 * j) * LANES:(2 * j + 1) * LANES] = jnp.where(low, qj, 0.0).astype(BF16)
        q_ref[:, (2 * j + 1) * LANES:(2 * j + 2) * LANES] = jnp.where(low, pltpu.roll(qj, HEAD_DIM, 1), 0.0).astype(BF16)
    kc_ref[...] = proj(C_KC, KV_WIDTH).astype(BF16)
    vc_ref[...] = proj(C_VC, KV_WIDTH).astype(BF16)
    blk_onehot = jnp.where((t // SLC_BLOCK) == (lane - HEAD_DIM), 1.0, 0.0)
    ones_up = jnp.where(low, 0.0, 1.0)

    def split_groups(v, upper, ref):
        ref[0, 0] = jnp.where(low, v, upper).astype(BF16)
        ref[0, 1] = jnp.where(low, pltpu.roll(v, HEAD_DIM, 1), upper).astype(BF16)

    split_groups(_rope_rotate(proj(C_KS, KV_WIDTH), cos, sin), blk_onehot, ksa_ref)
    split_groups(proj(C_VS, KV_WIDTH), ones_up, vsa_ref)
    split_groups(_rope_rotate(proj(C_KW, KV_WIDTH), cos, sin), 0.0, kwa_ref)
    split_groups(proj(C_VW, KV_WIDTH), ones_up, vwa_ref)

    gate_ref[...] = _sigmoid(proj(C_GATE, GATE_PAD))
    ussm_ref[...] = proj(C_SSM, SSM_WIDTH).astype(BF16)
    a = proj(C_CONV, CONV_WIDTH)
    g = proj(C_CONV + CONV_WIDTH, CONV_WIDTH)
    uconv_ref[...] = a * _sigmoid(g)


def _in_proj(x2d, g, w_packed, cos, sin, batch, seq):
    tm = TM_PROJ
    tiles_per_seq = seq // tm
    tokens = batch * seq
    row_blk = lambda n, dt: (pl.BlockSpec((tm, n), lambda i: (i, 0)), jax.ShapeDtypeStruct((tokens, n), dt))
    grp_blk = (pl.BlockSpec((1, N_KV_HEADS, tm, LANES), lambda i: (i // tiles_per_seq, 0, i % tiles_per_seq, 0)),
               jax.ShapeDtypeStruct((batch, N_KV_HEADS, seq, LANES), BF16))
    outs = [row_blk(N_Q_HEADS * LANES, BF16), row_blk(KV_WIDTH, BF16), row_blk(KV_WIDTH, BF16),
            grp_blk, grp_blk, grp_blk, grp_blk, row_blk(GATE_PAD, F32),
            (pl.BlockSpec((tm, SSM_WIDTH), lambda i: (i % tiles_per_seq, i // tiles_per_seq)),
             jax.ShapeDtypeStruct((seq, batch * SSM_WIDTH), BF16)),
            row_blk(CONV_WIDTH, F32)]
    return pl.pallas_call(
        functools.partial(_in_proj_kernel, tiles_per_seq=tiles_per_seq),
        grid=(tokens // tm,),
        in_specs=[pl.BlockSpec((tm, D_MODEL), lambda i: (i, 0)),
                  pl.BlockSpec((1, D_MODEL), lambda i: (0, 0)),
                  pl.BlockSpec((D_MODEL, IN_PACKED), lambda i: (0, 0)),
                  pl.BlockSpec((tm, LANES), lambda i: (i % tiles_per_seq, 0)),
                  pl.BlockSpec((tm, LANES), lambda i: (i % tiles_per_seq, 0))],
        out_specs=[o[0] for o in outs],
        out_shape=[o[1] for o in outs],
        compiler_params=pltpu.CompilerParams(dimension_semantics=("parallel",), vmem_limit_bytes=VMEM_LIMIT),
        name="in_proj",
    )(x2d, g, w_packed, cos, sin)


def _compress_kernel(kh_ref, vh_ref, wk_ref, wv_ref, pek_ref, pev_ref, w2k_ref, w2v_ref, cos_ref, sin_ref,
                     kca_ref, vca_ref):
    n_half = kh_ref.shape[1]
    hw = N_KV_HEADS * CMP_HIDDEN

    def compress(h_ref, w_ref, pe_ref, w2_ref):
        ab = _dot(h_ref[0], w_ref[...])
        pe = _dot(pe_ref[...], w_ref[...])
        bias = pe[0:1, 0:hw] + pe[4:5, hw:2 * hw]
        nxt = pltpu.roll(ab[:, hw:2 * hw], n_half - 1, 0)
        hid = _gelu_tanh(ab[:, 0:hw] + nxt + bias).astype(BF16)
        return jnp.concatenate([_dot(hid[:, j * CMP_HIDDEN:(j + 1) * CMP_HIDDEN], w2_ref[...])
                                for j in range(N_KV_HEADS)], axis=1)

    def split_groups(v, upper, ref):
        low = lax.broadcasted_iota(jnp.int32, v.shape, 1) < HEAD_DIM
        ref[0, 0] = jnp.where(low, v, upper).astype(BF16)
        ref[0, 1] = jnp.where(low, pltpu.roll(v, HEAD_DIM, 1), upper).astype(BF16)

    kc = compress(kh_ref, wk_ref, pek_ref, w2k_ref)
    split_groups(_rope_rotate(kc, cos_ref[...], sin_ref[...]), 0.0, kca_ref)
    split_groups(compress(vh_ref, wv_ref, pev_ref, w2v_ref), 1.0, vca_ref)


def _compress(kh, vh, wk, wv, pek, pev, w2k, w2v, cos_c, sin_c):
    batch, n_half, width = kh.shape
    full = lambda a: pl.BlockSpec(a.shape, lambda b: (0,) * a.ndim)
    per_b = pl.BlockSpec((1, n_half, width), lambda b: (b, 0, 0))
    out_blk = pl.BlockSpec((1, N_KV_HEADS, n_half, LANES), lambda b: (b, 0, 0, 0))
    out_shape = jax.ShapeDtypeStruct((batch, N_KV_HEADS, n_half, LANES), BF16)
    return pl.pallas_call(
        _compress_kernel,
        grid=(batch,),
        in_specs=[per_b, per_b, full(wk), full(wv), full(pek), full(pev), full(w2k), full(w2v),
                  full(cos_c), full(sin_c)],
        out_specs=[out_blk, out_blk],
        out_shape=[out_shape, out_shape],
        compiler_params=pltpu.CompilerParams(dimension_semantics=("parallel",), vmem_limit_bytes=VMEM_LIMIT),
        name="compress",
    )(kh, vh, wk, wv, pek, pev, w2k, w2v, cos_c, sin_c)


def _attention_kernel(q_ref, kca_ref, vca_ref, ksa_ref, vsa_ref, kwa_ref, vwa_ref, gate_ref, ovl_ref, gexp_ref,
                      y_ref,
                      qaug_ref, oc_ref, s0_ref, s1_ref, mx_ref, m_ref, acc_ref, rank_ref):
    tq = q_ref.shape[0]
    rows = GQA_GROUP * tq
    n_cmp = kca_ref.shape[2]
    seq = ksa_ref.shape[2]
    n_sel = seq // SLC_BLOCK
    t0 = pl.program_id(1) * tq
    row_q = lax.broadcasted_iota(jnp.int32, (rows, 1), 0) % tq
    t_row = t0 + row_q
    t_rep = t0 + lax.broadcasted_iota(jnp.int32, (rows, LANES), 0) % tq
    lane_iota = lax.broadcasted_iota(jnp.int32, (rows, LANES), 1)
    last_blk = (t0 + tq - 1) // SLC_BLOCK

    def row_max_rep(x):
        return jnp.broadcast_to(jnp.max(x, axis=-1, keepdims=True), x.shape)

    def sums_on_all_lanes(a):
        return jnp.where(lane_iota < HEAD_DIM, pltpu.roll(a, HEAD_DIM, 1), a)

    def group_queries(g):
        heads = [q_ref[:, (g * GQA_GROUP + r) * LANES:(g * GQA_GROUP + r + 1) * LANES] for r in range(GQA_GROUP)]
        return jnp.concatenate(heads, axis=0)

    vals = []
    for g in range(N_KV_HEADS):
        sc = _dot_nt(group_queries(g), kca_ref[0, g])
        slabs = []
        for j in range(n_cmp // LANES):
            cmp_end = (lane_iota + j * LANES) * CMP_STRIDE + (CMP_LEN - 1)
            slabs.append(jnp.where(cmp_end <= t_rep, sc[:, j * LANES:(j + 1) * LANES], NEG_INF))
        mc = slabs[0]
        for sj in slabs[1:]:
            mc = jnp.maximum(mc, sj)
        mc = row_max_rep(mc)
        pc = jnp.concatenate([jnp.exp(sj - mc) for sj in slabs], axis=1)
        oc = _dot(pc.astype(BF16), vca_ref[0, g])
        inv = jnp.where(t_rep >= CMP_LEN - 1, 1.0 / jnp.maximum(sums_on_all_lanes(oc), 1e-30), 0.0)
        oc_ref[g] = oc * inv
        pc = pc * jnp.concatenate([inv] * (n_cmp // LANES), axis=1)

        pc_sum = pc[0:tq]
        for r in range(1, GQA_GROUP):
            pc_sum = pc_sum + pc[r * tq:(r + 1) * tq]
        p_hi = pc_sum.astype(BF16)
        rem = pc_sum - p_hi.astype(F32)
        p_mid = rem.astype(BF16)
        p_lo = (rem - p_mid.astype(F32)).astype(BF16)
        ovl = ovl_ref[...]
        imp = _dot_nt(ovl, p_hi) + _dot_nt(ovl, p_mid) + _dot_nt(ovl, p_lo)

        blk = lax.broadcasted_iota(jnp.int32, (n_sel, tq), 0)
        cur = (t0 + lax.broadcasted_iota(jnp.int32, (n_sel, tq), 1)) // SLC_BLOCK
        causal = blk <= cur
        val = jnp.where(blk == 0, FORCE, jnp.where(blk >= cur - 1, FORCE, imp))
        vals.append(jnp.where(causal, val, -FORCE))

    val = jnp.concatenate(vals, axis=1)
    rank_ref[...] = jnp.zeros(rank_ref.shape, jnp.int32)
    grp_rows = lax.broadcasted_iota(jnp.int32, (RANK_GROUP, N_KV_HEADS * tq), 0)
    for ig in range(n_sel // RANK_GROUP):
        for jg in range(n_sel // RANK_GROUP):
            @pl.when(max(ig, jg) * RANK_GROUP <= last_blk)
            def _(ig=ig, jg=jg):
                vj = val[jg * RANK_GROUP:(jg + 1) * RANK_GROUP]
                cnt = jnp.zeros(vj.shape, jnp.int32)
                for ii in range(RANK_GROUP):
                    i = ig * RANK_GROUP + ii
                    vi = val[i:i + 1, :]
                    if jg > ig:
                        cnt = cnt + jnp.where(vi >= vj, 1, 0)
                    elif jg < ig:
                        cnt = cnt + jnp.where(vi > vj, 1, 0)
                    else:
                        tie = jnp.where(grp_rows > ii, 1, 0)
                        cnt = cnt + jnp.where(vi > vj, 1, jnp.where(vi == vj, tie, 0))
                sl = pl.ds(jg * RANK_GROUP, RANK_GROUP)
                rank_ref[sl, :] = rank_ref[sl, :] + cnt

    for g in range(N_KV_HEADS):
        rank = rank_ref[:, g * tq:(g + 1) * tq]
        bias_t = jnp.where(causal, jnp.where(rank < SLC_TOPN, 0.0, NEG_INF), NEG_INF)
        pieces = [jnp.zeros((HEAD_DIM, tq), F32), bias_t]
        if n_sel < LANES - HEAD_DIM:
            pieces.append(jnp.zeros((LANES - HEAD_DIM - n_sel, tq), F32))
        bias = jnp.concatenate(pieces, axis=0).T.astype(BF16)
        qaug_ref[g] = group_queries(g) + jnp.concatenate([bias] * GQA_GROUP, axis=0)

    n_chunks = (t0 + tq + CK - 1) // CK
    last = n_chunks - 1
    mx_ref[...] = jnp.full(mx_ref.shape, NEG_INF, F32)
    acc_ref[...] = jnp.zeros(acc_ref.shape, F32)
    diag_rep = t_rep - lane_iota
    s_refs = (s0_ref, s1_ref)

    def chunk_keys(ref, g, c):
        return ref[0, g, pl.ds(pl.multiple_of(c * CK, CK), CK), :]

    def score_chunk(g, c, diagonal):
        s = _dot_nt(qaug_ref[g], chunk_keys(ksa_ref, g, c))
        mx = mx_ref[g]
        slabs = []
        for j in range(CK // LANES):
            sj = s[:, j * LANES:(j + 1) * LANES]
            if diagonal:
                sj = jnp.where(diag_rep >= c * CK + j * LANES, sj, NEG_INF)
            mx = jnp.maximum(mx, sj)
            slabs.append(sj)
        s_refs[g][c] = jnp.concatenate(slabs, axis=1)
        mx_ref[g] = mx

    def value_chunk(g, c):
        s = s_refs[g][c]
        m = m_ref[g]
        p = jnp.concatenate([jnp.exp(s[:, j * LANES:(j + 1) * LANES] - m) for j in range(CK // LANES)], axis=1)
        acc_ref[g] = acc_ref[g] + _dot(p.astype(BF16), chunk_keys(vsa_ref, g, c))

    def sweep(body, final):
        def pair(i, carry):
            body(2 * i)
            body(2 * i + 1)
            return carry

        def single(c, carry):
            body(c)
            return carry
        n_pairs = last // 2
        lax.fori_loop(0, n_pairs, pair, 0)
        lax.fori_loop(2 * n_pairs, last, single, 0)
        final()

    sweep(lambda c: score_chunk(0, c, False), lambda: score_chunk(0, last, True))
    m_ref[0] = row_max_rep(mx_ref[0])

    def both(c):
        value_chunk(0, c)
        score_chunk(1, c, False)

    def both_last():
        value_chunk(0, last)
        score_chunk(1, last, True)

    sweep(both, both_last)
    m_ref[1] = row_max_rep(mx_ref[1])
    sweep(lambda c: value_chunk(1, c), lambda: value_chunk(1, last))

    gates = gate_ref[...]
    g_hi = gates.astype(BF16)
    g_lo = (gates - g_hi.astype(F32)).astype(BF16)
    w0 = pl.multiple_of(jnp.maximum(t0 - WINDOW, 0), tq)
    n_win = (WINDOW + tq) // LANES
    lane = lax.broadcasted_iota(jnp.int32, (tq, LANES), 1)
    for g in range(N_KV_HEADS):
        sw = _dot_nt(qaug_ref[g], kwa_ref[0, g, pl.ds(w0, WINDOW + tq), :])
        slabs = []
        for j in range(n_win):
            rel = t_rep - (w0 + j * LANES)
            sj = sw[:, j * LANES:(j + 1) * LANES]
            sj = jnp.where(lane_iota <= rel, sj, NEG_INF)
            if j < tq // LANES:
                sj = jnp.where(lane_iota > rel - WINDOW, sj, NEG_INF)
            slabs.append(sj)
        mw = slabs[0]
        for sj in slabs[1:]:
            mw = jnp.maximum(mw, sj)
        mw = row_max_rep(mw)
        pw = jnp.concatenate([jnp.exp(sj - mw) for sj in slabs], axis=1)
        aw = _dot(pw.astype(BF16), vwa_ref[0, g, pl.ds(w0, WINDOW + tq), :])
        def gate_rep(branch):
            e = gexp_ref[branch, g]
            spread = _dot(g_hi, e) + _dot(g_lo, e)
            return jnp.concatenate([spread[:, r * LANES:(r + 1) * LANES] for r in range(GQA_GROUP)], axis=0)

        asel = acc_ref[g]
        o = (gate_rep(0) * oc_ref[g]
             + gate_rep(1) * asel / jnp.maximum(pltpu.roll(asel, HEAD_DIM, 1), 1e-30)
             + gate_rep(2) * aw / jnp.maximum(pltpu.roll(aw, HEAD_DIM, 1), 1e-30))
        for pair in range(GQA_GROUP // 2):
            lo = o[(2 * pair) * tq:(2 * pair + 1) * tq]
            hi = o[(2 * pair + 1) * tq:(2 * pair + 2) * tq]
            col = g * GQA_GROUP * HEAD_DIM + pair * LANES
            y_ref[:, col:col + LANES] = jnp.where(lane < HEAD_DIM, lo, pltpu.roll(hi, HEAD_DIM, 1)).astype(BF16)


def _attention(q, kca, vca, ksa, vsa, kwa, vwa, gates, ovl, gexp, batch, seq):
    tq = TQ
    nq = seq // tq
    n_cmp = kca.shape[2]
    rows = GQA_GROUP * tq
    grp = pl.BlockSpec((1, N_KV_HEADS, seq, LANES), lambda b, i: (b, 0, 0, 0), pipeline_mode=pl.Buffered(1))
    cmp_grp = pl.BlockSpec((1, N_KV_HEADS, n_cmp, LANES), lambda b, i: (b, 0, 0, 0))
    grp_rows = lambda dt: pltpu.VMEM((N_KV_HEADS, rows, LANES), dt)
    return pl.pallas_call(
        _attention_kernel,
        grid=(batch, nq),
        in_specs=[pl.BlockSpec((tq, N_Q_HEADS * LANES), lambda b, i: (b * nq + i, 0)),
                  cmp_grp, cmp_grp, grp, grp, grp, grp,
                  pl.BlockSpec((tq, GATE_PAD), lambda b, i: (b * nq + i, 0)),
                  pl.BlockSpec(ovl.shape, lambda b, i: (0, 0)),
                  pl.BlockSpec(gexp.shape, lambda b, i: (0, 0, 0, 0))],
        out_specs=pl.BlockSpec((tq, ATTN_WIDTH), lambda b, i: (b * nq + i, 0)),
        out_shape=jax.ShapeDtypeStruct((batch * seq, ATTN_WIDTH), BF16),
        scratch_shapes=[grp_rows(BF16),
                        grp_rows(F32),
                        pltpu.VMEM((seq // CK, rows, CK), F32),
                        pltpu.VMEM((seq // CK, rows, CK), F32),
                        grp_rows(F32),
                        grp_rows(F32),
                        grp_rows(F32),
                        pltpu.VMEM((seq // SLC_BLOCK, N_KV_HEADS * tq), jnp.int32)],
        compiler_params=pltpu.CompilerParams(dimension_semantics=("parallel", "parallel"),
                                             vmem_limit_bytes=VMEM_LIMIT),
        name="attention",
    )(q, kca, vca, ksa, vsa, kwa, vwa, gates, ovl, gexp)


def _ssm_prep_kernel(a_re_ref, a_im_ref, log_dt_ref, b_re_ref, b_im_ref,
                     lam_re_ref, lam_im_ref, bb_re_ref, bb_im_ref):
    a_re = a_re_ref[...]
    a_im = a_im_ref[...]
    dt = jnp.exp(log_dt_ref[...])
    mag = jnp.exp(a_re * dt)
    lam_re = mag * jnp.cos(a_im * dt)
    lam_im = mag * jnp.sin(a_im * dt)
    den = a_re * a_re + a_im * a_im
    nr, ni = lam_re - 1.0, lam_im
    f_re = (nr * a_re + ni * a_im) / den
    f_im = (ni * a_re - nr * a_im) / den
    lam_re_ref[...] = lam_re
    lam_im_ref[...] = lam_im
    b_re = b_re_ref[...]
    b_im = b_im_ref[...]
    bb_re_ref[...] = f_re[:, None, :] * b_re - f_im[:, None, :] * b_im
    bb_im_ref[...] = f_re[:, None, :] * b_im + f_im[:, None, :] * b_re


def _ssm_prep(a_re, a_im, log_dt, b_re_t, b_im_t):
    gp = jax.ShapeDtypeStruct(a_re.shape, F32)
    gcp = jax.ShapeDtypeStruct(b_re_t.shape, F32)
    return pl.pallas_call(_ssm_prep_kernel, out_shape=[gp, gp, gcp, gcp], name="ssm_prep")(
        a_re, a_im, log_dt, b_re_t, b_im_t)


def _ssm_kernel(u_ref, wb_ref, lam_re_ref, lam_im_ref, wc_ref, d_ref, wglu_ref, y_ref, st_ref, xr_ref, xi_ref,
                *, batch):
    n_st = SSM_STATES_ALL
    steps = u_ref.shape[0] // batch

    @pl.when(pl.program_id(0) == 0)
    def _():
        xr_ref[...] = jnp.zeros(xr_ref.shape, F32)
        xi_ref[...] = jnp.zeros(xi_ref.shape, F32)

    u = u_ref[...]
    st_ref[...] = _dot(u, wb_ref[...])

    def step(t, carry):
        xr, xi = carry
        r0 = pl.multiple_of(t * batch, batch)
        lr = lam_re_ref[...]
        li = lam_im_ref[...]
        nr = lr * xr - li * xi + st_ref[pl.ds(r0, batch), 0:n_st]
        ni = lr * xi + li * xr + st_ref[pl.ds(r0, batch), n_st:2 * n_st]
        st_ref[pl.ds(r0, batch), 0:n_st] = nr
        st_ref[pl.ds(r0, batch), n_st:2 * n_st] = ni
        return nr, ni

    xr, xi = lax.fori_loop(0, steps, step, (xr_ref[...], xi_ref[...]))
    xr_ref[...] = xr
    xi_ref[...] = xi

    y = _dot(st_ref[...].astype(BF16), wc_ref[...]) + d_ref[...] * u.astype(F32)
    z = _dot(_gelu_tanh(y).astype(BF16), wglu_ref[...])
    y_ref[...] = (z[:, 0:SSM_WIDTH] * _sigmoid(z[:, SSM_WIDTH:2 * SSM_WIDTH])).astype(BF16)


def _ssm(u_tb, wb, lam_re_b, lam_im_b, wc, d_skip, w_glu, batch, seq):
    rows = SSM_LC * batch
    full = lambda a: pl.BlockSpec(a.shape, lambda i: (0,) * a.ndim)
    return pl.pallas_call(
        functools.partial(_ssm_kernel, batch=batch),
        grid=(seq // SSM_LC,),
        in_specs=[pl.BlockSpec((rows, SSM_WIDTH), lambda i: (i, 0)),
                  full(wb), full(lam_re_b), full(lam_im_b), full(wc), full(d_skip), full(w_glu)],
        out_specs=pl.BlockSpec((rows, SSM_WIDTH), lambda i: (i, 0)),
        out_shape=jax.ShapeDtypeStruct((seq * batch, SSM_WIDTH), BF16),
        scratch_shapes=[pltpu.VMEM((rows, 2 * SSM_STATES_ALL), F32),
                        pltpu.VMEM((batch, SSM_STATES_ALL), F32),
                        pltpu.VMEM((batch, SSM_STATES_ALL), F32)],
        compiler_params=pltpu.CompilerParams(dimension_semantics=("arbitrary",), vmem_limit_bytes=VMEM_LIMIT),
        name="ssm",
    )(u_tb, wb, lam_re_b, lam_im_b, wc, d_skip, w_glu)


def _conv_kernel(u_ref, wdw_ref, bdw_ref, lng_ref, lnb_ref, wpw_ref, bpw_ref, y_ref, pad_ref):
    seq = u_ref.shape[0]
    pad_ref[0:CONV_PAD, :] = jnp.zeros((CONV_PAD, CONV_WIDTH), F32)
    pad_ref[CONV_PAD:CONV_PAD + seq, :] = u_ref[...]
    wdw = wdw_ref[...]

    def tile(i, carry):
        t0 = pl.multiple_of(i * CONV_TC, CONV_TC)
        win_rows = CONV_TC + CONV_PAD
        win = pad_ref[pl.ds(t0, win_rows), :]
        acc = jnp.zeros((CONV_TC, CONV_WIDTH), F32)
        for b in range(SUBLANES):
            rolled = win if b == 0 else pltpu.roll(win, win_rows - b, 0)
            for a in range(win_rows // SUBLANES):
                k = a * SUBLANES + b - (CONV_PAD - (CONV_K - 1))
                if 0 <= k < CONV_K:
                    acc = acc + wdw[k:k + 1, :] * rolled[a * SUBLANES:a * SUBLANES + CONV_TC]
        y = acc + bdw_ref[...]
        mu = jnp.mean(y, axis=-1, keepdims=True)
        yc = y - mu
        var = jnp.mean(yc * yc, axis=-1, keepdims=True)
        yn = yc * lax.rsqrt(var + EPS) * lng_ref[...] + lnb_ref[...]
        act = yn * _sigmoid(yn)
        y_ref[pl.ds(t0, CONV_TC), :] = (_dot(act.astype(BF16), wpw_ref[...]) + bpw_ref[...]).astype(BF16)
        return carry

    lax.fori_loop(0, seq // CONV_TC, tile, 0)


def _conv(u, wdw, bdw, lng, lnb, wpw, bpw, batch, seq):
    full = lambda a: pl.BlockSpec(a.shape, lambda b: (0,) * a.ndim)
    return pl.pallas_call(
        _conv_kernel,
        grid=(batch,),
        in_specs=[pl.BlockSpec((seq, CONV_WIDTH), lambda b: (b, 0)),
                  full(wdw), full(bdw), full(lng), full(lnb), full(wpw), full(bpw)],
        out_specs=pl.BlockSpec((seq, CONV_WIDTH), lambda b: (b, 0)),
        out_shape=jax.ShapeDtypeStruct((batch * seq, CONV_WIDTH), BF16),
        scratch_shapes=[pltpu.VMEM((CONV_PAD + seq, CONV_WIDTH), F32)],
        compiler_params=pltpu.CompilerParams(dimension_semantics=("parallel",), vmem_limit_bytes=VMEM_LIMIT),
        name="conv",
    )(u, wdw, bdw, lng, lnb, wpw, bpw)


def _out_ffn_kernel(x_ref, ya_ref, ys_ref, yc_ref, ga_ref, gs_ref, gc_ref, wo_ref, gf_ref,
                    wg_ref, wu_ref, wd_ref, gfin_ref, o_ref, *, final_norm):
    na = _rms(ya_ref[...].astype(F32), ga_ref[...]).astype(BF16)
    ns = _rms(ys_ref[...].astype(F32), gs_ref[...]).astype(BF16)
    nc = _rms(yc_ref[...].astype(F32), gc_ref[...]).astype(BF16)
    o1, o2 = ATTN_WIDTH, ATTN_WIDTH + SSM_WIDTH
    x = x_ref[...] + _dot(na, wo_ref[0:o1, :]) + _dot(ns, wo_ref[o1:o2, :]) + _dot(nc, wo_ref[o2:, :])
    o_ref[...] = x
    h = _rms(x, gf_ref[...]).astype(BF16)
    gate = _dot(h, wg_ref[...])
    up = _dot(h, wu_ref[...])
    x = o_ref[...] + _dot((gate * _sigmoid(gate) * up).astype(BF16), wd_ref[...])
    if final_norm:
        x = _rms(x, gfin_ref[...])
    o_ref[...] = x


def _out_ffn(x2d, y_attn, y_ssm_tm, y_conv, ga, gs, gc, wo, gf, wg, wu, wd, gfin, batch, seq, final_norm):
    tm = TM_PROJ
    tiles_per_seq = seq // tm
    tokens = batch * seq
    row = lambda n: pl.BlockSpec((tm, n), lambda i: (i, 0))
    resident = lambda a: pl.BlockSpec(a.shape, lambda i: (0,) * a.ndim)
    return pl.pallas_call(
        functools.partial(_out_ffn_kernel, final_norm=final_norm),
        grid=(tokens // tm,),
        in_specs=[row(D_MODEL), row(ATTN_WIDTH),
                  pl.BlockSpec((tm, SSM_WIDTH), lambda i: (i % tiles_per_seq, i // tiles_per_seq)),
                  row(CONV_WIDTH),
                  resident(ga), resident(gs), resident(gc), resident(wo), resident(gf),
                  resident(wg), resident(wu), resident(wd), resident(gfin)],
        out_specs=row(D_MODEL),
        out_shape=jax.ShapeDtypeStruct((tokens, D_MODEL), F32),
        compiler_params=pltpu.CompilerParams(dimension_semantics=("parallel",), vmem_limit_bytes=VMEM_LIMIT),
        name="out_ffn",
    )(x2d, y_attn, y_ssm_tm, y_conv, ga, gs, gc, wo, gf, wg, wu, wd, gfin)


def _rope_tables(pos):
    half = HEAD_DIM // 2
    freqs = ROPE_THETA ** (-jnp.arange(half, dtype=F32) / half)
    ang = pos.astype(F32)[:, None] * freqs[None, :]
    cos = jnp.tile(jnp.cos(ang), (1, LANES // half))
    sin = jnp.sin(ang)
    sin_signed = jnp.tile(jnp.concatenate([-sin, sin], axis=1), (1, LANES // HEAD_DIM))
    return cos, sin_signed


def _pack_w_in(w_in):
    sizes = (ATTN_WIDTH,) + (KV_WIDTH,) * 6 + (N_BRANCH * N_Q_HEADS, SSM_WIDTH, 2 * CONV_WIDTH)
    parts = jnp.split(w_in, np.cumsum(sizes)[:-1].tolist(), axis=1)
    gate = jnp.pad(parts[7], ((0, 0), (0, GATE_PAD - N_BRANCH * N_Q_HEADS)))
    return jnp.concatenate(parts[:7] + parts[8:] + [gate], axis=1).astype(BF16)


def _pack_cmp_w1(w1):
    w = w1.reshape(2, CMP_STRIDE, HEAD_DIM, CMP_HIDDEN)
    eye = jnp.eye(N_KV_HEADS, dtype=w1.dtype)
    big = jnp.einsum('rpdh,ab->padrbh', w, eye)
    return big.reshape(CMP_STRIDE * N_KV_HEADS * HEAD_DIM, 2 * N_KV_HEADS * CMP_HIDDEN).astype(BF16)


def _pe_operand(pe):
    lo = jnp.tile(pe[:CMP_STRIDE, None, :], (1, N_KV_HEADS, 1)).reshape(1, -1)
    hi = jnp.tile(pe[CMP_STRIDE:, None, :], (1, N_KV_HEADS, 1)).reshape(1, -1)
    return jnp.concatenate([jnp.tile(lo, (4, 1)), jnp.tile(hi, (4, 1))], axis=0).astype(BF16)


def _block_diag(blocks):
    g, a, b = blocks.shape
    eye = jnp.eye(g, dtype=blocks.dtype)
    return jnp.einsum('gab,gh->gahb', blocks, eye).reshape(g * a, g * b)


def _overlap_matrix(seq):
    n_cmp = seq // CMP_STRIDE
    n_sel = seq // SLC_BLOCK
    ci = np.arange(n_cmp)[None, :]
    sj = np.arange(n_sel)[:, None]
    ovl = (ci * CMP_STRIDE < (sj + 1) * SLC_BLOCK) & (ci * CMP_STRIDE + CMP_LEN > sj * SLC_BLOCK) & (ci < n_cmp - 1)
    return jnp.asarray(ovl.astype(np.float32), dtype=BF16)


def _gate_spread_matrix():
    e = np.zeros((N_BRANCH, N_KV_HEADS, GATE_PAD, GQA_GROUP * LANES), np.float32)
    for br in range(N_BRANCH):
        for g in range(N_KV_HEADS):
            for r in range(GQA_GROUP):
                e[br, g, br * N_Q_HEADS + g * GQA_GROUP + r, r * LANES:(r + 1) * LANES] = 1.0
    return jnp.asarray(e, dtype=BF16)


def _layer(x2d, p, batch, seq, tables, final_norm, norm_final):
    cos, sin, cos_c, sin_c, ovl, gexp = tables
    row = lambda v: v.reshape(1, -1).astype(F32)

    q, kc_raw, vc_raw, ksa, vsa, kwa, vwa, gates, u_ssm, u_conv = _in_proj(
        x2d, row(p['norm_mix']), _pack_w_in(p['w_in']), cos, sin, batch, seq)

    n_half = seq // CMP_STRIDE
    kh = kc_raw.reshape(batch, n_half, CMP_STRIDE * KV_WIDTH)
    vh = vc_raw.reshape(batch, n_half, CMP_STRIDE * KV_WIDTH)

    wk = _pack_cmp_w1(p['cmp_k_w1'])
    wv = _pack_cmp_w1(p['cmp_v_w1'])
    pek = _pe_operand(p['cmp_pe_k'])
    pev = _pe_operand(p['cmp_pe_v'])
    kca, vca = _compress(kh, vh, wk, wv, pek, pev, p['cmp_k_w2'].astype(BF16), p['cmp_v_w2'].astype(BF16),
                       cos_c, sin_c)

    y_attn = _attention(q, kca, vca, ksa, vsa, kwa, vwa, gates, ovl, gexp, batch, seq)

    lam_re, lam_im, bb_re, bb_im = _ssm_prep(
        p['ssm_a_re'], p['ssm_a_im'], p['ssm_log_dt'].reshape(-1, 1),
        p['ssm_b_re'].transpose(0, 2, 1), p['ssm_b_im'].transpose(0, 2, 1))
    wb = jnp.concatenate([_block_diag(bb_re), _block_diag(bb_im)], axis=1).astype(BF16)
    wc = jnp.concatenate([_block_diag(p['ssm_c_re'].transpose(0, 2, 1)),
                          -_block_diag(p['ssm_c_im'].transpose(0, 2, 1))], axis=0).astype(BF16)
    lam_re_b = jnp.broadcast_to(lam_re.reshape(1, -1), (batch, SSM_STATES_ALL))
    lam_im_b = jnp.broadcast_to(lam_im.reshape(1, -1), (batch, SSM_STATES_ALL))
    y_ssm = _ssm(u_ssm.reshape(seq * batch, SSM_WIDTH), wb, lam_re_b, lam_im_b, wc, row(p['ssm_d']),
                 p['ssm_w_glu'].astype(BF16), batch, seq)
    y_ssm_tm = y_ssm.reshape(seq, batch * SSM_WIDTH)

    wdw = jnp.pad(p['conv_w_dw'], ((0, CONV_PAD - CONV_K), (0, 0)))
    y_conv = _conv(u_conv, wdw, row(p['conv_b_dw']), row(p['conv_ln_g']), row(p['conv_ln_b']),
                   p['conv_w_pw'].astype(BF16), row(p['conv_b_pw']), batch, seq)

    return _out_ffn(x2d, y_attn, y_ssm_tm, y_conv, row(p['norm_out_attn']), row(p['norm_out_ssm']),
                    row(p['norm_out_conv']), p['w_out'].astype(BF16), row(p['norm_ffn']),
                    p['w_gate'].astype(BF16), p['w_up'].astype(BF16), p['w_down'].astype(BF16),
                    row(norm_final), batch, seq, final_norm)


_LAYER_KEYS = ('norm_mix', 'w_in', 'cmp_pe_k', 'cmp_k_w1', 'cmp_k_w2', 'cmp_pe_v', 'cmp_v_w1', 'cmp_v_w2',
               'ssm_a_re', 'ssm_a_im', 'ssm_log_dt', 'ssm_b_re', 'ssm_b_im', 'ssm_c_re', 'ssm_c_im', 'ssm_d',
               'ssm_w_glu', 'conv_w_dw', 'conv_b_dw', 'conv_ln_g', 'conv_ln_b', 'conv_w_pw', 'conv_b_pw',
               'norm_out_attn', 'norm_out_ssm', 'norm_out_conv', 'w_out', 'norm_ffn', 'w_gate', 'w_up', 'w_down')


def kernel(x, norm_mix, w_in, cmp_pe_k, cmp_k_w1, cmp_k_w2, cmp_pe_v, cmp_v_w1, cmp_v_w2, ssm_a_re, ssm_a_im, ssm_log_dt, ssm_b_re, ssm_b_im, ssm_c_re, ssm_c_im, ssm_d, ssm_w_glu, conv_w_dw, conv_b_dw, conv_ln_g, conv_ln_b, conv_w_pw, conv_b_pw, norm_out_attn, norm_out_ssm, norm_out_conv, w_out, norm_ffn, w_gate, w_up, w_down, norm_final):
    stacked = dict(zip(_LAYER_KEYS, (norm_mix, w_in, cmp_pe_k, cmp_k_w1, cmp_k_w2, cmp_pe_v, cmp_v_w1, cmp_v_w2,
                                     ssm_a_re, ssm_a_im, ssm_log_dt, ssm_b_re, ssm_b_im, ssm_c_re, ssm_c_im, ssm_d,
                                     ssm_w_glu, conv_w_dw, conv_b_dw, conv_ln_g, conv_ln_b, conv_w_pw, conv_b_pw,
                                     norm_out_attn, norm_out_ssm, norm_out_conv, w_out, norm_ffn, w_gate, w_up,
                                     w_down)))
    batch, seq, _ = x.shape
    depth = norm_mix.shape[0]
    cos, sin = _rope_tables(jnp.arange(seq))
    cos_c, sin_c = _rope_tables(jnp.arange(seq // CMP_STRIDE) * CMP_STRIDE + CMP_LEN - 1)
    tables = (cos, sin, cos_c, sin_c, _overlap_matrix(seq), _gate_spread_matrix())
    x2d = x.reshape(batch * seq, D_MODEL)
    for l in range(depth):
        p = {k: v[l] for k, v in stacked.items()}
        x2d = _layer(x2d, p, batch, seq, tables, l == depth - 1, norm_final)
    return x2d.reshape(batch, seq, D_MODEL)
```

```python
import functools
import math

import numpy as np
import jax
import jax.numpy as jnp
from jax import lax
from jax.experimental import pallas as pl
from jax.experimental.pallas import tpu as pltpu

F32 = jnp.float32
BF16 = jnp.bfloat16

D_MODEL = 1024
HEAD_DIM = 64
N_Q_HEADS = 8
N_KV_HEADS = 2
GQA_GROUP = N_Q_HEADS // N_KV_HEADS
ATTN_WIDTH = N_Q_HEADS * HEAD_DIM
KV_WIDTH = N_KV_HEADS * HEAD_DIM
N_BRANCH = 3
SSM_WIDTH = D_MODEL // 4
SSM_GROUP = 16
SSM_N_GROUPS = SSM_WIDTH // SSM_GROUP
SSM_STATE = 64
SSM_STATES_ALL = SSM_N_GROUPS * SSM_STATE
CONV_WIDTH = D_MODEL // 4
CONV_K = 31
CMP_LEN = 32
CMP_STRIDE = 16
CMP_HIDDEN = 256
SLC_BLOCK = 64
SLC_TOPN = 16
WINDOW = 512
ROPE_THETA = 10000.0
D_FF = ((8 * D_MODEL // 3 + 255) // 256) * 256
EPS = 1e-6
NEG_INF = -1e30
Q_SCALE = HEAD_DIM ** -0.5 * math.log2(math.e)
FORCE = 1e9

LANES = 128
SUBLANES = 8
VMEM_LIMIT = 56 * 1024 * 1024

GATE_PAD = LANES
PROJ_BLOCK = 512
C_Q = 0
C_KC = C_Q + ATTN_WIDTH
C_VC = C_KC + KV_WIDTH
C_KS = C_VC + KV_WIDTH
C_VS = C_KS + KV_WIDTH
C_KW = C_VS + KV_WIDTH
C_VW = C_KW + KV_WIDTH
C_SSM = C_VW + KV_WIDTH
C_CONV = C_SSM + SSM_WIDTH
C_GATE = C_CONV + 2 * CONV_WIDTH
IN_PACKED = C_GATE + GATE_PAD

TM_PROJ = 512
TQ = 256
CK = 512
WIN_PIECE = 256
RANK_GROUP = 16
SSM_LC = 128
CONV_TC = 128
CONV_PAD = 32


def _gelu_tanh(x):
    return 0.5 * x * (1.0 + jnp.tanh(math.sqrt(2.0 / math.pi) * (x + 0.044715 * (x * x * x))))


def _sigmoid(x):
    return 1.0 / (1.0 + jnp.exp(-x))


def _rms(x, g):
    return x * lax.rsqrt(jnp.mean(x * x, axis=-1, keepdims=True) + EPS) * g


def _dot(a, b):
    return jnp.dot(a, b, preferred_element_type=F32)


def _dot_nt(a, b):
    return lax.dot_general(a, b, (((1,), (1,)), ((), ())), preferred_element_type=F32)


def _rope_rotate(v, cos, sin_signed):
    lane = lax.broadcasted_iota(jnp.int32, v.shape, 1)
    first_half = (lane % HEAD_DIM) < (HEAD_DIM // 2)
    rot = jnp.where(first_half, pltpu.roll(v, LANES - HEAD_DIM // 2, 1), pltpu.roll(v, HEAD_DIM // 2, 1))
    return v * cos + rot * sin_signed


def _in_proj_kernel(x_ref, g_ref, w_ref, cos_ref, sin_ref,
                    q_ref, kc_ref, vc_ref, ksa_ref, vsa_ref, kwa_ref, vwa_ref, gate_ref, ussm_ref, uconv_ref,
                    *, tiles_per_seq):
    tm = x_ref.shape[0]
    h = _rms(x_ref[...], g_ref[...]).astype(BF16)
    cos = cos_ref[...]
    sin = sin_ref[...]

    blocks = {}

    def proj(c0, n):
        b0 = (c0 // PROJ_BLOCK) * PROJ_BLOCK
        if b0 not in blocks:
            blocks[b0] = _dot(h, w_ref[:, b0:min(b0 + PROJ_BLOCK, IN_PACKED)])
        return blocks[b0][:, c0 - b0:c0 - b0 + n]

    lane = lax.broadcasted_iota(jnp.int32, (tm, LANES), 1)
    row = lax.broadcasted_iota(jnp.int32, (tm, LANES), 0)
    t = (pl.program_id(0) % tiles_per_seq) * tm + row
    low = lane < HEAD_DIM
    for j in range(ATTN_WIDTH // LANES):
        qj = _rope_rotate(proj(C_Q + j * LANES, LANES), cos, sin) * Q_SCALE
        q_ref[:, (2 * j) * LANES:(2 * j + 1) * LANES] = jnp.where(low, qj, 0.0).astype(BF16)
        q_ref[:, (2 * j + 1) * LANES:(2 * j + 2) * LANES] = jnp.where(low, pltpu.roll(qj, HEAD_DIM, 1), 0.0).astype(BF16)
    kc_ref[...] = proj(C_KC, KV_WIDTH).astype(BF16)
    vc_ref[...] = proj(C_VC, KV_WIDTH).astype(BF16)
    blk_onehot = jnp.where((t // SLC_BLOCK) == (lane - HEAD_DIM), 1.0, 0.0)
    ones_up = jnp.where(low, 0.0, 1.0)

    def split_groups(v, upper, ref):
        ref[0, 0] = jnp.where(low, v, upper).astype(BF16)
        ref[0, 1] = jnp.where(low, pltpu.roll(v, HEAD_DIM, 1), upper).astype(BF16)

    split_groups(_rope_rotate(proj(C_KS, KV_WIDTH), cos, sin), blk_onehot, ksa_ref)
    split_groups(proj(C_VS, KV_WIDTH), ones_up, vsa_ref)
    split_groups(_rope_rotate(proj(C_KW, KV_WIDTH), cos, sin), 0.0, kwa_ref)
    split_groups(proj(C_VW, KV_WIDTH), ones_up, vwa_ref)

    gate_ref[...] = _sigmoid(proj(C_GATE, GATE_PAD))
    ussm_ref[...] = proj(C_SSM, SSM_WIDTH).astype(BF16)
    a = proj(C_CONV, CONV_WIDTH)
    g = proj(C_CONV + CONV_WIDTH, CONV_WIDTH)
    uconv_ref[...] = a * _sigmoid(g)


def _in_proj(x2d, g, w_packed, cos, sin, batch, seq):
    tm = TM_PROJ
    tiles_per_seq = seq // tm
    tokens = batch * seq
    row_blk = lambda n, dt: (pl.BlockSpec((tm, n), lambda i: (i, 0)), jax.ShapeDtypeStruct((tokens, n), dt))
    grp_blk = (pl.BlockSpec((1, N_KV_HEADS, tm, LANES), lambda i: (i // tiles_per_seq, 0, i % tiles_per_seq, 0)),
               jax.ShapeDtypeStruct((batch, N_KV_HEADS, seq, LANES), BF16))
    outs = [row_blk(N_Q_HEADS * LANES, BF16), row_blk(KV_WIDTH, BF16), row_blk(KV_WIDTH, BF16),
            grp_blk, grp_blk, grp_blk, grp_blk, row_blk(GATE_PAD, F32),
            (pl.BlockSpec((tm, SSM_WIDTH), lambda i: (i % tiles_per_seq, i // tiles_per_seq)),
             jax.ShapeDtypeStruct((seq, batch * SSM_WIDTH), BF16)),
            row_blk(CONV_WIDTH, F32)]
    return pl.pallas_call(
        functools.partial(_in_proj_kernel, tiles_per_seq=tiles_per_seq),
        grid=(tokens // tm,),
        in_specs=[pl.BlockSpec((tm, D_MODEL), lambda i: (i, 0)),
                  pl.BlockSpec((1, D_MODEL), lambda i: (0, 0)),
                  pl.BlockSpec((D_MODEL, IN_PACKED), lambda i: (0, 0)),
                  pl.BlockSpec((tm, LANES), lambda i: (i % tiles_per_seq, 0)),
                  pl.BlockSpec((tm, LANES), lambda i: (i % tiles_per_seq, 0))],
        out_specs=[o[0] for o in outs],
        out_shape=[o[1] for o in outs],
        compiler_params=pltpu.CompilerParams(dimension_semantics=("parallel",), vmem_limit_bytes=VMEM_LIMIT),
        name="in_proj",
    )(x2d, g, w_packed, cos, sin)


def _compress_kernel(kh_ref, vh_ref, wk_ref, wv_ref, pek_ref, pev_ref, w2k_ref, w2v_ref, cos_ref, sin_ref,
                     kca_ref, vca_ref):
    n_half = kh_ref.shape[1]
    hw = N_KV_HEADS * CMP_HIDDEN

    def compress(h_ref, w_ref, pe_ref, w2_ref):
        ab = _dot(h_ref[0], w_ref[...])
        pe = _dot(pe_ref[...], w_ref[...])
        bias = pe[0:1, 0:hw] + pe[4:5, hw:2 * hw]
        nxt = pltpu.roll(ab[:, hw:2 * hw], n_half - 1, 0)
        hid = _gelu_tanh(ab[:, 0:hw] + nxt + bias).astype(BF16)
        return jnp.concatenate([_dot(hid[:, j * CMP_HIDDEN:(j + 1) * CMP_HIDDEN], w2_ref[...])
                                for j in range(N_KV_HEADS)], axis=1)

    def split_groups(v, upper, ref):
        low = lax.broadcasted_iota(jnp.int32, v.shape, 1) < HEAD_DIM
        ref[0, 0] = jnp.where(low, v, upper).astype(BF16)
        ref[0, 1] = jnp.where(low, pltpu.roll(v, HEAD_DIM, 1), upper).astype(BF16)

    kc = compress(kh_ref, wk_ref, pek_ref, w2k_ref)
    split_groups(_rope_rotate(kc, cos_ref[...], sin_ref[...]), 0.0, kca_ref)
    split_groups(compress(vh_ref, wv_ref, pev_ref, w2v_ref), 1.0, vca_ref)


def _compress(kh, vh, wk, wv, pek, pev, w2k, w2v, cos_c, sin_c):
    batch, n_half, width = kh.shape
    full = lambda a: pl.BlockSpec(a.shape, lambda b: (0,) * a.ndim)
    per_b = pl.BlockSpec((1, n_half, width), lambda b: (b, 0, 0))
    out_blk = pl.BlockSpec((1, N_KV_HEADS, n_half, LANES), lambda b: (b, 0, 0, 0))
    out_shape = jax.ShapeDtypeStruct((batch, N_KV_HEADS, n_half, LANES), BF16)
    return pl.pallas_call(
        _compress_kernel,
        grid=(batch,),
        in_specs=[per_b, per_b, full(wk), full(wv), full(pek), full(pev), full(w2k), full(w2v),
                  full(cos_c), full(sin_c)],
        out_specs=[out_blk, out_blk],
        out_shape=[out_shape, out_shape],
        compiler_params=pltpu.CompilerParams(dimension_semantics=("parallel",), vmem_limit_bytes=VMEM_LIMIT),
        name="compress",
    )(kh, vh, wk, wv, pek, pev, w2k, w2v, cos_c, sin_c)


def _attention_kernel(q_ref, kca_ref, vca_ref, ksa_ref, vsa_ref, kwa_ref, vwa_ref, gate_ref, ovl_ref, gexp_ref,
                      y_ref,
                      qaug_ref, oc_ref, s0_ref, s1_ref, mx_ref, m_ref, acc_ref, rank_ref):
    tq = q_ref.shape[0]
    rows = GQA_GROUP * tq
    n_cmp = kca_ref.shape[2]
    seq = ksa_ref.shape[2]
    n_sel = seq // SLC_BLOCK
    t0 = pl.program_id(1) * tq
    row_q = lax.broadcasted_iota(jnp.int32, (rows, 1), 0) % tq
    t_row = t0 + row_q
    t_rep = t0 + lax.broadcasted_iota(jnp.int32, (rows, LANES), 0) % tq
    lane_iota = lax.broadcasted_iota(jnp.int32, (rows, LANES), 1)
    diag_rep = t_rep - lane_iota
    cmp_rep = t_rep - lane_iota * CMP_STRIDE - (CMP_LEN - 1)
    last_blk = (t0 + tq - 1) // SLC_BLOCK

    def row_max_rep(x):
        return jnp.broadcast_to(jnp.max(x, axis=-1, keepdims=True), x.shape)

    def sums_on_all_lanes(a):
        return jnp.where(lane_iota < HEAD_DIM, pltpu.roll(a, HEAD_DIM, 1), a)

    def group_queries(g):
        heads = [q_ref[:, (g * GQA_GROUP + r) * LANES:(g * GQA_GROUP + r + 1) * LANES] for r in range(GQA_GROUP)]
        return jnp.concatenate(heads, axis=0)

    vals = []
    for g in range(N_KV_HEADS):
        sc = _dot_nt(group_queries(g), kca_ref[0, g])
        slabs = []
        for j in range(n_cmp // LANES):
            slabs.append(jnp.where(cmp_rep >= j * LANES * CMP_STRIDE, sc[:, j * LANES:(j + 1) * LANES], NEG_INF))
        mc = slabs[0]
        for sj in slabs[1:]:
            mc = jnp.maximum(mc, sj)
        mc = row_max_rep(mc)
        pc = jnp.concatenate([jnp.exp2(sj - mc) for sj in slabs], axis=1)
        oc = _dot(pc.astype(BF16), vca_ref[0, g])
        inv = jnp.where(t_rep >= CMP_LEN - 1, 1.0 / jnp.maximum(sums_on_all_lanes(oc), 1e-30), 0.0)
        oc_ref[g] = oc * inv
        pc = pc * jnp.concatenate([inv] * (n_cmp // LANES), axis=1)

        pc_sum = pc[0:tq]
        for r in range(1, GQA_GROUP):
            pc_sum = pc_sum + pc[r * tq:(r + 1) * tq]
        p_hi = pc_sum.astype(BF16)
        rem = pc_sum - p_hi.astype(F32)
        p_mid = rem.astype(BF16)
        p_lo = (rem - p_mid.astype(F32)).astype(BF16)
        ovl = ovl_ref[...]
        imp = _dot_nt(ovl, p_hi) + _dot_nt(ovl, p_mid) + _dot_nt(ovl, p_lo)

        blk = lax.broadcasted_iota(jnp.int32, (n_sel, tq), 0)
        cur = (t0 + lax.broadcasted_iota(jnp.int32, (n_sel, tq), 1)) // SLC_BLOCK
        causal = blk <= cur
        val = jnp.where(blk == 0, FORCE, jnp.where(blk >= cur - 1, FORCE, imp))
        vals.append(jnp.where(causal, val, -FORCE))

    val = jnp.concatenate(vals, axis=1)
    rank_ref[...] = jnp.zeros(rank_ref.shape, jnp.int32)
    grp_rows = lax.broadcasted_iota(jnp.int32, (RANK_GROUP, N_KV_HEADS * tq), 0)
    for ig in range(n_sel // RANK_GROUP):
        for jg in range(n_sel // RANK_GROUP):
            @pl.when(max(ig, jg) * RANK_GROUP <= last_blk)
            def _(ig=ig, jg=jg):
                vj = val[jg * RANK_GROUP:(jg + 1) * RANK_GROUP]
                cnt = jnp.zeros(vj.shape, jnp.int32)
                for ii in range(RANK_GROUP):
                    i = ig * RANK_GROUP + ii
                    vi = val[i:i + 1, :]
                    if jg > ig:
                        cnt = cnt + jnp.where(vi >= vj, 1, 0)
                    elif jg < ig:
                        cnt = cnt + jnp.where(vi > vj, 1, 0)
                    else:
                        tie = jnp.where(grp_rows > ii, 1, 0)
                        cnt = cnt + jnp.where(vi > vj, 1, jnp.where(vi == vj, tie, 0))
                sl = pl.ds(jg * RANK_GROUP, RANK_GROUP)
                rank_ref[sl, :] = rank_ref[sl, :] + cnt

    for g in range(N_KV_HEADS):
        rank = rank_ref[:, g * tq:(g + 1) * tq]
        bias_t = jnp.where(causal, jnp.where(rank < SLC_TOPN, 0.0, NEG_INF), NEG_INF)
        pieces = [jnp.zeros((HEAD_DIM, tq), F32), bias_t]
        if n_sel < LANES - HEAD_DIM:
            pieces.append(jnp.zeros((LANES - HEAD_DIM - n_sel, tq), F32))
        bias = jnp.concatenate(pieces, axis=0).T.astype(BF16)
        qaug_ref[g] = group_queries(g) + jnp.concatenate([bias] * GQA_GROUP, axis=0)

    n_chunks = (t0 + tq + CK - 1) // CK
    last = n_chunks - 1
    mx_ref[...] = jnp.full(mx_ref.shape, NEG_INF, F32)
    acc_ref[...] = jnp.zeros(acc_ref.shape, F32)
    s_refs = (s0_ref, s1_ref)

    def chunk_keys(ref, g, c):
        return ref[0, g, pl.ds(pl.multiple_of(c * CK, CK), CK), :]

    def score_chunk(g, c, diagonal):
        s = _dot_nt(qaug_ref[g], chunk_keys(ksa_ref, g, c))
        mx = mx_ref[g]
        slabs = []
        for j in range(CK // LANES):
            sj = s[:, j * LANES:(j + 1) * LANES]
            if diagonal:
                sj = jnp.where(diag_rep >= c * CK + j * LANES, sj, NEG_INF)
            mx = jnp.maximum(mx, sj)
            slabs.append(sj)
        s_refs[g][c] = jnp.concatenate(slabs, axis=1)
        mx_ref[g] = mx

    def value_chunk(g, c):
        s = s_refs[g][c]
        m = m_ref[g]
        p = jnp.concatenate([jnp.exp2(s[:, j * LANES:(j + 1) * LANES] - m) for j in range(CK // LANES)], axis=1)
        acc_ref[g] = acc_ref[g] + _dot(p.astype(BF16), chunk_keys(vsa_ref, g, c))

    def sweep(body, final):
        def pair(i, carry):
            body(2 * i)
            body(2 * i + 1)
            return carry

        def single(c, carry):
            body(c)
            return carry
        n_pairs = last // 2
        lax.fori_loop(0, n_pairs, pair, 0)
        lax.fori_loop(2 * n_pairs, last, single, 0)
        final()

    sweep(lambda c: score_chunk(0, c, False), lambda: score_chunk(0, last, True))
    m_ref[0] = row_max_rep(mx_ref[0])

    def both(c):
        value_chunk(0, c)
        score_chunk(1, c, False)

    def both_last():
        value_chunk(0, last)
        score_chunk(1, last, True)

    sweep(both, both_last)
    m_ref[1] = row_max_rep(mx_ref[1])
    sweep(lambda c: value_chunk(1, c), lambda: value_chunk(1, last))

    gates = gate_ref[...]
    g_hi = gates.astype(BF16)
    g_lo = (gates - g_hi.astype(F32)).astype(BF16)
    w0 = pl.multiple_of(jnp.maximum(t0 - WINDOW, 0), tq)
    n_pieces = (WINDOW + tq) // WIN_PIECE
    lane = lax.broadcasted_iota(jnp.int32, (tq, LANES), 1)
    for g in range(N_KV_HEADS):
        mw = None
        for j in range(n_pieces):
            sw = _dot_nt(qaug_ref[g], kwa_ref[0, g, pl.ds(w0 + j * WIN_PIECE, WIN_PIECE), :])
            slabs = []
            for jj in range(WIN_PIECE // LANES):
                slab = j * (WIN_PIECE // LANES) + jj
                k_first = w0 + slab * LANES
                sj = jnp.where(diag_rep >= k_first, sw[:, jj * LANES:(jj + 1) * LANES], NEG_INF)
                if slab < tq // LANES:
                    sj = jnp.where(diag_rep < k_first + WINDOW, sj, NEG_INF)
                mw = sj if mw is None else jnp.maximum(mw, sj)
                slabs.append(sj)
            s_refs[g][j, :, 0:WIN_PIECE] = jnp.concatenate(slabs, axis=1)
        mw = row_max_rep(mw)
        aw = None
        for j in range(n_pieces):
            sw = s_refs[g][j, :, 0:WIN_PIECE]
            pw = jnp.concatenate([jnp.exp2(sw[:, jj * LANES:(jj + 1) * LANES] - mw)
                                  for jj in range(WIN_PIECE // LANES)], axis=1)
            part = _dot(pw.astype(BF16), vwa_ref[0, g, pl.ds(w0 + j * WIN_PIECE, WIN_PIECE), :])
            aw = part if aw is None else aw + part
        def gate_rep(branch):
            e = gexp_ref[branch, g]
            spread = _dot(g_hi, e) + _dot(g_lo, e)
            return jnp.concatenate([spread[:, r * LANES:(r + 1) * LANES] for r in range(GQA_GROUP)], axis=0)

        asel = acc_ref[g]
        o = (gate_rep(0) * oc_ref[g]
             + gate_rep(1) * asel / jnp.maximum(pltpu.roll(asel, HEAD_DIM, 1), 1e-30)
             + gate_rep(2) * aw / jnp.maximum(pltpu.roll(aw, HEAD_DIM, 1), 1e-30))
        for pair in range(GQA_GROUP // 2):
            lo = o[(2 * pair) * tq:(2 * pair + 1) * tq]
            hi = o[(2 * pair + 1) * tq:(2 * pair + 2) * tq]
            col = g * GQA_GROUP * HEAD_DIM + pair * LANES
            y_ref[:, col:col + LANES] = jnp.where(lane < HEAD_DIM, lo, pltpu.roll(hi, HEAD_DIM, 1)).astype(BF16)


def _attention(q, kca, vca, ksa, vsa, kwa, vwa, gates, ovl, gexp, batch, seq):
    tq = TQ
    nq = seq // tq
    n_cmp = kca.shape[2]
    rows = GQA_GROUP * tq
    grp = pl.BlockSpec((1, N_KV_HEADS, seq, LANES), lambda b, i: (b, 0, 0, 0), pipeline_mode=pl.Buffered(1))
    cmp_grp = pl.BlockSpec((1, N_KV_HEADS, n_cmp, LANES), lambda b, i: (b, 0, 0, 0))
    grp_rows = lambda dt: pltpu.VMEM((N_KV_HEADS, rows, LANES), dt)
    return pl.pallas_call(
        _attention_kernel,
        grid=(batch, nq),
        in_specs=[pl.BlockSpec((tq, N_Q_HEADS * LANES), lambda b, i: (b * nq + i, 0)),
                  cmp_grp, cmp_grp, grp, grp, grp, grp,
                  pl.BlockSpec((tq, GATE_PAD), lambda b, i: (b * nq + i, 0)),
                  pl.BlockSpec(ovl.shape, lambda b, i: (0, 0)),
                  pl.BlockSpec(gexp.shape, lambda b, i: (0, 0, 0, 0))],
        out_specs=pl.BlockSpec((tq, ATTN_WIDTH), lambda b, i: (b * nq + i, 0)),
        out_shape=jax.ShapeDtypeStruct((batch * seq, ATTN_WIDTH), BF16),
        scratch_shapes=[grp_rows(BF16),
                        grp_rows(F32),
                        pltpu.VMEM((seq // CK, rows, CK), F32),
                        pltpu.VMEM((seq // CK, rows, CK), F32),
                        grp_rows(F32),
                        grp_rows(F32),
                        grp_rows(F32),
                        pltpu.VMEM((seq // SLC_BLOCK, N_KV_HEADS * tq), jnp.int32)],
        compiler_params=pltpu.CompilerParams(dimension_semantics=("parallel", "parallel"),
                                             vmem_limit_bytes=VMEM_LIMIT),
        name="attention",
    )(q, kca, vca, ksa, vsa, kwa, vwa, gates, ovl, gexp)


def _ssm_prep_kernel(a_re_ref, a_im_ref, log_dt_ref, b_re_ref, b_im_ref,
                     lam_re_ref, lam_im_ref, bb_re_ref, bb_im_ref):
    a_re = a_re_ref[...]
    a_im = a_im_ref[...]
    dt = jnp.exp(log_dt_ref[...])
    mag = jnp.exp(a_re * dt)
    lam_re = mag * jnp.cos(a_im * dt)
    lam_im = mag * jnp.sin(a_im * dt)
    den = a_re * a_re + a_im * a_im
    nr, ni = lam_re - 1.0, lam_im
    f_re = (nr * a_re + ni * a_im) / den
    f_im = (ni * a_re - nr * a_im) / den
    lam_re_ref[...] = lam_re
    lam_im_ref[...] = lam_im
    b_re = b_re_ref[...]
    b_im = b_im_ref[...]
    bb_re_ref[...] = f_re[:, None, :] * b_re - f_im[:, None, :] * b_im
    bb_im_ref[...] = f_re[:, None, :] * b_im + f_im[:, None, :] * b_re


def _ssm_prep(a_re, a_im, log_dt, b_re_t, b_im_t):
    gp = jax.ShapeDtypeStruct(a_re.shape, F32)
    gcp = jax.ShapeDtypeStruct(b_re_t.shape, F32)
    return pl.pallas_call(_ssm_prep_kernel, out_shape=[gp, gp, gcp, gcp], name="ssm_prep")(
        a_re, a_im, log_dt, b_re_t, b_im_t)


def _ssm_kernel(u_ref, wb_ref, lam_re_ref, lam_im_ref, wc_ref, d_ref, wglu_ref, y_ref, st_ref, xr_ref, xi_ref,
                *, batch):
    n_st = SSM_STATES_ALL
    steps = u_ref.shape[0] // batch

    @pl.when(pl.program_id(0) == 0)
    def _():
        xr_ref[...] = jnp.zeros(xr_ref.shape, F32)
        xi_ref[...] = jnp.zeros(xi_ref.shape, F32)

    rows = u_ref.shape[0]
    halves = [slice(h * rows // 2, (h + 1) * rows // 2) for h in range(2)]
    for sl in halves:
        st_ref[sl, :] = _dot(u_ref[sl, :], wb_ref[...])

    def step(t, carry):
        xr, xi = carry
        r0 = pl.multiple_of(t * batch, batch)
        lr = lam_re_ref[...]
        li = lam_im_ref[...]
        nr = lr * xr - li * xi + st_ref[pl.ds(r0, batch), 0:n_st]
        ni = lr * xi + li * xr + st_ref[pl.ds(r0, batch), n_st:2 * n_st]
        st_ref[pl.ds(r0, batch), 0:n_st] = nr
        st_ref[pl.ds(r0, batch), n_st:2 * n_st] = ni
        return nr, ni

    xr, xi = lax.fori_loop(0, steps, step, (xr_ref[...], xi_ref[...]))
    xr_ref[...] = xr
    xi_ref[...] = xi

    ys = [_dot(st_ref[sl, 0:n_st].astype(BF16), wc_ref[0:n_st, :])
          + _dot(st_ref[sl, n_st:2 * n_st].astype(BF16), wc_ref[n_st:2 * n_st, :]) for sl in halves]
    for sl, y in zip(halves, ys):
        y = y + d_ref[...] * u_ref[sl, :].astype(F32)
        z = _dot(_gelu_tanh(y).astype(BF16), wglu_ref[...])
        y_ref[sl, :] = (z[:, 0:SSM_WIDTH] * _sigmoid(z[:, SSM_WIDTH:2 * SSM_WIDTH])).astype(BF16)


def _ssm(u_tb, wb, lam_re_b, lam_im_b, wc, d_skip, w_glu, batch, seq):
    rows = SSM_LC * batch
    full = lambda a: pl.BlockSpec(a.shape, lambda i: (0,) * a.ndim)
    return pl.pallas_call(
        functools.partial(_ssm_kernel, batch=batch),
        grid=(seq // SSM_LC,),
        in_specs=[pl.BlockSpec((rows, SSM_WIDTH), lambda i: (i, 0)),
                  full(wb), full(lam_re_b), full(lam_im_b), full(wc), full(d_skip), full(w_glu)],
        out_specs=pl.BlockSpec((rows, SSM_WIDTH), lambda i: (i, 0)),
        out_shape=jax.ShapeDtypeStruct((seq * batch, SSM_WIDTH), BF16),
        scratch_shapes=[pltpu.VMEM((rows, 2 * SSM_STATES_ALL), F32),
                        pltpu.VMEM((batch, SSM_STATES_ALL), F32),
                        pltpu.VMEM((batch, SSM_STATES_ALL), F32)],
        compiler_params=pltpu.CompilerParams(dimension_semantics=("arbitrary",), vmem_limit_bytes=VMEM_LIMIT),
        name="ssm",
    )(u_tb, wb, lam_re_b, lam_im_b, wc, d_skip, w_glu)


def _conv_kernel(u_ref, wdw_ref, bdw_ref, lng_ref, lnb_ref, wpw_ref, bpw_ref, y_ref, pad_ref):
    seq = u_ref.shape[0]
    pad_ref[0:CONV_PAD, :] = jnp.zeros((CONV_PAD, CONV_WIDTH), F32)
    pad_ref[CONV_PAD:CONV_PAD + seq, :] = u_ref[...]
    wdw = wdw_ref[...]

    def tile(i, carry):
        t0 = pl.multiple_of(i * CONV_TC, CONV_TC)
        win_rows = CONV_TC + CONV_PAD
        win = pad_ref[pl.ds(t0, win_rows), :]
        acc = jnp.zeros((CONV_TC, CONV_WIDTH), F32)
        for b in range(SUBLANES):
            rolled = win if b == 0 else pltpu.roll(win, win_rows - b, 0)
            for a in range(win_rows // SUBLANES):
                k = a * SUBLANES + b - (CONV_PAD - (CONV_K - 1))
                if 0 <= k < CONV_K:
                    acc = acc + wdw[k:k + 1, :] * rolled[a * SUBLANES:a * SUBLANES + CONV_TC]
        y = acc + bdw_ref[...]
        mu = jnp.mean(y, axis=-1, keepdims=True)
        yc = y - mu
        var = jnp.mean(yc * yc, axis=-1, keepdims=True)
        yn = yc * lax.rsqrt(var + EPS) * lng_ref[...] + lnb_ref[...]
        act = yn * _sigmoid(yn)
        y_ref[pl.ds(t0, CONV_TC), :] = (_dot(act.astype(BF16), wpw_ref[...]) + bpw_ref[...]).astype(BF16)
        return carry

    lax.fori_loop(0, seq // CONV_TC, tile, 0)


def _conv(u, wdw, bdw, lng, lnb, wpw, bpw, batch, seq):
    full = lambda a: pl.BlockSpec(a.shape, lambda b: (0,) * a.ndim)
    return pl.pallas_call(
        _conv_kernel,
        grid=(batch,),
        in_specs=[pl.BlockSpec((seq, CONV_WIDTH), lambda b: (b, 0)),
                  full(wdw), full(bdw), full(lng), full(lnb), full(wpw), full(bpw)],
        out_specs=pl.BlockSpec((seq, CONV_WIDTH), lambda b: (b, 0)),
        out_shape=jax.ShapeDtypeStruct((batch * seq, CONV_WIDTH), BF16),
        scratch_shapes=[pltpu.VMEM((CONV_PAD + seq, CONV_WIDTH), F32)],
        compiler_params=pltpu.CompilerParams(dimension_semantics=("parallel",), vmem_limit_bytes=VMEM_LIMIT),
        name="conv",
    )(u, wdw, bdw, lng, lnb, wpw, bpw)


def _out_ffn_kernel(x_ref, ya_ref, ys_ref, yc_ref, ga_ref, gs_ref, gc_ref, wo_ref, gf_ref,
                    wg_ref, wu_ref, wd_ref, gfin_ref, o_ref, *, final_norm):
    na = _rms(ya_ref[...].astype(F32), ga_ref[...]).astype(BF16)
    ns = _rms(ys_ref[...].astype(F32), gs_ref[...]).astype(BF16)
    nc = _rms(yc_ref[...].astype(F32), gc_ref[...]).astype(BF16)
    o1, o2 = ATTN_WIDTH, ATTN_WIDTH + SSM_WIDTH
    x = x_ref[...] + _dot(na, wo_ref[0:o1, :]) + _dot(ns, wo_ref[o1:o2, :]) + _dot(nc, wo_ref[o2:, :])
    o_ref[...] = x
    h = _rms(x, gf_ref[...]).astype(BF16)
    gate = _dot(h, wg_ref[...])
    up = _dot(h, wu_ref[...])
    x = o_ref[...] + _dot((gate * _sigmoid(gate) * up).astype(BF16), wd_ref[...])
    if final_norm:
        x = _rms(x, gfin_ref[...])
    o_ref[...] = x


def _out_ffn(x2d, y_attn, y_ssm_tm, y_conv, ga, gs, gc, wo, gf, wg, wu, wd, gfin, batch, seq, final_norm):
    tm = TM_PROJ
    tiles_per_seq = seq // tm
    tokens = batch * seq
    row = lambda n: pl.BlockSpec((tm, n), lambda i: (i, 0))
    resident = lambda a: pl.BlockSpec(a.shape, lambda i: (0,) * a.ndim)
    return pl.pallas_call(
        functools.partial(_out_ffn_kernel, final_norm=final_norm),
        grid=(tokens // tm,),
        in_specs=[row(D_MODEL), row(ATTN_WIDTH),
                  pl.BlockSpec((tm, SSM_WIDTH), lambda i: (i % tiles_per_seq, i // tiles_per_seq)),
                  row(CONV_WIDTH),
                  resident(ga), resident(gs), resident(gc), resident(wo), resident(gf),
                  resident(wg), resident(wu), resident(wd), resident(gfin)],
        out_specs=row(D_MODEL),
        out_shape=jax.ShapeDtypeStruct((tokens, D_MODEL), F32),
        compiler_params=pltpu.CompilerParams(dimension_semantics=("parallel",), vmem_limit_bytes=VMEM_LIMIT),
        name="out_ffn",
    )(x2d, y_attn, y_ssm_tm, y_conv, ga, gs, gc, wo, gf, wg, wu, wd, gfin)


def _rope_tables(pos):
    half = HEAD_DIM // 2
    freqs = ROPE_THETA ** (-jnp.arange(half, dtype=F32) / half)
    ang = pos.astype(F32)[:, None] * freqs[None, :]
    cos = jnp.tile(jnp.cos(ang), (1, LANES // half))
    sin = jnp.sin(ang)
    sin_signed = jnp.tile(jnp.concatenate([-sin, sin], axis=1), (1, LANES // HEAD_DIM))
    return cos, sin_signed


def _pack_w_in(w_in):
    sizes = (ATTN_WIDTH,) + (KV_WIDTH,) * 6 + (N_BRANCH * N_Q_HEADS, SSM_WIDTH, 2 * CONV_WIDTH)
    parts = jnp.split(w_in, np.cumsum(sizes)[:-1].tolist(), axis=1)
    gate = jnp.pad(parts[7], ((0, 0), (0, GATE_PAD - N_BRANCH * N_Q_HEADS)))
    return jnp.concatenate(parts[:7] + parts[8:] + [gate], axis=1).astype(BF16)


def _pack_cmp_w1(w1):
    w = w1.reshape(2, CMP_STRIDE, HEAD_DIM, CMP_HIDDEN)
    eye = jnp.eye(N_KV_HEADS, dtype=w1.dtype)
    big = jnp.einsum('rpdh,ab->padrbh', w, eye)
    return big.reshape(CMP_STRIDE * N_KV_HEADS * HEAD_DIM, 2 * N_KV_HEADS * CMP_HIDDEN).astype(BF16)


def _pe_operand(pe):
    lo = jnp.tile(pe[:CMP_STRIDE, None, :], (1, N_KV_HEADS, 1)).reshape(1, -1)
    hi = jnp.tile(pe[CMP_STRIDE:, None, :], (1, N_KV_HEADS, 1)).reshape(1, -1)
    return jnp.concatenate([jnp.tile(lo, (4, 1)), jnp.tile(hi, (4, 1))], axis=0).astype(BF16)


def _block_diag(blocks):
    g, a, b = blocks.shape
    eye = jnp.eye(g, dtype=blocks.dtype)
    return jnp.einsum('gab,gh->gahb', blocks, eye).reshape(g * a, g * b)


def _overlap_matrix(seq):
    n_cmp = seq // CMP_STRIDE
    n_sel = seq // SLC_BLOCK
    ci = np.arange(n_cmp)[None, :]
    sj = np.arange(n_sel)[:, None]
    ovl = (ci * CMP_STRIDE < (sj + 1) * SLC_BLOCK) & (ci * CMP_STRIDE + CMP_LEN > sj * SLC_BLOCK) & (ci < n_cmp - 1)
    return jnp.asarray(ovl.astype(np.float32), dtype=BF16)


def _gate_spread_matrix():
    e = np.zeros((N_BRANCH, N_KV_HEADS, GATE_PAD, GQA_GROUP * LANES), np.float32)
    for br in range(N_BRANCH):
        for g in range(N_KV_HEADS):
            for r in range(GQA_GROUP):
                e[br, g, br * N_Q_HEADS + g * GQA_GROUP + r, r * LANES:(r + 1) * LANES] = 1.0
    return jnp.asarray(e, dtype=BF16)


def _layer(x2d, p, batch, seq, tables, final_norm, norm_final):
    cos, sin, cos_c, sin_c, ovl, gexp = tables
    row = lambda v: v.reshape(1, -1).astype(F32)

    q, kc_raw, vc_raw, ksa, vsa, kwa, vwa, gates, u_ssm, u_conv = _in_proj(
        x2d, row(p['norm_mix']), _pack_w_in(p['w_in']), cos, sin, batch, seq)

    n_half = seq // CMP_STRIDE
    kh = kc_raw.reshape(batch, n_half, CMP_STRIDE * KV_WIDTH)
    vh = vc_raw.reshape(batch, n_half, CMP_STRIDE * KV_WIDTH)

    wk = _pack_cmp_w1(p['cmp_k_w1'])
    wv = _pack_cmp_w1(p['cmp_v_w1'])
    pek = _pe_operand(p['cmp_pe_k'])
    pev = _pe_operand(p['cmp_pe_v'])
    kca, vca = _compress(kh, vh, wk, wv, pek, pev, p['cmp_k_w2'].astype(BF16), p['cmp_v_w2'].astype(BF16),
                       cos_c, sin_c)

    y_attn = _attention(q, kca, vca, ksa, vsa, kwa, vwa, gates, ovl, gexp, batch, seq)

    lam_re, lam_im, bb_re, bb_im = _ssm_prep(
        p['ssm_a_re'], p['ssm_a_im'], p['ssm_log_dt'].reshape(-1, 1),
        p['ssm_b_re'].transpose(0, 2, 1), p['ssm_b_im'].transpose(0, 2, 1))
    wb = jnp.concatenate([_block_diag(bb_re), _block_diag(bb_im)], axis=1).astype(BF16)
    wc = jnp.concatenate([_block_diag(p['ssm_c_re'].transpose(0, 2, 1)),
                          -_block_diag(p['ssm_c_im'].transpose(0, 2, 1))], axis=0).astype(BF16)
    lam_re_b = jnp.broadcast_to(lam_re.reshape(1, -1), (batch, SSM_STATES_ALL))
    lam_im_b = jnp.broadcast_to(lam_im.reshape(1, -1), (batch, SSM_STATES_ALL))
    y_ssm = _ssm(u_ssm.reshape(seq * batch, SSM_WIDTH), wb, lam_re_b, lam_im_b, wc, row(p['ssm_d']),
                 p['ssm_w_glu'].astype(BF16), batch, seq)
    y_ssm_tm = y_ssm.reshape(seq, batch * SSM_WIDTH)

    wdw = jnp.pad(p['conv_w_dw'], ((0, CONV_PAD - CONV_K), (0, 0)))
    y_conv = _conv(u_conv, wdw, row(p['conv_b_dw']), row(p['conv_ln_g']), row(p['conv_ln_b']),
                   p['conv_w_pw'].astype(BF16), row(p['conv_b_pw']), batch, seq)

    return _out_ffn(x2d, y_attn, y_ssm_tm, y_conv, row(p['norm_out_attn']), row(p['norm_out_ssm']),
                    row(p['norm_out_conv']), p['w_out'].astype(BF16), row(p['norm_ffn']),
                    p['w_gate'].astype(BF16), p['w_up'].astype(BF16), p['w_down'].astype(BF16),
                    row(norm_final), batch, seq, final_norm)


_LAYER_KEYS = ('norm_mix', 'w_in', 'cmp_pe_k', 'cmp_k_w1', 'cmp_k_w2', 'cmp_pe_v', 'cmp_v_w1', 'cmp_v_w2',
               'ssm_a_re', 'ssm_a_im', 'ssm_log_dt', 'ssm_b_re', 'ssm_b_im', 'ssm_c_re', 'ssm_c_im', 'ssm_d',
               'ssm_w_glu', 'conv_w_dw', 'conv_b_dw', 'conv_ln_g', 'conv_ln_b', 'conv_w_pw', 'conv_b_pw',
               'norm_out_attn', 'norm_out_ssm', 'norm_out_conv', 'w_out', 'norm_ffn', 'w_gate', 'w_up', 'w_down')


def kernel(x, norm_mix, w_in, cmp_pe_k, cmp_k_w1, cmp_k_w2, cmp_pe_v, cmp_v_w1, cmp_v_w2, ssm_a_re, ssm_a_im, ssm_log_dt, ssm_b_re, ssm_b_im, ssm_c_re, ssm_c_im, ssm_d, ssm_w_glu, conv_w_dw, conv_b_dw, conv_ln_g, conv_ln_b, conv_w_pw, conv_b_pw, norm_out_attn, norm_out_ssm, norm_out_conv, w_out, norm_ffn, w_gate, w_up, w_down, norm_final):
    stacked = dict(zip(_LAYER_KEYS, (norm_mix, w_in, cmp_pe_k, cmp_k_w1, cmp_k_w2, cmp_pe_v, cmp_v_w1, cmp_v_w2,
                                     ssm_a_re, ssm_a_im, ssm_log_dt, ssm_b_re, ssm_b_im, ssm_c_re, ssm_c_im, ssm_d,
                                     ssm_w_glu, conv_w_dw, conv_b_dw, conv_ln_g, conv_ln_b, conv_w_pw, conv_b_pw,
                                     norm_out_attn, norm_out_ssm, norm_out_conv, w_out, norm_ffn, w_gate, w_up,
                                     w_down)))
    batch, seq, _ = x.shape
    depth = norm_mix.shape[0]
    cos, sin = _rope_tables(jnp.arange(seq))
    cos_c, sin_c = _rope_tables(jnp.arange(seq // CMP_STRIDE) * CMP_STRIDE + CMP_LEN - 1)
    tables = (cos, sin, cos_c, sin_c, _overlap_matrix(seq), _gate_spread_matrix())
    x2d = x.reshape(batch * seq, D_MODEL)
    for l in range(depth):
        p = {k: v[l] for k, v in stacked.items()}
        x2d = _layer(x2d, p, batch, seq, tables, l == depth - 1, norm_final)
    return x2d.reshape(batch, seq, D_MODEL)
```

```python
import functools
import math

import numpy as np
import jax
import jax.numpy as jnp
from jax import lax
from jax.experimental import pallas as pl
from jax.experimental.pallas import tpu as pltpu

F32 = jnp.float32
BF16 = jnp.bfloat16

D_MODEL = 1024
HEAD_DIM = 64
N_Q_HEADS = 8
N_KV_HEADS = 2
GQA_GROUP = N_Q_HEADS // N_KV_HEADS
ATTN_WIDTH = N_Q_HEADS * HEAD_DIM
KV_WIDTH = N_KV_HEADS * HEAD_DIM
N_BRANCH = 3
SSM_WIDTH = D_MODEL // 4
SSM_GROUP = 16
SSM_N_GROUPS = SSM_WIDTH // SSM_GROUP
SSM_STATE = 64
SSM_STATES_ALL = SSM_N_GROUPS * SSM_STATE
CONV_WIDTH = D_MODEL // 4
CONV_K = 31
CMP_LEN = 32
CMP_STRIDE = 16
CMP_HIDDEN = 256
SLC_BLOCK = 64
SLC_TOPN = 16
WINDOW = 512
ROPE_THETA = 10000.0
D_FF = ((8 * D_MODEL // 3 + 255) // 256) * 256
EPS = 1e-6
NEG_INF = -1e30
Q_SCALE = HEAD_DIM ** -0.5 * math.log2(math.e)
FORCE = 1e9

LANES = 128
SUBLANES = 8
VMEM_LIMIT = 56 * 1024 * 1024

GATE_PAD = LANES
PROJ_BLOCK = 512
C_Q = 0
C_KC = C_Q + ATTN_WIDTH
C_VC = C_KC + KV_WIDTH
C_KS = C_VC + KV_WIDTH
C_VS = C_KS + KV_WIDTH
C_KW = C_VS + KV_WIDTH
C_VW = C_KW + KV_WIDTH
C_SSM = C_VW + KV_WIDTH
C_CONV = C_SSM + SSM_WIDTH
C_GATE = C_CONV + 2 * CONV_WIDTH
IN_PACKED = C_GATE + GATE_PAD

TM_PROJ = 512
TQ = 256
CK = 512
WIN_PIECE = 256
RANK_GROUP = 16
SSM_LC = 128
CONV_TC = 128
CONV_PAD = 32


def _gelu_tanh(x):
    return 0.5 * x * (1.0 + jnp.tanh(math.sqrt(2.0 / math.pi) * (x + 0.044715 * (x * x * x))))


def _sigmoid(x):
    return 1.0 / (1.0 + jnp.exp(-x))


def _rms(x, g):
    return x * lax.rsqrt(jnp.mean(x * x, axis=-1, keepdims=True) + EPS) * g


def _dot(a, b):
    return jnp.dot(a, b, preferred_element_type=F32)


def _dot_nt(a, b):
    return lax.dot_general(a, b, (((1,), (1,)), ((), ())), preferred_element_type=F32)


def _rope_rotate(v, cos, sin_signed):
    lane = lax.broadcasted_iota(jnp.int32, v.shape, 1)
    first_half = (lane % HEAD_DIM) < (HEAD_DIM // 2)
    rot = jnp.where(first_half, pltpu.roll(v, LANES - HEAD_DIM // 2, 1), pltpu.roll(v, HEAD_DIM // 2, 1))
    return v * cos + rot * sin_signed


def _in_proj_kernel(x_ref, g_ref, w_ref, cos_ref, sin_ref,
                    q_ref, kc_ref, vc_ref, ksa_ref, vsa_ref, kwa_ref, vwa_ref, gate_ref, ussm_ref, uconv_ref,
                    *, tiles_per_seq):
    tm = x_ref.shape[0]
    h = _rms(x_ref[...], g_ref[...]).astype(BF16)
    cos = cos_ref[...]
    sin = sin_ref[...]

    blocks = {}

    def proj(c0, n):
        b0 = (c0 // PROJ_BLOCK) * PROJ_BLOCK
        if b0 not in blocks:
            blocks[b0] = _dot(h, w_ref[:, b0:min(b0 + PROJ_BLOCK, IN_PACKED)])
        return blocks[b0][:, c0 - b0:c0 - b0 + n]

    lane = lax.broadcasted_iota(jnp.int32, (tm, LANES), 1)
    row = lax.broadcasted_iota(jnp.int32, (tm, LANES), 0)
    t = (pl.program_id(0) % tiles_per_seq) * tm + row
    low = lane < HEAD_DIM
    for j in range(ATTN_WIDTH // LANES):
        qj = _rope_rotate(proj(C_Q + j * LANES, LANES), cos, sin) * Q_SCALE
        q_ref[:, (2 * j) * LANES:(2 * j + 1) * LANES] = jnp.where(low, qj, 0.0).astype(BF16)
        q_ref[:, (2 * j + 1) * LANES:(2 * j + 2) * LANES] = jnp.where(low, pltpu.roll(qj, HEAD_DIM, 1), 0.0).astype(BF16)
    kc_ref[...] = proj(C_KC, KV_WIDTH).astype(BF16)
    vc_ref[...] = proj(C_VC, KV_WIDTH).astype(BF16)
    blk_onehot = jnp.where((t // SLC_BLOCK) == (lane - HEAD_DIM), 1.0, 0.0)
    ones_up = jnp.where(low, 0.0, 1.0)

    def split_groups(v, upper, ref):
        ref[0, 0] = jnp.where(low, v, upper).astype(BF16)
        ref[0, 1] = jnp.where(low, pltpu.roll(v, HEAD_DIM, 1), upper).astype(BF16)

    split_groups(_rope_rotate(proj(C_KS, KV_WIDTH), cos, sin), blk_onehot, ksa_ref)
    split_groups(proj(C_VS, KV_WIDTH), ones_up, vsa_ref)
    split_groups(_rope_rotate(proj(C_KW, KV_WIDTH), cos, sin), 0.0, kwa_ref)
    split_groups(proj(C_VW, KV_WIDTH), ones_up, vwa_ref)

    gate_ref[...] = _sigmoid(proj(C_GATE, GATE_PAD))
    ussm_ref[...] = proj(C_SSM, SSM_WIDTH).astype(BF16)
    a = proj(C_CONV, CONV_WIDTH)
    g = proj(C_CONV + CONV_WIDTH, CONV_WIDTH)
    uconv_ref[...] = a * _sigmoid(g)


def _in_proj(x2d, g, w_packed, cos, sin, batch, seq):
    tm = TM_PROJ
    tiles_per_seq = seq // tm
    tokens = batch * seq
    row_blk = lambda n, dt: (pl.BlockSpec((tm, n), lambda i: (i, 0)), jax.ShapeDtypeStruct((tokens, n), dt))
    grp_blk = (pl.BlockSpec((1, N_KV_HEADS, tm, LANES), lambda i: (i // tiles_per_seq, 0, i % tiles_per_seq, 0)),
               jax.ShapeDtypeStruct((batch, N_KV_HEADS, seq, LANES), BF16))
    outs = [row_blk(N_Q_HEADS * LANES, BF16), row_blk(KV_WIDTH, BF16), row_blk(KV_WIDTH, BF16),
            grp_blk, grp_blk, grp_blk, grp_blk, row_blk(GATE_PAD, F32),
            (pl.BlockSpec((tm, SSM_WIDTH), lambda i: (i % tiles_per_seq, i // tiles_per_seq)),
             jax.ShapeDtypeStruct((seq, batch * SSM_WIDTH), BF16)),
            row_blk(CONV_WIDTH, F32)]
    return pl.pallas_call(
        functools.partial(_in_proj_kernel, tiles_per_seq=tiles_per_seq),
        grid=(tokens // tm,),
        in_specs=[pl.BlockSpec((tm, D_MODEL), lambda i: (i, 0)),
                  pl.BlockSpec((1, D_MODEL), lambda i: (0, 0)),
                  pl.BlockSpec((D_MODEL, IN_PACKED), lambda i: (0, 0)),
                  pl.BlockSpec((tm, LANES), lambda i: (i % tiles_per_seq, 0)),
                  pl.BlockSpec((tm, LANES), lambda i: (i % tiles_per_seq, 0))],
        out_specs=[o[0] for o in outs],
        out_shape=[o[1] for o in outs],
        compiler_params=pltpu.CompilerParams(dimension_semantics=("parallel",), vmem_limit_bytes=VMEM_LIMIT),
        name="in_proj",
    )(x2d, g, w_packed, cos, sin)


def _compress_kernel(kh_ref, vh_ref, wk_ref, wv_ref, pek_ref, pev_ref, w2k_ref, w2v_ref, cos_ref, sin_ref,
                     kca_ref, vca_ref):
    n_half = kh_ref.shape[1]
    hw = N_KV_HEADS * CMP_HIDDEN

    def compress(h_ref, w_ref, pe_ref, w2_ref):
        ab = _dot(h_ref[0], w_ref[...])
        pe = _dot(pe_ref[...], w_ref[...])
        bias = pe[0:1, 0:hw] + pe[4:5, hw:2 * hw]
        nxt = pltpu.roll(ab[:, hw:2 * hw], n_half - 1, 0)
        hid = _gelu_tanh(ab[:, 0:hw] + nxt + bias).astype(BF16)
        return jnp.concatenate([_dot(hid[:, j * CMP_HIDDEN:(j + 1) * CMP_HIDDEN], w2_ref[...])
                                for j in range(N_KV_HEADS)], axis=1)

    def split_groups(v, upper, ref):
        low = lax.broadcasted_iota(jnp.int32, v.shape, 1) < HEAD_DIM
        ref[0, 0] = jnp.where(low, v, upper).astype(BF16)
        ref[0, 1] = jnp.where(low, pltpu.roll(v, HEAD_DIM, 1), upper).astype(BF16)

    kc = compress(kh_ref, wk_ref, pek_ref, w2k_ref)
    split_groups(_rope_rotate(kc, cos_ref[...], sin_ref[...]), 0.0, kca_ref)
    split_groups(compress(vh_ref, wv_ref, pev_ref, w2v_ref), 1.0, vca_ref)


def _compress(kh, vh, wk, wv, pek, pev, w2k, w2v, cos_c, sin_c):
    batch, n_half, width = kh.shape
    full = lambda a: pl.BlockSpec(a.shape, lambda b: (0,) * a.ndim)
    per_b = pl.BlockSpec((1, n_half, width), lambda b: (b, 0, 0))
    out_blk = pl.BlockSpec((1, N_KV_HEADS, n_half, LANES), lambda b: (b, 0, 0, 0))
    out_shape = jax.ShapeDtypeStruct((batch, N_KV_HEADS, n_half, LANES), BF16)
    return pl.pallas_call(
        _compress_kernel,
        grid=(batch,),
        in_specs=[per_b, per_b, full(wk), full(wv), full(pek), full(pev), full(w2k), full(w2v),
                  full(cos_c), full(sin_c)],
        out_specs=[out_blk, out_blk],
        out_shape=[out_shape, out_shape],
        compiler_params=pltpu.CompilerParams(dimension_semantics=("parallel",), vmem_limit_bytes=VMEM_LIMIT),
        name="compress",
    )(kh, vh, wk, wv, pek, pev, w2k, w2v, cos_c, sin_c)


def _attention_kernel(q_ref, kca_ref, vca_ref, ksa_ref, vsa_ref, kwa_ref, vwa_ref, gate_ref, ovl_ref, gexp_ref,
                      y_ref,
                      qaug_ref, oc_ref, s0_ref, s1_ref, mx_ref, m_ref, acc_ref, val_ref, rank_ref):
    tq = q_ref.shape[0]
    rows = GQA_GROUP * tq
    n_cmp = kca_ref.shape[2]
    seq = ksa_ref.shape[2]
    n_sel = seq // SLC_BLOCK
    t0 = pl.program_id(1) * tq
    row_q = lax.broadcasted_iota(jnp.int32, (rows, 1), 0) % tq
    t_row = t0 + row_q
    t_rep = t0 + lax.broadcasted_iota(jnp.int32, (rows, LANES), 0) % tq
    lane_iota = lax.broadcasted_iota(jnp.int32, (rows, LANES), 1)
    diag_rep = t_rep - lane_iota
    cmp_rep = t_rep - lane_iota * CMP_STRIDE - (CMP_LEN - 1)
    last_blk = (t0 + tq - 1) // SLC_BLOCK

    def row_max_rep(x):
        return jnp.broadcast_to(jnp.max(x, axis=-1, keepdims=True), x.shape)

    def sums_on_all_lanes(a):
        return jnp.where(lane_iota < HEAD_DIM, pltpu.roll(a, HEAD_DIM, 1), a)

    def group_queries(g):
        heads = [q_ref[:, (g * GQA_GROUP + r) * LANES:(g * GQA_GROUP + r + 1) * LANES] for r in range(GQA_GROUP)]
        return jnp.concatenate(heads, axis=0)

    blk = lax.broadcasted_iota(jnp.int32, (n_sel, tq), 0)
    cur = (t0 + lax.broadcasted_iota(jnp.int32, (n_sel, tq), 1)) // SLC_BLOCK
    causal = blk <= cur

    def compressed_branch(n_slabs):
        n_used = n_slabs * LANES
        for g in range(N_KV_HEADS):
            sc = _dot_nt(group_queries(g), kca_ref[0, g, 0:n_used, :])
            slabs = []
            for j in range(n_slabs):
                slabs.append(jnp.where(cmp_rep >= j * LANES * CMP_STRIDE, sc[:, j * LANES:(j + 1) * LANES],
                                       NEG_INF))
            mc = slabs[0]
            for sj in slabs[1:]:
                mc = jnp.maximum(mc, sj)
            mc = row_max_rep(mc)
            pc = jnp.concatenate([jnp.exp2(sj - mc) for sj in slabs], axis=1)
            oc = _dot(pc.astype(BF16), vca_ref[0, g, 0:n_used, :])
            inv = jnp.where(t_rep >= CMP_LEN - 1, 1.0 / jnp.maximum(sums_on_all_lanes(oc), 1e-30), 0.0)
            oc_ref[g] = oc * inv
            pc = pc * jnp.concatenate([inv] * n_slabs, axis=1)

            pc_sum = pc[0:tq]
            for r in range(1, GQA_GROUP):
                pc_sum = pc_sum + pc[r * tq:(r + 1) * tq]
            p_hi = pc_sum.astype(BF16)
            rem = pc_sum - p_hi.astype(F32)
            p_mid = rem.astype(BF16)
            p_lo = (rem - p_mid.astype(F32)).astype(BF16)
            ovl = ovl_ref[:, 0:n_used]
            imp = _dot_nt(ovl, p_hi) + _dot_nt(ovl, p_mid) + _dot_nt(ovl, p_lo)
            val = jnp.where(blk == 0, FORCE, jnp.where(blk >= cur - 1, FORCE, imp))
            val_ref[:, g * tq:(g + 1) * tq] = jnp.where(causal, val, -FORCE)

    compressed_branch(n_cmp // LANES)

    val = val_ref[...]
    rank_ref[...] = jnp.zeros(rank_ref.shape, jnp.int32)
    grp_rows = lax.broadcasted_iota(jnp.int32, (RANK_GROUP, N_KV_HEADS * tq), 0)
    for ig in range(n_sel // RANK_GROUP):
        for jg in range(n_sel // RANK_GROUP):
            @pl.when(max(ig, jg) * RANK_GROUP <= last_blk)
            def _(ig=ig, jg=jg):
                vj = val[jg * RANK_GROUP:(jg + 1) * RANK_GROUP]
                cnt = jnp.zeros(vj.shape, jnp.int32)
                for ii in range(RANK_GROUP):
                    i = ig * RANK_GROUP + ii
                    vi = val[i:i + 1, :]
                    if jg > ig:
                        cnt = cnt + jnp.where(vi >= vj, 1, 0)
                    elif jg < ig:
                        cnt = cnt + jnp.where(vi > vj, 1, 0)
                    else:
                        tie = jnp.where(grp_rows > ii, 1, 0)
                        cnt = cnt + jnp.where(vi > vj, 1, jnp.where(vi == vj, tie, 0))
                sl = pl.ds(jg * RANK_GROUP, RANK_GROUP)
                rank_ref[sl, :] = rank_ref[sl, :] + cnt

    for g in range(N_KV_HEADS):
        rank = rank_ref[:, g * tq:(g + 1) * tq]
        bias_t = jnp.where(causal, jnp.where(rank < SLC_TOPN, 0.0, NEG_INF), NEG_INF)
        pieces = [jnp.zeros((HEAD_DIM, tq), F32), bias_t]
        if n_sel < LANES - HEAD_DIM:
            pieces.append(jnp.zeros((LANES - HEAD_DIM - n_sel, tq), F32))
        bias = jnp.concatenate(pieces, axis=0).T.astype(BF16)
        qaug_ref[g] = group_queries(g) + jnp.concatenate([bias] * GQA_GROUP, axis=0)

    n_chunks = (t0 + tq + CK - 1) // CK
    last = n_chunks - 1
    mx_ref[...] = jnp.full(mx_ref.shape, NEG_INF, F32)
    acc_ref[...] = jnp.zeros(acc_ref.shape, F32)
    s_refs = (s0_ref, s1_ref)

    def chunk_keys(ref, g, c, width=CK):
        return ref[0, g, pl.ds(pl.multiple_of(c * CK, CK), width), :]

    def score_chunk(g, c, diagonal, width=CK):
        s = _dot_nt(qaug_ref[g], chunk_keys(ksa_ref, g, c, width))
        mx = mx_ref[g]
        slabs = []
        for j in range(width // LANES):
            sj = s[:, j * LANES:(j + 1) * LANES]
            if diagonal:
                sj = jnp.where(diag_rep >= c * CK + j * LANES, sj, NEG_INF)
            mx = jnp.maximum(mx, sj)
            slabs.append(sj)
        s_refs[g][c, :, 0:width] = jnp.concatenate(slabs, axis=1)
        mx_ref[g] = mx

    def value_chunk(g, c, width=CK):
        s = s_refs[g][c, :, 0:width]
        m = m_ref[g]
        p = jnp.concatenate([jnp.exp2(s[:, j * LANES:(j + 1) * LANES] - m) for j in range(width // LANES)], axis=1)
        acc_ref[g] = acc_ref[g] + _dot(p.astype(BF16), chunk_keys(vsa_ref, g, c, width))

    last_is_half = t0 + tq - last * CK <= CK // 2

    def sweep(body, final):
        def pair(i, carry):
            body(2 * i)
            body(2 * i + 1)
            return carry

        def single(c, carry):
            body(c)
            return carry
        n_pairs = last // 2
        lax.fori_loop(0, n_pairs, pair, 0)
        lax.fori_loop(2 * n_pairs, last, single, 0)
        pl.when(last_is_half)(functools.partial(final, CK // 2))
        pl.when(jnp.logical_not(last_is_half))(functools.partial(final, CK))

    w0 = pl.multiple_of(jnp.maximum(t0 - WINDOW, 0), tq)
    n_pieces = (WINDOW + tq) // WIN_PIECE

    def window_branch(g, stage_ref):
        mw = None
        for j in range(n_pieces):
            sw = _dot_nt(qaug_ref[g], kwa_ref[0, g, pl.ds(w0 + j * WIN_PIECE, WIN_PIECE), :])
            slabs = []
            for jj in range(WIN_PIECE // LANES):
                slab = j * (WIN_PIECE // LANES) + jj
                k_first = w0 + slab * LANES
                sj = jnp.where(diag_rep >= k_first, sw[:, jj * LANES:(jj + 1) * LANES], NEG_INF)
                if slab < tq // LANES:
                    sj = jnp.where(diag_rep < k_first + WINDOW, sj, NEG_INF)
                mw = sj if mw is None else jnp.maximum(mw, sj)
                slabs.append(sj)
            stage_ref[j, :, 0:WIN_PIECE] = jnp.concatenate(slabs, axis=1)
        mw = row_max_rep(mw)
        aw = None
        for j in range(n_pieces):
            sw = stage_ref[j, :, 0:WIN_PIECE]
            pw = jnp.concatenate([jnp.exp2(sw[:, jj * LANES:(jj + 1) * LANES] - mw)
                                  for jj in range(WIN_PIECE // LANES)], axis=1)
            part = _dot(pw.astype(BF16), vwa_ref[0, g, pl.ds(w0 + j * WIN_PIECE, WIN_PIECE), :])
            aw = part if aw is None else aw + part
        mx_ref[g] = aw

    sweep(lambda c: score_chunk(0, c, False), lambda width: score_chunk(0, last, True, width))
    m_ref[0] = row_max_rep(mx_ref[0])

    def both(c):
        value_chunk(0, c)
        score_chunk(1, c, False)

    def both_last(width):
        value_chunk(0, last, width)
        score_chunk(1, last, True, width)

    sweep(both, both_last)
    m_ref[1] = row_max_rep(mx_ref[1])
    sweep(lambda c: value_chunk(1, c), lambda width: value_chunk(1, last, width))
    for g in range(N_KV_HEADS):
        window_branch(g, s_refs[g])

    gates = gate_ref[...]
    g_hi = gates.astype(BF16)
    g_lo = (gates - g_hi.astype(F32)).astype(BF16)
    lane = lax.broadcasted_iota(jnp.int32, (tq, LANES), 1)
    for g in range(N_KV_HEADS):
        def gate_rep(branch):
            e = gexp_ref[branch, g]
            spread = _dot(g_hi, e) + _dot(g_lo, e)
            return jnp.concatenate([spread[:, r * LANES:(r + 1) * LANES] for r in range(GQA_GROUP)], axis=0)

        asel = acc_ref[g]
        aw = mx_ref[g]
        o = (gate_rep(0) * oc_ref[g]
             + gate_rep(1) * asel / jnp.maximum(pltpu.roll(asel, HEAD_DIM, 1), 1e-30)
             + gate_rep(2) * aw / jnp.maximum(pltpu.roll(aw, HEAD_DIM, 1), 1e-30))
        for pair in range(GQA_GROUP // 2):
            lo = o[(2 * pair) * tq:(2 * pair + 1) * tq]
            hi = o[(2 * pair + 1) * tq:(2 * pair + 2) * tq]
            col = g * GQA_GROUP * HEAD_DIM + pair * LANES
            y_ref[:, col:col + LANES] = jnp.where(lane < HEAD_DIM, lo, pltpu.roll(hi, HEAD_DIM, 1)).astype(BF16)


def _attention(q, kca, vca, ksa, vsa, kwa, vwa, gates, ovl, gexp, batch, seq):
    tq = TQ
    nq = seq // tq
    n_cmp = kca.shape[2]
    rows = GQA_GROUP * tq
    grp = pl.BlockSpec((1, N_KV_HEADS, seq, LANES), lambda b, i: (b, 0, 0, 0), pipeline_mode=pl.Buffered(1))
    cmp_grp = pl.BlockSpec((1, N_KV_HEADS, n_cmp, LANES), lambda b, i: (b, 0, 0, 0))
    grp_rows = lambda dt: pltpu.VMEM((N_KV_HEADS, rows, LANES), dt)
    return pl.pallas_call(
        _attention_kernel,
        grid=(batch, nq),
        in_specs=[pl.BlockSpec((tq, N_Q_HEADS * LANES), lambda b, i: (b * nq + i, 0)),
                  cmp_grp, cmp_grp, grp, grp, grp, grp,
                  pl.BlockSpec((tq, GATE_PAD), lambda b, i: (b * nq + i, 0)),
                  pl.BlockSpec(ovl.shape, lambda b, i: (0, 0)),
                  pl.BlockSpec(gexp.shape, lambda b, i: (0, 0, 0, 0))],
        out_specs=pl.BlockSpec((tq, ATTN_WIDTH), lambda b, i: (b * nq + i, 0)),
        out_shape=jax.ShapeDtypeStruct((batch * seq, ATTN_WIDTH), BF16),
        scratch_shapes=[grp_rows(BF16),
                        grp_rows(F32),
                        pltpu.VMEM((seq // CK, rows, CK), F32),
                        pltpu.VMEM((seq // CK, rows, CK), F32),
                        grp_rows(F32),
                        grp_rows(F32),
                        grp_rows(F32),
                        pltpu.VMEM((seq // SLC_BLOCK, N_KV_HEADS * tq), F32),
                        pltpu.VMEM((seq // SLC_BLOCK, N_KV_HEADS * tq), jnp.int32)],
        compiler_params=pltpu.CompilerParams(dimension_semantics=("parallel", "parallel"),
                                             vmem_limit_bytes=VMEM_LIMIT),
        name="attention",
    )(q, kca, vca, ksa, vsa, kwa, vwa, gates, ovl, gexp)


def _ssm_prep_kernel(a_re_ref, a_im_ref, log_dt_ref, b_re_ref, b_im_ref,
                     lam_re_ref, lam_im_ref, bb_re_ref, bb_im_ref):
    a_re = a_re_ref[...]
    a_im = a_im_ref[...]
    dt = jnp.exp(log_dt_ref[...])
    mag = jnp.exp(a_re * dt)
    lam_re = mag * jnp.cos(a_im * dt)
    lam_im = mag * jnp.sin(a_im * dt)
    den = a_re * a_re + a_im * a_im
    nr, ni = lam_re - 1.0, lam_im
    f_re = (nr * a_re + ni * a_im) / den
    f_im = (ni * a_re - nr * a_im) / den
    lam_re_ref[...] = lam_re
    lam_im_ref[...] = lam_im
    b_re = b_re_ref[...]
    b_im = b_im_ref[...]
    bb_re_ref[...] = f_re[:, None, :] * b_re - f_im[:, None, :] * b_im
    bb_im_ref[...] = f_re[:, None, :] * b_im + f_im[:, None, :] * b_re


def _ssm_prep(a_re, a_im, log_dt, b_re_t, b_im_t):
    gp = jax.ShapeDtypeStruct(a_re.shape, F32)
    gcp = jax.ShapeDtypeStruct(b_re_t.shape, F32)
    return pl.pallas_call(_ssm_prep_kernel, out_shape=[gp, gp, gcp, gcp], name="ssm_prep")(
        a_re, a_im, log_dt, b_re_t, b_im_t)


def _ssm_kernel(u_ref, wb_ref, lam_re_ref, lam_im_ref, wc_ref, d_ref, wglu_ref, y_ref, st_ref, xr_ref, xi_ref,
                *, batch):
    n_st = SSM_STATES_ALL
    steps = u_ref.shape[0] // batch

    @pl.when(pl.program_id(0) == 0)
    def _():
        xr_ref[...] = jnp.zeros(xr_ref.shape, F32)
        xi_ref[...] = jnp.zeros(xi_ref.shape, F32)

    rows = u_ref.shape[0]
    halves = [slice(h * rows // 2, (h + 1) * rows // 2) for h in range(2)]
    for sl in halves:
        st_ref[sl, :] = _dot(u_ref[sl, :], wb_ref[...])

    def step(t, carry):
        xr, xi = carry
        r0 = pl.multiple_of(t * batch, batch)
        lr = lam_re_ref[...]
        li = lam_im_ref[...]
        nr = lr * xr - li * xi + st_ref[pl.ds(r0, batch), 0:n_st]
        ni = lr * xi + li * xr + st_ref[pl.ds(r0, batch), n_st:2 * n_st]
        st_ref[pl.ds(r0, batch), 0:n_st] = nr
        st_ref[pl.ds(r0, batch), n_st:2 * n_st] = ni
        return nr, ni

    xr, xi = lax.fori_loop(0, steps, step, (xr_ref[...], xi_ref[...]))
    xr_ref[...] = xr
    xi_ref[...] = xi

    ys = [_dot(st_ref[sl, 0:n_st].astype(BF16), wc_ref[0:n_st, :])
          + _dot(st_ref[sl, n_st:2 * n_st].astype(BF16), wc_ref[n_st:2 * n_st, :]) for sl in halves]
    for sl, y in zip(halves, ys):
        y = y + d_ref[...] * u_ref[sl, :].astype(F32)
        z = _dot(_gelu_tanh(y).astype(BF16), wglu_ref[...])
        y_ref[sl, :] = (z[:, 0:SSM_WIDTH] * _sigmoid(z[:, SSM_WIDTH:2 * SSM_WIDTH])).astype(BF16)


def _ssm(u_tb, wb, lam_re_b, lam_im_b, wc, d_skip, w_glu, batch, seq):
    rows = SSM_LC * batch
    full = lambda a: pl.BlockSpec(a.shape, lambda i: (0,) * a.ndim)
    return pl.pallas_call(
        functools.partial(_ssm_kernel, batch=batch),
        grid=(seq // SSM_LC,),
        in_specs=[pl.BlockSpec((rows, SSM_WIDTH), lambda i: (i, 0)),
                  full(wb), full(lam_re_b), full(lam_im_b), full(wc), full(d_skip), full(w_glu)],
        out_specs=pl.BlockSpec((rows, SSM_WIDTH), lambda i: (i, 0)),
        out_shape=jax.ShapeDtypeStruct((seq * batch, SSM_WIDTH), BF16),
        scratch_shapes=[pltpu.VMEM((rows, 2 * SSM_STATES_ALL), F32),
                        pltpu.VMEM((batch, SSM_STATES_ALL), F32),
                        pltpu.VMEM((batch, SSM_STATES_ALL), F32)],
        compiler_params=pltpu.CompilerParams(dimension_semantics=("arbitrary",), vmem_limit_bytes=VMEM_LIMIT),
        name="ssm",
    )(u_tb, wb, lam_re_b, lam_im_b, wc, d_skip, w_glu)


def _conv_kernel(u_ref, wdw_ref, bdw_ref, lng_ref, lnb_ref, wpw_ref, bpw_ref, y_ref, pad_ref):
    seq = u_ref.shape[0]
    pad_ref[0:CONV_PAD, :] = jnp.zeros((CONV_PAD, CONV_WIDTH), F32)
    pad_ref[CONV_PAD:CONV_PAD + seq, :] = u_ref[...]
    wdw = wdw_ref[...]

    def tile(i):
        t0 = pl.multiple_of(i * CONV_TC, CONV_TC)
        win_rows = CONV_TC + CONV_PAD
        win = pad_ref[pl.ds(t0, win_rows), :]
        acc = jnp.zeros((CONV_TC, CONV_WIDTH), F32)
        for b in range(SUBLANES):
            rolled = win if b == 0 else pltpu.roll(win, win_rows - b, 0)
            for a in range(win_rows // SUBLANES):
                k = a * SUBLANES + b - (CONV_PAD - (CONV_K - 1))
                if 0 <= k < CONV_K:
                    acc = acc + wdw[k:k + 1, :] * rolled[a * SUBLANES:a * SUBLANES + CONV_TC]
        y = acc + bdw_ref[...]
        mu = jnp.mean(y, axis=-1, keepdims=True)
        yc = y - mu
        var = jnp.mean(yc * yc, axis=-1, keepdims=True)
        yn = yc * lax.rsqrt(var + EPS) * lng_ref[...] + lnb_ref[...]
        act = yn * _sigmoid(yn)
        y_ref[pl.ds(t0, CONV_TC), :] = (_dot(act.astype(BF16), wpw_ref[...]) + bpw_ref[...]).astype(BF16)

    def tile_pair(i, carry):
        tile(2 * i)
        tile(2 * i + 1)
        return carry

    lax.fori_loop(0, seq // (2 * CONV_TC), tile_pair, 0)


def _conv(u, wdw, bdw, lng, lnb, wpw, bpw, batch, seq):
    full = lambda a: pl.BlockSpec(a.shape, lambda b: (0,) * a.ndim)
    return pl.pallas_call(
        _conv_kernel,
        grid=(batch,),
        in_specs=[pl.BlockSpec((seq, CONV_WIDTH), lambda b: (b, 0)),
                  full(wdw), full(bdw), full(lng), full(lnb), full(wpw), full(bpw)],
        out_specs=pl.BlockSpec((seq, CONV_WIDTH), lambda b: (b, 0)),
        out_shape=jax.ShapeDtypeStruct((batch * seq, CONV_WIDTH), BF16),
        scratch_shapes=[pltpu.VMEM((CONV_PAD + seq, CONV_WIDTH), F32)],
        compiler_params=pltpu.CompilerParams(dimension_semantics=("parallel",), vmem_limit_bytes=VMEM_LIMIT),
        name="conv",
    )(u, wdw, bdw, lng, lnb, wpw, bpw)


def _out_ffn_kernel(x_ref, ya_ref, ys_ref, yc_ref, ga_ref, gs_ref, gc_ref, wo_ref, gf_ref,
                    wg_ref, wu_ref, wd_ref, gfin_ref, o_ref, *, final_norm):
    na = _rms(ya_ref[...].astype(F32), ga_ref[...]).astype(BF16)
    ns = _rms(ys_ref[...].astype(F32), gs_ref[...]).astype(BF16)
    nc = _rms(yc_ref[...].astype(F32), gc_ref[...]).astype(BF16)
    o1, o2 = ATTN_WIDTH, ATTN_WIDTH + SSM_WIDTH
    x = x_ref[...] + _dot(na, wo_ref[0:o1, :]) + _dot(ns, wo_ref[o1:o2, :]) + _dot(nc, wo_ref[o2:, :])
    o_ref[...] = x
    h = _rms(x, gf_ref[...]).astype(BF16)
    gate = _dot(h, wg_ref[...])
    up = _dot(h, wu_ref[...])
    x = o_ref[...] + _dot((gate * _sigmoid(gate) * up).astype(BF16), wd_ref[...])
    if final_norm:
        x = _rms(x, gfin_ref[...])
    o_ref[...] = x


def _out_ffn(x2d, y_attn, y_ssm_tm, y_conv, ga, gs, gc, wo, gf, wg, wu, wd, gfin, batch, seq, final_norm):
    tm = TM_PROJ
    tiles_per_seq = seq // tm
    tokens = batch * seq
    row = lambda n: pl.BlockSpec((tm, n), lambda i: (i, 0))
    resident = lambda a: pl.BlockSpec(a.shape, lambda i: (0,) * a.ndim)
    return pl.pallas_call(
        functools.partial(_out_ffn_kernel, final_norm=final_norm),
        grid=(tokens // tm,),
        in_specs=[row(D_MODEL), row(ATTN_WIDTH),
                  pl.BlockSpec((tm, SSM_WIDTH), lambda i: (i % tiles_per_seq, i // tiles_per_seq)),
                  row(CONV_WIDTH),
                  resident(ga), resident(gs), resident(gc), resident(wo), resident(gf),
                  resident(wg), resident(wu), resident(wd), resident(gfin)],
        out_specs=row(D_MODEL),
        out_shape=jax.ShapeDtypeStruct((tokens, D_MODEL), F32),
        compiler_params=pltpu.CompilerParams(dimension_semantics=("parallel",), vmem_limit_bytes=VMEM_LIMIT),
        name="out_ffn",
    )(x2d, y_attn, y_ssm_tm, y_conv, ga, gs, gc, wo, gf, wg, wu, wd, gfin)


def _rope_tables(pos):
    half = HEAD_DIM // 2
    freqs = ROPE_THETA ** (-jnp.arange(half, dtype=F32) / half)
    ang = pos.astype(F32)[:, None] * freqs[None, :]
    cos = jnp.tile(jnp.cos(ang), (1, LANES // half))
    sin = jnp.sin(ang)
    sin_signed = jnp.tile(jnp.concatenate([-sin, sin], axis=1), (1, LANES // HEAD_DIM))
    return cos, sin_signed


def _pack_w_in(w_in):
    sizes = (ATTN_WIDTH,) + (KV_WIDTH,) * 6 + (N_BRANCH * N_Q_HEADS, SSM_WIDTH, 2 * CONV_WIDTH)
    parts = jnp.split(w_in, np.cumsum(sizes)[:-1].tolist(), axis=1)
    gate = jnp.pad(parts[7], ((0, 0), (0, GATE_PAD - N_BRANCH * N_Q_HEADS)))
    return jnp.concatenate(parts[:7] + parts[8:] + [gate], axis=1).astype(BF16)


def _pack_cmp_w1(w1):
    w = w1.reshape(2, CMP_STRIDE, HEAD_DIM, CMP_HIDDEN)
    eye = jnp.eye(N_KV_HEADS, dtype=w1.dtype)
    big = jnp.einsum('rpdh,ab->padrbh', w, eye)
    return big.reshape(CMP_STRIDE * N_KV_HEADS * HEAD_DIM, 2 * N_KV_HEADS * CMP_HIDDEN).astype(BF16)


def _pe_operand(pe):
    lo = jnp.tile(pe[:CMP_STRIDE, None, :], (1, N_KV_HEADS, 1)).reshape(1, -1)
    hi = jnp.tile(pe[CMP_STRIDE:, None, :], (1, N_KV_HEADS, 1)).reshape(1, -1)
    return jnp.concatenate([jnp.tile(lo, (4, 1)), jnp.tile(hi, (4, 1))], axis=0).astype(BF16)


def _block_diag(blocks):
    g, a, b = blocks.shape
    eye = jnp.eye(g, dtype=blocks.dtype)
    return jnp.einsum('gab,gh->gahb', blocks, eye).reshape(g * a, g * b)


def _overlap_matrix(seq):
    n_cmp = seq // CMP_STRIDE
    n_sel = seq // SLC_BLOCK
    ci = np.arange(n_cmp)[None, :]
    sj = np.arange(n_sel)[:, None]
    ovl = (ci * CMP_STRIDE < (sj + 1) * SLC_BLOCK) & (ci * CMP_STRIDE + CMP_LEN > sj * SLC_BLOCK) & (ci < n_cmp - 1)
    return jnp.asarray(ovl.astype(np.float32), dtype=BF16)


def _gate_spread_matrix():
    e = np.zeros((N_BRANCH, N_KV_HEADS, GATE_PAD, GQA_GROUP * LANES), np.float32)
    for br in range(N_BRANCH):
        for g in range(N_KV_HEADS):
            for r in range(GQA_GROUP):
                e[br, g, br * N_Q_HEADS + g * GQA_GROUP + r, r * LANES:(r + 1) * LANES] = 1.0
    return jnp.asarray(e, dtype=BF16)


def _layer(x2d, p, batch, seq, tables, final_norm, norm_final):
    cos, sin, cos_c, sin_c, ovl, gexp = tables
    row = lambda v: v.reshape(1, -1).astype(F32)

    q, kc_raw, vc_raw, ksa, vsa, kwa, vwa, gates, u_ssm, u_conv = _in_proj(
        x2d, row(p['norm_mix']), _pack_w_in(p['w_in']), cos, sin, batch, seq)

    n_half = seq // CMP_STRIDE
    kh = kc_raw.reshape(batch, n_half, CMP_STRIDE * KV_WIDTH)
    vh = vc_raw.reshape(batch, n_half, CMP_STRIDE * KV_WIDTH)

    wk = _pack_cmp_w1(p['cmp_k_w1'])
    wv = _pack_cmp_w1(p['cmp_v_w1'])
    pek = _pe_operand(p['cmp_pe_k'])
    pev = _pe_operand(p['cmp_pe_v'])
    kca, vca = _compress(kh, vh, wk, wv, pek, pev, p['cmp_k_w2'].astype(BF16), p['cmp_v_w2'].astype(BF16),
                       cos_c, sin_c)

    y_attn = _attention(q, kca, vca, ksa, vsa, kwa, vwa, gates, ovl, gexp, batch, seq)

    lam_re, lam_im, bb_re, bb_im = _ssm_prep(
        p['ssm_a_re'], p['ssm_a_im'], p['ssm_log_dt'].reshape(-1, 1),
        p['ssm_b_re'].transpose(0, 2, 1), p['ssm_b_im'].transpose(0, 2, 1))
    wb = jnp.concatenate([_block_diag(bb_re), _block_diag(bb_im)], axis=1).astype(BF16)
    wc = jnp.concatenate([_block_diag(p['ssm_c_re'].transpose(0, 2, 1)),
                          -_block_diag(p['ssm_c_im'].transpose(0, 2, 1))], axis=0).astype(BF16)
    lam_re_b = jnp.broadcast_to(lam_re.reshape(1, -1), (batch, SSM_STATES_ALL))
    lam_im_b = jnp.broadcast_to(lam_im.reshape(1, -1), (batch, SSM_STATES_ALL))
    y_ssm = _ssm(u_ssm.reshape(seq * batch, SSM_WIDTH), wb, lam_re_b, lam_im_b, wc, row(p['ssm_d']),
                 p['ssm_w_glu'].astype(BF16), batch, seq)
    y_ssm_tm = y_ssm.reshape(seq, batch * SSM_WIDTH)

    wdw = jnp.pad(p['conv_w_dw'], ((0, CONV_PAD - CONV_K), (0, 0)))
    y_conv = _conv(u_conv, wdw, row(p['conv_b_dw']), row(p['conv_ln_g']), row(p['conv_ln_b']),
                   p['conv_w_pw'].astype(BF16), row(p['conv_b_pw']), batch, seq)

    return _out_ffn(x2d, y_attn, y_ssm_tm, y_conv, row(p['norm_out_attn']), row(p['norm_out_ssm']),
                    row(p['norm_out_conv']), p['w_out'].astype(BF16), row(p['norm_ffn']),
                    p['w_gate'].astype(BF16), p['w_up'].astype(BF16), p['w_down'].astype(BF16),
                    row(norm_final), batch, seq, final_norm)


_LAYER_KEYS = ('norm_mix', 'w_in', 'cmp_pe_k', 'cmp_k_w1', 'cmp_k_w2', 'cmp_pe_v', 'cmp_v_w1', 'cmp_v_w2',
               'ssm_a_re', 'ssm_a_im', 'ssm_log_dt', 'ssm_b_re', 'ssm_b_im', 'ssm_c_re', 'ssm_c_im', 'ssm_d',
               'ssm_w_glu', 'conv_w_dw', 'conv_b_dw', 'conv_ln_g', 'conv_ln_b', 'conv_w_pw', 'conv_b_pw',
               'norm_out_attn', 'norm_out_ssm', 'norm_out_conv', 'w_out', 'norm_ffn', 'w_gate', 'w_up', 'w_down')


def kernel(x, norm_mix, w_in, cmp_pe_k, cmp_k_w1, cmp_k_w2, cmp_pe_v, cmp_v_w1, cmp_v_w2, ssm_a_re, ssm_a_im, ssm_log_dt, ssm_b_re, ssm_b_im, ssm_c_re, ssm_c_im, ssm_d, ssm_w_glu, conv_w_dw, conv_b_dw, conv_ln_g, conv_ln_b, conv_w_pw, conv_b_pw, norm_out_attn, norm_out_ssm, norm_out_conv, w_out, norm_ffn, w_gate, w_up, w_down, norm_final):
    stacked = dict(zip(_LAYER_KEYS, (norm_mix, w_in, cmp_pe_k, cmp_k_w1, cmp_k_w2, cmp_pe_v, cmp_v_w1, cmp_v_w2,
                                     ssm_a_re, ssm_a_im, ssm_log_dt, ssm_b_re, ssm_b_im, ssm_c_re, ssm_c_im, ssm_d,
                                     ssm_w_glu, conv_w_dw, conv_b_dw, conv_ln_g, conv_ln_b, conv_w_pw, conv_b_pw,
                                     norm_out_attn, norm_out_ssm, norm_out_conv, w_out, norm_ffn, w_gate, w_up,
                                     w_down)))
    batch, seq, _ = x.shape
    depth = norm_mix.shape[0]
    cos, sin = _rope_tables(jnp.arange(seq))
    cos_c, sin_c = _rope_tables(jnp.arange(seq // CMP_STRIDE) * CMP_STRIDE + CMP_LEN - 1)
    tables = (cos, sin, cos_c, sin_c, _overlap_matrix(seq), _gate_spread_matrix())
    x2d = x.reshape(batch * seq, D_MODEL)
    for l in range(depth):
        p = {k: v[l] for k, v in stacked.items()}
        x2d = _layer(x2d, p, batch, seq, tables, l == depth - 1, norm_final)
    return x2d.reshape(batch, seq, D_MODEL)
```

```python
import functools
import math

import numpy as np
import jax
import jax.numpy as jnp
from jax import lax
from jax.experimental import pallas as pl
from jax.experimental.pallas import tpu as pltpu

F32 = jnp.float32
BF16 = jnp.bfloat16

D_MODEL = 1024
HEAD_DIM = 64
N_Q_HEADS = 8
N_KV_HEADS = 2
GQA_GROUP = N_Q_HEADS // N_KV_HEADS
ATTN_WIDTH = N_Q_HEADS * HEAD_DIM
KV_WIDTH = N_KV_HEADS * HEAD_DIM
N_BRANCH = 3
SSM_WIDTH = D_MODEL // 4
SSM_GROUP = 16
SSM_N_GROUPS = SSM_WIDTH // SSM_GROUP
SSM_STATE = 64
CONV_WIDTH = D_MODEL // 4
CONV_K = 31
CMP_LEN = 32
CMP_STRIDE = 16
CMP_HIDDEN = 256
SLC_BLOCK = 64
SLC_TOPN = 16
WINDOW = 512
ROPE_THETA = 10000.0
D_FF = ((8 * D_MODEL // 3 + 255) // 256) * 256
EPS = 1e-6
NEG_INF = -1e30
Q_SCALE = HEAD_DIM ** -0.5 * math.log2(math.e)
FORCE = 1e9

LANES = 128
SUBLANES = 8
VMEM_LIMIT = 56 * 1024 * 1024

GATE_PAD = LANES
PROJ_BLOCK = 512
C_Q = 0
C_KC = C_Q + ATTN_WIDTH
C_VC = C_KC + KV_WIDTH
C_KS = C_VC + KV_WIDTH
C_VS = C_KS + KV_WIDTH
C_KW = C_VS + KV_WIDTH
C_VW = C_KW + KV_WIDTH
C_SSM = C_VW + KV_WIDTH
C_CONV = C_SSM + SSM_WIDTH
C_GATE = C_CONV + 2 * CONV_WIDTH
IN_PACKED = C_GATE + GATE_PAD

TM_PROJ = 512
TQ = 256
CK = 512
WIN_PIECE = 256
RANK_GROUP = 16
SSM_CHUNK = 32
CONV_TC = 128
CONV_PAD = 32


def _gelu_tanh(x):
    return 0.5 * x * (1.0 + jnp.tanh(math.sqrt(2.0 / math.pi) * (x + 0.044715 * (x * x * x))))


def _sigmoid(x):
    return 1.0 / (1.0 + jnp.exp(-x))


def _rms(x, g):
    return x * lax.rsqrt(jnp.mean(x * x, axis=-1, keepdims=True) + EPS) * g


def _dot(a, b):
    return jnp.dot(a, b, preferred_element_type=F32)


def _dot_nt(a, b):
    return lax.dot_general(a, b, (((1,), (1,)), ((), ())), preferred_element_type=F32)


def _rope_rotate(v, cos, sin_signed):
    lane = lax.broadcasted_iota(jnp.int32, v.shape, 1)
    first_half = (lane % HEAD_DIM) < (HEAD_DIM // 2)
    rot = jnp.where(first_half, pltpu.roll(v, LANES - HEAD_DIM // 2, 1), pltpu.roll(v, HEAD_DIM // 2, 1))
    return v * cos + rot * sin_signed


def _in_proj_kernel(x_ref, g_ref, w_ref, cos_ref, sin_ref,
                    q_ref, kc_ref, vc_ref, ksa_ref, vsa_ref, kwa_ref, vwa_ref, gate_ref, ussm_ref, uconv_ref,
                    *, tiles_per_seq):
    tm = x_ref.shape[0]
    h = _rms(x_ref[...], g_ref[...]).astype(BF16)
    cos = cos_ref[...]
    sin = sin_ref[...]

    blocks = {}

    def proj(c0, n):
        b0 = (c0 // PROJ_BLOCK) * PROJ_BLOCK
        if b0 not in blocks:
            blocks[b0] = _dot(h, w_ref[:, b0:min(b0 + PROJ_BLOCK, IN_PACKED)])
        return blocks[b0][:, c0 - b0:c0 - b0 + n]

    lane = lax.broadcasted_iota(jnp.int32, (tm, LANES), 1)
    row = lax.broadcasted_iota(jnp.int32, (tm, LANES), 0)
    t = (pl.program_id(0) % tiles_per_seq) * tm + row
    low = lane < HEAD_DIM
    for j in range(ATTN_WIDTH // LANES):
        qj = _rope_rotate(proj(C_Q + j * LANES, LANES), cos, sin) * Q_SCALE
        q_ref[:, (2 * j) * LANES:(2 * j + 1) * LANES] = jnp.where(low, qj, 0.0).astype(BF16)
        q_ref[:, (2 * j + 1) * LANES:(2 * j + 2) * LANES] = jnp.where(low, pltpu.roll(qj, HEAD_DIM, 1), 0.0).astype(BF16)
    kc_ref[...] = proj(C_KC, KV_WIDTH).astype(BF16)
    vc_ref[...] = proj(C_VC, KV_WIDTH).astype(BF16)
    blk_onehot = jnp.where((t // SLC_BLOCK) == (lane - HEAD_DIM), 1.0, 0.0)
    ones_up = jnp.where(low, 0.0, 1.0)

    def split_groups(v, upper, ref):
        ref[0, 0] = jnp.where(low, v, upper).astype(BF16)
        ref[0, 1] = jnp.where(low, pltpu.roll(v, HEAD_DIM, 1), upper).astype(BF16)

    split_groups(_rope_rotate(proj(C_KS, KV_WIDTH), cos, sin), blk_onehot, ksa_ref)
    split_groups(proj(C_VS, KV_WIDTH), ones_up, vsa_ref)
    split_groups(_rope_rotate(proj(C_KW, KV_WIDTH), cos, sin), 0.0, kwa_ref)
    split_groups(proj(C_VW, KV_WIDTH), ones_up, vwa_ref)

    gate_ref[...] = _sigmoid(proj(C_GATE, GATE_PAD))
    ussm_ref[...] = proj(C_SSM, SSM_WIDTH).astype(BF16)
    a = proj(C_CONV, CONV_WIDTH)
    g = proj(C_CONV + CONV_WIDTH, CONV_WIDTH)
    uconv_ref[...] = a * _sigmoid(g)


def _in_proj(x2d, g, w_packed, cos, sin, batch, seq):
    tm = TM_PROJ
    tiles_per_seq = seq // tm
    tokens = batch * seq
    row_blk = lambda n, dt: (pl.BlockSpec((tm, n), lambda i: (i, 0)), jax.ShapeDtypeStruct((tokens, n), dt))
    grp_blk = (pl.BlockSpec((1, N_KV_HEADS, tm, LANES), lambda i: (i // tiles_per_seq, 0, i % tiles_per_seq, 0)),
               jax.ShapeDtypeStruct((batch, N_KV_HEADS, seq, LANES), BF16))
    outs = [row_blk(N_Q_HEADS * LANES, BF16), row_blk(KV_WIDTH, BF16), row_blk(KV_WIDTH, BF16),
            grp_blk, grp_blk, grp_blk, grp_blk, row_blk(GATE_PAD, F32),
            row_blk(SSM_WIDTH, BF16),
            row_blk(CONV_WIDTH, F32)]
    return pl.pallas_call(
        functools.partial(_in_proj_kernel, tiles_per_seq=tiles_per_seq),
        grid=(tokens // tm,),
        in_specs=[pl.BlockSpec((tm, D_MODEL), lambda i: (i, 0)),
                  pl.BlockSpec((1, D_MODEL), lambda i: (0, 0)),
                  pl.BlockSpec((D_MODEL, IN_PACKED), lambda i: (0, 0)),
                  pl.BlockSpec((tm, LANES), lambda i: (i % tiles_per_seq, 0)),
                  pl.BlockSpec((tm, LANES), lambda i: (i % tiles_per_seq, 0))],
        out_specs=[o[0] for o in outs],
        out_shape=[o[1] for o in outs],
        compiler_params=pltpu.CompilerParams(dimension_semantics=("parallel",), vmem_limit_bytes=VMEM_LIMIT),
        name="in_proj",
    )(x2d, g, w_packed, cos, sin)


def _compress_kernel(kh_ref, vh_ref, wk_ref, wv_ref, pek_ref, pev_ref, w2k_ref, w2v_ref, cos_ref, sin_ref,
                     kca_ref, vca_ref):
    n_half = kh_ref.shape[1]
    hw = N_KV_HEADS * CMP_HIDDEN

    def compress(h_ref, w_ref, pe_ref, w2_ref):
        ab = _dot(h_ref[0], w_ref[...])
        pe = _dot(pe_ref[...], w_ref[...])
        bias = pe[0:1, 0:hw] + pe[4:5, hw:2 * hw]
        nxt = pltpu.roll(ab[:, hw:2 * hw], n_half - 1, 0)
        hid = _gelu_tanh(ab[:, 0:hw] + nxt + bias).astype(BF16)
        return jnp.concatenate([_dot(hid[:, j * CMP_HIDDEN:(j + 1) * CMP_HIDDEN], w2_ref[...])
                                for j in range(N_KV_HEADS)], axis=1)

    def split_groups(v, upper, ref):
        low = lax.broadcasted_iota(jnp.int32, v.shape, 1) < HEAD_DIM
        ref[0, 0] = jnp.where(low, v, upper).astype(BF16)
        ref[0, 1] = jnp.where(low, pltpu.roll(v, HEAD_DIM, 1), upper).astype(BF16)

    kc = compress(kh_ref, wk_ref, pek_ref, w2k_ref)
    split_groups(_rope_rotate(kc, cos_ref[...], sin_ref[...]), 0.0, kca_ref)
    split_groups(compress(vh_ref, wv_ref, pev_ref, w2v_ref), 1.0, vca_ref)


def _compress(kh, vh, wk, wv, pek, pev, w2k, w2v, cos_c, sin_c):
    batch, n_half, width = kh.shape
    full = lambda a: pl.BlockSpec(a.shape, lambda b: (0,) * a.ndim)
    per_b = pl.BlockSpec((1, n_half, width), lambda b: (b, 0, 0))
    out_blk = pl.BlockSpec((1, N_KV_HEADS, n_half, LANES), lambda b: (b, 0, 0, 0))
    out_shape = jax.ShapeDtypeStruct((batch, N_KV_HEADS, n_half, LANES), BF16)
    return pl.pallas_call(
        _compress_kernel,
        grid=(batch,),
        in_specs=[per_b, per_b, full(wk), full(wv), full(pek), full(pev), full(w2k), full(w2v),
                  full(cos_c), full(sin_c)],
        out_specs=[out_blk, out_blk],
        out_shape=[out_shape, out_shape],
        compiler_params=pltpu.CompilerParams(dimension_semantics=("parallel",), vmem_limit_bytes=VMEM_LIMIT),
        name="compress",
    )(kh, vh, wk, wv, pek, pev, w2k, w2v, cos_c, sin_c)


def _attention_kernel(q_ref, kca_ref, vca_ref, ksa_ref, vsa_ref, kwa_ref, vwa_ref, gate_ref, ovl_ref, gexp_ref,
                      y_ref,
                      qaug_ref, oc_ref, s0_ref, s1_ref, mx_ref, m_ref, acc_ref, val_ref, rank_ref):
    tq = q_ref.shape[0]
    rows = GQA_GROUP * tq
    n_cmp = kca_ref.shape[2]
    seq = ksa_ref.shape[2]
    n_sel = seq // SLC_BLOCK
    t0 = pl.program_id(1) * tq
    row_q = lax.broadcasted_iota(jnp.int32, (rows, 1), 0) % tq
    t_row = t0 + row_q
    t_rep = t0 + lax.broadcasted_iota(jnp.int32, (rows, LANES), 0) % tq
    lane_iota = lax.broadcasted_iota(jnp.int32, (rows, LANES), 1)
    diag_rep = t_rep - lane_iota
    cmp_rep = t_rep - lane_iota * CMP_STRIDE - (CMP_LEN - 1)
    last_blk = (t0 + tq - 1) // SLC_BLOCK

    def row_max_rep(x):
        return jnp.broadcast_to(jnp.max(x, axis=-1, keepdims=True), x.shape)

    def sums_on_all_lanes(a):
        return jnp.where(lane_iota < HEAD_DIM, pltpu.roll(a, HEAD_DIM, 1), a)

    def group_queries(g):
        heads = [q_ref[:, (g * GQA_GROUP + r) * LANES:(g * GQA_GROUP + r + 1) * LANES] for r in range(GQA_GROUP)]
        return jnp.concatenate(heads, axis=0)

    blk = lax.broadcasted_iota(jnp.int32, (n_sel, tq), 0)
    cur = (t0 + lax.broadcasted_iota(jnp.int32, (n_sel, tq), 1)) // SLC_BLOCK
    causal = blk <= cur

    def compressed_branch(n_slabs):
        n_used = n_slabs * LANES
        for g in range(N_KV_HEADS):
            sc = _dot_nt(group_queries(g), kca_ref[0, g, 0:n_used, :])
            slabs = []
            for j in range(n_slabs):
                slabs.append(jnp.where(cmp_rep >= j * LANES * CMP_STRIDE, sc[:, j * LANES:(j + 1) * LANES],
                                       NEG_INF))
            mc = slabs[0]
            for sj in slabs[1:]:
                mc = jnp.maximum(mc, sj)
            mc = row_max_rep(mc)
            pc = jnp.concatenate([jnp.exp2(sj - mc) for sj in slabs], axis=1)
            oc = _dot(pc.astype(BF16), vca_ref[0, g, 0:n_used, :])
            inv = jnp.where(t_rep >= CMP_LEN - 1, 1.0 / jnp.maximum(sums_on_all_lanes(oc), 1e-30), 0.0)
            oc_ref[g] = oc * inv
            pc = pc * jnp.concatenate([inv] * n_slabs, axis=1)

            pc_sum = pc[0:tq]
            for r in range(1, GQA_GROUP):
                pc_sum = pc_sum + pc[r * tq:(r + 1) * tq]
            p_hi = pc_sum.astype(BF16)
            rem = pc_sum - p_hi.astype(F32)
            p_mid = rem.astype(BF16)
            p_lo = (rem - p_mid.astype(F32)).astype(BF16)
            ovl = ovl_ref[:, 0:n_used]
            imp = _dot_nt(ovl, p_hi) + _dot_nt(ovl, p_mid) + _dot_nt(ovl, p_lo)
            val = jnp.where(blk == 0, FORCE, jnp.where(blk >= cur - 1, FORCE, imp))
            val_ref[:, g * tq:(g + 1) * tq] = jnp.where(causal, val, -FORCE)

    compressed_branch(n_cmp // LANES)

    val = val_ref[...]
    rank_ref[...] = jnp.zeros(rank_ref.shape, jnp.int32)
    grp_rows = lax.broadcasted_iota(jnp.int32, (RANK_GROUP, N_KV_HEADS * tq), 0)
    for ig in range(n_sel // RANK_GROUP):
        for jg in range(n_sel // RANK_GROUP):
            @pl.when(max(ig, jg) * RANK_GROUP <= last_blk)
            def _(ig=ig, jg=jg):
                vj = val[jg * RANK_GROUP:(jg + 1) * RANK_GROUP]
                cnt = jnp.zeros(vj.shape, jnp.int32)
                for ii in range(RANK_GROUP):
                    i = ig * RANK_GROUP + ii
                    vi = val[i:i + 1, :]
                    if jg > ig:
                        cnt = cnt + jnp.where(vi >= vj, 1, 0)
                    elif jg < ig:
                        cnt = cnt + jnp.where(vi > vj, 1, 0)
                    else:
                        tie = jnp.where(grp_rows > ii, 1, 0)
                        cnt = cnt + jnp.where(vi > vj, 1, jnp.where(vi == vj, tie, 0))
                sl = pl.ds(jg * RANK_GROUP, RANK_GROUP)
                rank_ref[sl, :] = rank_ref[sl, :] + cnt

    for g in range(N_KV_HEADS):
        rank = rank_ref[:, g * tq:(g + 1) * tq]
        bias_t = jnp.where(causal, jnp.where(rank < SLC_TOPN, 0.0, NEG_INF), NEG_INF)
        pieces = [jnp.zeros((HEAD_DIM, tq), F32), bias_t]
        if n_sel < LANES - HEAD_DIM:
            pieces.append(jnp.zeros((LANES - HEAD_DIM - n_sel, tq), F32))
        bias = jnp.concatenate(pieces, axis=0).T.astype(BF16)
        qaug_ref[g] = group_queries(g) + jnp.concatenate([bias] * GQA_GROUP, axis=0)

    n_chunks = (t0 + tq + CK - 1) // CK
    last = n_chunks - 1
    mx_ref[...] = jnp.full(mx_ref.shape, NEG_INF, F32)
    acc_ref[...] = jnp.zeros(acc_ref.shape, F32)
    s_refs = (s0_ref, s1_ref)

    def chunk_keys(ref, g, c, width=CK):
        return ref[0, g, pl.ds(pl.multiple_of(c * CK, CK), width), :]

    def score_chunk(g, c, diagonal, width=CK):
        s = _dot_nt(qaug_ref[g], chunk_keys(ksa_ref, g, c, width))
        mx = mx_ref[g]
        slabs = []
        for j in range(width // LANES):
            sj = s[:, j * LANES:(j + 1) * LANES]
            if diagonal:
                sj = jnp.where(diag_rep >= c * CK + j * LANES, sj, NEG_INF)
            mx = jnp.maximum(mx, sj)
            slabs.append(sj)
        s_refs[g][c, :, 0:width] = jnp.concatenate(slabs, axis=1)
        mx_ref[g] = mx

    def value_chunk(g, c, width=CK):
        s = s_refs[g][c, :, 0:width]
        m = m_ref[g]
        p = jnp.concatenate([jnp.exp2(s[:, j * LANES:(j + 1) * LANES] - m) for j in range(width // LANES)], axis=1)
        acc_ref[g] = acc_ref[g] + _dot(p.astype(BF16), chunk_keys(vsa_ref, g, c, width))

    last_is_half = t0 + tq - last * CK <= CK // 2

    def sweep(body, final):
        def pair(i, carry):
            body(2 * i)
            body(2 * i + 1)
            return carry

        def single(c, carry):
            body(c)
            return carry
        n_pairs = last // 2
        lax.fori_loop(0, n_pairs, pair, 0)
        lax.fori_loop(2 * n_pairs, last, single, 0)
        pl.when(last_is_half)(functools.partial(final, CK // 2))
        pl.when(jnp.logical_not(last_is_half))(functools.partial(final, CK))

    w0 = pl.multiple_of(jnp.maximum(t0 - WINDOW, 0), tq)
    n_pieces = (WINDOW + tq) // WIN_PIECE

    def window_branch(g, stage_ref):
        mw = None
        for j in range(n_pieces):
            sw = _dot_nt(qaug_ref[g], kwa_ref[0, g, pl.ds(w0 + j * WIN_PIECE, WIN_PIECE), :])
            slabs = []
            for jj in range(WIN_PIECE // LANES):
                slab = j * (WIN_PIECE // LANES) + jj
                k_first = w0 + slab * LANES
                sj = jnp.where(diag_rep >= k_first, sw[:, jj * LANES:(jj + 1) * LANES], NEG_INF)
                if slab < tq // LANES:
                    sj = jnp.where(diag_rep < k_first + WINDOW, sj, NEG_INF)
                mw = sj if mw is None else jnp.maximum(mw, sj)
                slabs.append(sj)
            stage_ref[j, :, 0:WIN_PIECE] = jnp.concatenate(slabs, axis=1)
        mw = row_max_rep(mw)
        aw = None
        for j in range(n_pieces):
            sw = stage_ref[j, :, 0:WIN_PIECE]
            pw = jnp.concatenate([jnp.exp2(sw[:, jj * LANES:(jj + 1) * LANES] - mw)
                                  for jj in range(WIN_PIECE // LANES)], axis=1)
            part = _dot(pw.astype(BF16), vwa_ref[0, g, pl.ds(w0 + j * WIN_PIECE, WIN_PIECE), :])
            aw = part if aw is None else aw + part
        mx_ref[g] = aw

    sweep(lambda c: score_chunk(0, c, False), lambda width: score_chunk(0, last, True, width))
    m_ref[0] = row_max_rep(mx_ref[0])

    def both(c):
        value_chunk(0, c)
        score_chunk(1, c, False)

    def both_last(width):
        value_chunk(0, last, width)
        score_chunk(1, last, True, width)

    sweep(both, both_last)
    m_ref[1] = row_max_rep(mx_ref[1])
    sweep(lambda c: value_chunk(1, c), lambda width: value_chunk(1, last, width))
    for g in range(N_KV_HEADS):
        window_branch(g, s_refs[g])

    gates = gate_ref[...]
    g_hi = gates.astype(BF16)
    g_lo = (gates - g_hi.astype(F32)).astype(BF16)
    lane = lax.broadcasted_iota(jnp.int32, (tq, LANES), 1)
    for g in range(N_KV_HEADS):
        def gate_rep(branch):
            e = gexp_ref[branch, g]
            spread = _dot(g_hi, e) + _dot(g_lo, e)
            return jnp.concatenate([spread[:, r * LANES:(r + 1) * LANES] for r in range(GQA_GROUP)], axis=0)

        asel = acc_ref[g]
        aw = mx_ref[g]
        o = (gate_rep(0) * oc_ref[g]
             + gate_rep(1) * asel / jnp.maximum(pltpu.roll(asel, HEAD_DIM, 1), 1e-30)
             + gate_rep(2) * aw / jnp.maximum(pltpu.roll(aw, HEAD_DIM, 1), 1e-30))
        for pair in range(GQA_GROUP // 2):
            lo = o[(2 * pair) * tq:(2 * pair + 1) * tq]
            hi = o[(2 * pair + 1) * tq:(2 * pair + 2) * tq]
            col = g * GQA_GROUP * HEAD_DIM + pair * LANES
            y_ref[:, col:col + LANES] = jnp.where(lane < HEAD_DIM, lo, pltpu.roll(hi, HEAD_DIM, 1)).astype(BF16)


def _attention(q, kca, vca, ksa, vsa, kwa, vwa, gates, ovl, gexp, batch, seq):
    tq = TQ
    nq = seq // tq
    n_cmp = kca.shape[2]
    rows = GQA_GROUP * tq
    grp = pl.BlockSpec((1, N_KV_HEADS, seq, LANES), lambda b, i: (b, 0, 0, 0), pipeline_mode=pl.Buffered(1))
    cmp_grp = pl.BlockSpec((1, N_KV_HEADS, n_cmp, LANES), lambda b, i: (b, 0, 0, 0))
    grp_rows = lambda dt: pltpu.VMEM((N_KV_HEADS, rows, LANES), dt)
    return pl.pallas_call(
        _attention_kernel,
        grid=(batch, nq),
        in_specs=[pl.BlockSpec((tq, N_Q_HEADS * LANES), lambda b, i: (b * nq + i, 0)),
                  cmp_grp, cmp_grp, grp, grp, grp, grp,
                  pl.BlockSpec((tq, GATE_PAD), lambda b, i: (b * nq + i, 0)),
                  pl.BlockSpec(ovl.shape, lambda b, i: (0, 0)),
                  pl.BlockSpec(gexp.shape, lambda b, i: (0, 0, 0, 0))],
        out_specs=pl.BlockSpec((tq, ATTN_WIDTH), lambda b, i: (b * nq + i, 0)),
        out_shape=jax.ShapeDtypeStruct((batch * seq, ATTN_WIDTH), BF16),
        scratch_shapes=[grp_rows(BF16),
                        grp_rows(F32),
                        pltpu.VMEM((seq // CK, rows, CK), F32),
                        pltpu.VMEM((seq // CK, rows, CK), F32),
                        grp_rows(F32),
                        grp_rows(F32),
                        grp_rows(F32),
                        pltpu.VMEM((seq // SLC_BLOCK, N_KV_HEADS * tq), F32),
                        pltpu.VMEM((seq // SLC_BLOCK, N_KV_HEADS * tq), jnp.int32)],
        compiler_params=pltpu.CompilerParams(dimension_semantics=("parallel", "parallel"),
                                             vmem_limit_bytes=VMEM_LIMIT),
        name="attention",
    )(q, kca, vca, ksa, vsa, kwa, vwa, gates, ovl, gexp)


def _ssm_prep_kernel(a_re_ref, a_im_ref, log_dt_ref, b_re_ref, b_im_ref, c_re_ref, c_im_ref,
                     kmat_ref, pt_ref, q_ref, laml_ref):
    lc = SSM_CHUNK
    a_re = a_re_ref[0]
    a_im = a_im_ref[0]
    dt = jnp.exp(log_dt_ref[0])
    mag = jnp.exp(a_re * dt)
    lam_re = mag * jnp.cos(a_im * dt)
    lam_im = mag * jnp.sin(a_im * dt)
    den = a_re * a_re + a_im * a_im
    nr, ni = lam_re - 1.0, lam_im
    f_re = (nr * a_re + ni * a_im) / den
    f_im = (ni * a_re - nr * a_im) / den
    b_re = b_re_ref[0]
    b_im = b_im_ref[0]
    bb_re = f_re * b_re - f_im * b_im
    bb_im = f_re * b_im + f_im * b_re
    c_re = c_re_ref[0]
    c_im = c_im_ref[0]

    step = lax.broadcasted_iota(jnp.int32, (lc, SSM_STATE), 0).astype(F32)

    def lam_pow(n):
        m = jnp.exp(n * (a_re * dt))
        return m * jnp.cos(n * (a_im * dt)), m * jnp.sin(n * (a_im * dt))

    def times_c(pr, pi):
        re = pr[:, None, :] * c_re[None] - pi[:, None, :] * c_im[None]
        im = pr[:, None, :] * c_im[None] + pi[:, None, :] * c_re[None]
        return re.reshape(lc * SSM_GROUP, SSM_STATE), im.reshape(lc * SSM_GROUP, SSM_STATE)

    cl_re, cl_im = times_c(*lam_pow(step))
    kmat_ref[0] = _dot_nt(cl_re, bb_re) - _dot_nt(cl_im, bb_im)
    cl1_re, cl1_im = times_c(*lam_pow(step + 1.0))
    pt_ref[0] = jnp.concatenate([cl1_re, -cl1_im], axis=1)
    rr, ri = lam_pow((lc - 1.0) - step)
    q_re = (rr[:, None, :] * bb_re[None] - ri[:, None, :] * bb_im[None]).reshape(lc * SSM_GROUP, SSM_STATE)
    q_im = (rr[:, None, :] * bb_im[None] + ri[:, None, :] * bb_re[None]).reshape(lc * SSM_GROUP, SSM_STATE)
    q_ref[0] = jnp.concatenate([q_re, q_im], axis=1)
    lr, li = lam_pow(jnp.full((1, SSM_STATE), float(lc), F32))
    laml_ref[0] = jnp.concatenate([lr, li], axis=1)


def _ssm_prep(a_re, a_im, log_dt, b_re_t, b_im_t, c_re, c_im):
    g = a_re.shape[0]
    n = SSM_CHUNK * SSM_GROUP
    per_g = lambda a: pl.BlockSpec((1,) + a.shape[1:], lambda i: (i,) + (0,) * (a.ndim - 1))
    ins = (a_re.reshape(g, 1, SSM_STATE), a_im.reshape(g, 1, SSM_STATE), log_dt.reshape(g, 1, 1),
           b_re_t, b_im_t, c_re, c_im)
    outs = [jax.ShapeDtypeStruct((g, n, SSM_GROUP), F32), jax.ShapeDtypeStruct((g, n, 2 * SSM_STATE), F32),
            jax.ShapeDtypeStruct((g, n, 2 * SSM_STATE), F32), jax.ShapeDtypeStruct((g, 1, 2 * SSM_STATE), F32)]
    return pl.pallas_call(
        _ssm_prep_kernel, grid=(g,), in_specs=[per_g(a) for a in ins], out_specs=[per_g(o) for o in outs],
        out_shape=outs, compiler_params=pltpu.CompilerParams(dimension_semantics=("parallel",)),
        name="ssm_prep")(*ins)


def _ssm_kernel(u_ref, tt_ref, q_ref, pt_ref, laml_ref, d_ref, y_ref, inc_ref, xin_ref, *, batch):
    u = u_ref[0]
    n_chunks = u.shape[0] // batch
    inc_ref[...] = _dot(u, q_ref[0])

    lam = laml_ref[0]
    lane = lax.broadcasted_iota(jnp.int32, (batch, 2 * SSM_STATE), 1)
    swapped = pltpu.roll(jnp.broadcast_to(lam, (batch, 2 * SSM_STATE)), SSM_STATE, 1)
    mul_same = jnp.where(lane < SSM_STATE, lam, swapped)
    mul_swap = jnp.where(lane < SSM_STATE, -swapped, lam)

    def step(k, x):
        r0 = pl.multiple_of(k * batch, batch)
        xin_ref[pl.ds(r0, batch), :] = x
        return mul_same * x + mul_swap * pltpu.roll(x, SSM_STATE, 1) + inc_ref[pl.ds(r0, batch), :]

    lax.fori_loop(0, n_chunks, step, jnp.zeros((batch, 2 * SSM_STATE), F32))
    y = (_dot(u, tt_ref[0]) + _dot_nt(xin_ref[...].astype(BF16), pt_ref[0])
         + d_ref[0] * u.astype(F32))
    y_ref[0] = _gelu_tanh(y).astype(BF16)


def _ssm(u_g, tt, q, pt, laml, d_t, batch):
    g, rows, width = u_g.shape
    per_g = lambda a: pl.BlockSpec((1,) + a.shape[1:], lambda i: (i,) + (0,) * (a.ndim - 1))
    return pl.pallas_call(
        functools.partial(_ssm_kernel, batch=batch),
        grid=(g,),
        in_specs=[per_g(a) for a in (u_g, tt, q, pt, laml, d_t)],
        out_specs=per_g(u_g),
        out_shape=jax.ShapeDtypeStruct(u_g.shape, BF16),
        scratch_shapes=[pltpu.VMEM((rows, 2 * SSM_STATE), F32), pltpu.VMEM((rows, 2 * SSM_STATE), F32)],
        compiler_params=pltpu.CompilerParams(dimension_semantics=("parallel",), vmem_limit_bytes=VMEM_LIMIT),
        name="ssm",
    )(u_g, tt, q, pt, laml, d_t)


def _toeplitz(kmat):
    g = kmat.shape[0]
    lc = SSM_CHUNK
    k4 = kmat.reshape(g, lc, SSM_GROUP, SSM_GROUP)
    t = jnp.arange(lc)[None, :]
    s = jnp.arange(lc)[:, None]
    blocks = k4[:, jnp.clip(t - s, 0, lc - 1)]
    blocks = jnp.where((t >= s)[None, :, :, None, None], blocks, 0.0)
    return blocks.transpose(0, 1, 4, 2, 3).reshape(g, lc * SSM_GROUP, lc * SSM_GROUP).astype(BF16)


def _ssm_branch(u_ssm, p, batch, seq):
    lc = SSM_CHUNK
    n_chunks = seq // lc
    kmat, pt, q, laml = _ssm_prep(p['ssm_a_re'], p['ssm_a_im'], p['ssm_log_dt'],
                                  p['ssm_b_re'].transpose(0, 2, 1), p['ssm_b_im'].transpose(0, 2, 1),
                                  p['ssm_c_re'], p['ssm_c_im'])
    u_g = u_ssm.reshape(batch, n_chunks, lc, SSM_N_GROUPS, SSM_GROUP).transpose(3, 1, 0, 2, 4)
    u_g = u_g.reshape(SSM_N_GROUPS, n_chunks * batch, lc * SSM_GROUP)
    d_t = jnp.tile(p['ssm_d'].reshape(SSM_N_GROUPS, 1, SSM_GROUP), (1, lc, 1)).reshape(SSM_N_GROUPS, 1, lc * SSM_GROUP)
    y_g = _ssm(u_g, _toeplitz(kmat), q.astype(BF16), pt.astype(BF16), laml, d_t.astype(F32), batch)
    y = y_g.reshape(SSM_N_GROUPS, n_chunks, batch, lc, SSM_GROUP).transpose(2, 1, 3, 0, 4)
    return y.reshape(batch * seq, SSM_WIDTH)


def _conv_kernel(u_ref, wdw_ref, bdw_ref, lng_ref, lnb_ref, wpw_ref, bpw_ref, y_ref, pad_ref):
    seq = u_ref.shape[0]
    pad_ref[0:CONV_PAD, :] = jnp.zeros((CONV_PAD, CONV_WIDTH), F32)
    pad_ref[CONV_PAD:CONV_PAD + seq, :] = u_ref[...]
    wdw = wdw_ref[...]

    def tile(i):
        t0 = pl.multiple_of(i * CONV_TC, CONV_TC)
        win_rows = CONV_TC + CONV_PAD
        win = pad_ref[pl.ds(t0, win_rows), :]
        acc = jnp.zeros((CONV_TC, CONV_WIDTH), F32)
        for b in range(SUBLANES):
            rolled = win if b == 0 else pltpu.roll(win, win_rows - b, 0)
            for a in range(win_rows // SUBLANES):
                k = a * SUBLANES + b - (CONV_PAD - (CONV_K - 1))
                if 0 <= k < CONV_K:
                    acc = acc + wdw[k:k + 1, :] * rolled[a * SUBLANES:a * SUBLANES + CONV_TC]
        y = acc + bdw_ref[...]
        mu = jnp.mean(y, axis=-1, keepdims=True)
        yc = y - mu
        var = jnp.mean(yc * yc, axis=-1, keepdims=True)
        yn = yc * lax.rsqrt(var + EPS) * lng_ref[...] + lnb_ref[...]
        act = yn * _sigmoid(yn)
        y_ref[pl.ds(t0, CONV_TC), :] = (_dot(act.astype(BF16), wpw_ref[...]) + bpw_ref[...]).astype(BF16)

    def tile_pair(i, carry):
        tile(2 * i)
        tile(2 * i + 1)
        return carry

    lax.fori_loop(0, seq // (2 * CONV_TC), tile_pair, 0)


def _conv(u, wdw, bdw, lng, lnb, wpw, bpw, batch, seq):
    full = lambda a: pl.BlockSpec(a.shape, lambda b: (0,) * a.ndim)
    return pl.pallas_call(
        _conv_kernel,
        grid=(batch,),
        in_specs=[pl.BlockSpec((seq, CONV_WIDTH), lambda b: (b, 0)),
                  full(wdw), full(bdw), full(lng), full(lnb), full(wpw), full(bpw)],
        out_specs=pl.BlockSpec((seq, CONV_WIDTH), lambda b: (b, 0)),
        out_shape=jax.ShapeDtypeStruct((batch * seq, CONV_WIDTH), BF16),
        scratch_shapes=[pltpu.VMEM((CONV_PAD + seq, CONV_WIDTH), F32)],
        compiler_params=pltpu.CompilerParams(dimension_semantics=("parallel",), vmem_limit_bytes=VMEM_LIMIT),
        name="conv",
    )(u, wdw, bdw, lng, lnb, wpw, bpw)


def _out_ffn_kernel(x_ref, ya_ref, ys_ref, yc_ref, wglu_ref, ga_ref, gs_ref, gc_ref, wo_ref, gf_ref,
                    wg_ref, wu_ref, wd_ref, gfin_ref, o_ref, *, final_norm):
    na = _rms(ya_ref[...].astype(F32), ga_ref[...]).astype(BF16)
    z = _dot(ys_ref[...], wglu_ref[...])
    ns = _rms(z[:, 0:SSM_WIDTH] * _sigmoid(z[:, SSM_WIDTH:2 * SSM_WIDTH]), gs_ref[...]).astype(BF16)
    nc = _rms(yc_ref[...].astype(F32), gc_ref[...]).astype(BF16)
    o1, o2 = ATTN_WIDTH, ATTN_WIDTH + SSM_WIDTH
    x = x_ref[...] + _dot(na, wo_ref[0:o1, :]) + _dot(ns, wo_ref[o1:o2, :]) + _dot(nc, wo_ref[o2:, :])
    o_ref[...] = x
    h = _rms(x, gf_ref[...]).astype(BF16)
    gate = _dot(h, wg_ref[...])
    up = _dot(h, wu_ref[...])
    x = o_ref[...] + _dot((gate * _sigmoid(gate) * up).astype(BF16), wd_ref[...])
    if final_norm:
        x = _rms(x, gfin_ref[...])
    o_ref[...] = x


def _out_ffn(x2d, y_attn, y_ssm, y_conv, wglu, ga, gs, gc, wo, gf, wg, wu, wd, gfin, batch, seq, final_norm):
    tm = TM_PROJ
    tokens = batch * seq
    row = lambda n: pl.BlockSpec((tm, n), lambda i: (i, 0))
    resident = lambda a: pl.BlockSpec(a.shape, lambda i: (0,) * a.ndim)
    return pl.pallas_call(
        functools.partial(_out_ffn_kernel, final_norm=final_norm),
        grid=(tokens // tm,),
        in_specs=[row(D_MODEL), row(ATTN_WIDTH),
                  row(SSM_WIDTH), row(CONV_WIDTH),
                  resident(wglu), resident(ga), resident(gs), resident(gc), resident(wo), resident(gf),
                  resident(wg), resident(wu), resident(wd), resident(gfin)],
        out_specs=row(D_MODEL),
        out_shape=jax.ShapeDtypeStruct((tokens, D_MODEL), F32),
        compiler_params=pltpu.CompilerParams(dimension_semantics=("parallel",), vmem_limit_bytes=VMEM_LIMIT),
        name="out_ffn",
    )(x2d, y_attn, y_ssm, y_conv, wglu, ga, gs, gc, wo, gf, wg, wu, wd, gfin)


def _rope_tables(pos):
    half = HEAD_DIM // 2
    freqs = ROPE_THETA ** (-jnp.arange(half, dtype=F32) / half)
    ang = pos.astype(F32)[:, None] * freqs[None, :]
    cos = jnp.tile(jnp.cos(ang), (1, LANES // half))
    sin = jnp.sin(ang)
    sin_signed = jnp.tile(jnp.concatenate([-sin, sin], axis=1), (1, LANES // HEAD_DIM))
    return cos, sin_signed


def _pack_w_in(w_in):
    sizes = (ATTN_WIDTH,) + (KV_WIDTH,) * 6 + (N_BRANCH * N_Q_HEADS, SSM_WIDTH, 2 * CONV_WIDTH)
    parts = jnp.split(w_in, np.cumsum(sizes)[:-1].tolist(), axis=1)
    gate = jnp.pad(parts[7], ((0, 0), (0, GATE_PAD - N_BRANCH * N_Q_HEADS)))
    return jnp.concatenate(parts[:7] + parts[8:] + [gate], axis=1).astype(BF16)


def _pack_cmp_w1(w1):
    w = w1.reshape(2, CMP_STRIDE, HEAD_DIM, CMP_HIDDEN)
    eye = jnp.eye(N_KV_HEADS, dtype=w1.dtype)
    big = jnp.einsum('rpdh,ab->padrbh', w, eye)
    return big.reshape(CMP_STRIDE * N_KV_HEADS * HEAD_DIM, 2 * N_KV_HEADS * CMP_HIDDEN).astype(BF16)


def _pe_operand(pe):
    lo = jnp.tile(pe[:CMP_STRIDE, None, :], (1, N_KV_HEADS, 1)).reshape(1, -1)
    hi = jnp.tile(pe[CMP_STRIDE:, None, :], (1, N_KV_HEADS, 1)).reshape(1, -1)
    return jnp.concatenate([jnp.tile(lo, (4, 1)), jnp.tile(hi, (4, 1))], axis=0).astype(BF16)


def _overlap_matrix(seq):
    n_cmp = seq // CMP_STRIDE
    n_sel = seq // SLC_BLOCK
    ci = np.arange(n_cmp)[None, :]
    sj = np.arange(n_sel)[:, None]
    ovl = (ci * CMP_STRIDE < (sj + 1) * SLC_BLOCK) & (ci * CMP_STRIDE + CMP_LEN > sj * SLC_BLOCK) & (ci < n_cmp - 1)
    return jnp.asarray(ovl.astype(np.float32), dtype=BF16)


def _gate_spread_matrix():
    e = np.zeros((N_BRANCH, N_KV_HEADS, GATE_PAD, GQA_GROUP * LANES), np.float32)
    for br in range(N_BRANCH):
        for g in range(N_KV_HEADS):
            for r in range(GQA_GROUP):
                e[br, g, br * N_Q_HEADS + g * GQA_GROUP + r, r * LANES:(r + 1) * LANES] = 1.0
    return jnp.asarray(e, dtype=BF16)


def _layer(x2d, p, batch, seq, tables, final_norm, norm_final):
    cos, sin, cos_c, sin_c, ovl, gexp = tables
    row = lambda v: v.reshape(1, -1).astype(F32)

    q, kc_raw, vc_raw, ksa, vsa, kwa, vwa, gates, u_ssm, u_conv = _in_proj(
        x2d, row(p['norm_mix']), _pack_w_in(p['w_in']), cos, sin, batch, seq)

    n_half = seq // CMP_STRIDE
    kh = kc_raw.reshape(batch, n_half, CMP_STRIDE * KV_WIDTH)
    vh = vc_raw.reshape(batch, n_half, CMP_STRIDE * KV_WIDTH)

    wk = _pack_cmp_w1(p['cmp_k_w1'])
    wv = _pack_cmp_w1(p['cmp_v_w1'])
    pek = _pe_operand(p['cmp_pe_k'])
    pev = _pe_operand(p['cmp_pe_v'])
    kca, vca = _compress(kh, vh, wk, wv, pek, pev, p['cmp_k_w2'].astype(BF16), p['cmp_v_w2'].astype(BF16),
                       cos_c, sin_c)

    y_attn = _attention(q, kca, vca, ksa, vsa, kwa, vwa, gates, ovl, gexp, batch, seq)

    y_ssm = _ssm_branch(u_ssm, p, batch, seq)

    wdw = jnp.pad(p['conv_w_dw'], ((0, CONV_PAD - CONV_K), (0, 0)))
    y_conv = _conv(u_conv, wdw, row(p['conv_b_dw']), row(p['conv_ln_g']), row(p['conv_ln_b']),
                   p['conv_w_pw'].astype(BF16), row(p['conv_b_pw']), batch, seq)

    return _out_ffn(x2d, y_attn, y_ssm, y_conv, p['ssm_w_glu'].astype(BF16), row(p['norm_out_attn']), row(p['norm_out_ssm']),
                    row(p['norm_out_conv']), p['w_out'].astype(BF16), row(p['norm_ffn']),
                    p['w_gate'].astype(BF16), p['w_up'].astype(BF16), p['w_down'].astype(BF16),
                    row(norm_final), batch, seq, final_norm)


_LAYER_KEYS = ('norm_mix', 'w_in', 'cmp_pe_k', 'cmp_k_w1', 'cmp_k_w2', 'cmp_pe_v', 'cmp_v_w1', 'cmp_v_w2',
               'ssm_a_re', 'ssm_a_im', 'ssm_log_dt', 'ssm_b_re', 'ssm_b_im', 'ssm_c_re', 'ssm_c_im', 'ssm_d',
               'ssm_w_glu', 'conv_w_dw', 'conv_b_dw', 'conv_ln_g', 'conv_ln_b', 'conv_w_pw', 'conv_b_pw',
               'norm_out_attn', 'norm_out_ssm', 'norm_out_conv', 'w_out', 'norm_ffn', 'w_gate', 'w_up', 'w_down')


def kernel(x, norm_mix, w_in, cmp_pe_k, cmp_k_w1, cmp_k_w2, cmp_pe_v, cmp_v_w1, cmp_v_w2, ssm_a_re, ssm_a_im, ssm_log_dt, ssm_b_re, ssm_b_im, ssm_c_re, ssm_c_im, ssm_d, ssm_w_glu, conv_w_dw, conv_b_dw, conv_ln_g, conv_ln_b, conv_w_pw, conv_b_pw, norm_out_attn, norm_out_ssm, norm_out_conv, w_out, norm_ffn, w_gate, w_up, w_down, norm_final):
    stacked = dict(zip(_LAYER_KEYS, (norm_mix, w_in, cmp_pe_k, cmp_k_w1, cmp_k_w2, cmp_pe_v, cmp_v_w1, cmp_v_w2,
                                     ssm_a_re, ssm_a_im, ssm_log_dt, ssm_b_re, ssm_b_im, ssm_c_re, ssm_c_im, ssm_d,
                                     ssm_w_glu, conv_w_dw, conv_b_dw, conv_ln_g, conv_ln_b, conv_w_pw, conv_b_pw,
                                     norm_out_attn, norm_out_ssm, norm_out_conv, w_out, norm_ffn, w_gate, w_up,
                                     w_down)))
    batch, seq, _ = x.shape
    depth = norm_mix.shape[0]
    cos, sin = _rope_tables(jnp.arange(seq))
    cos_c, sin_c = _rope_tables(jnp.arange(seq // CMP_STRIDE) * CMP_STRIDE + CMP_LEN - 1)
    tables = (cos, sin, cos_c, sin_c, _overlap_matrix(seq), _gate_spread_matrix())
    x2d = x.reshape(batch * seq, D_MODEL)
    for l in range(depth):
        p = {k: v[l] for k, v in stacked.items()}
        x2d = _layer(x2d, p, batch, seq, tables, l == depth - 1, norm_final)
    return x2d.reshape(batch, seq, D_MODEL)
```

```python
import functools
import math

import numpy as np
import jax
import jax.numpy as jnp
from jax import lax
from jax.experimental import pallas as pl
from jax.experimental.pallas import tpu as pltpu

F32 = jnp.float32
BF16 = jnp.bfloat16

D_MODEL = 1024
HEAD_DIM = 64
N_Q_HEADS = 8
N_KV_HEADS = 2
GQA_GROUP = N_Q_HEADS // N_KV_HEADS
ATTN_WIDTH = N_Q_HEADS * HEAD_DIM
KV_WIDTH = N_KV_HEADS * HEAD_DIM
N_BRANCH = 3
SSM_WIDTH = D_MODEL // 4
SSM_GROUP = 16
SSM_N_GROUPS = SSM_WIDTH // SSM_GROUP
SSM_STATE = 64
SSM_STATES_ALL = SSM_N_GROUPS * SSM_STATE
CONV_WIDTH = D_MODEL // 4
CONV_K = 31
CMP_LEN = 32
CMP_STRIDE = 16
CMP_HIDDEN = 256
SLC_BLOCK = 64
SLC_TOPN = 16
WINDOW = 512
ROPE_THETA = 10000.0
D_FF = ((8 * D_MODEL // 3 + 255) // 256) * 256
EPS = 1e-6
NEG_INF = -1e30
Q_SCALE = HEAD_DIM ** -0.5 * math.log2(math.e)
FORCE = 1e9

LANES = 128
SUBLANES = 8
VMEM_LIMIT = 56 * 1024 * 1024

GATE_PAD = LANES
PROJ_BLOCK = 512
C_Q = 0
C_KC = C_Q + ATTN_WIDTH
C_VC = C_KC + KV_WIDTH
C_KS = C_VC + KV_WIDTH
C_VS = C_KS + KV_WIDTH
C_KW = C_VS + KV_WIDTH
C_VW = C_KW + KV_WIDTH
C_SSM = C_VW + KV_WIDTH
C_CONV = C_SSM + SSM_WIDTH
C_GATE = C_CONV + 2 * CONV_WIDTH
IN_PACKED = C_GATE + GATE_PAD

TM_PROJ = 512
TM_IN = 1024
TQ = 256
CK = 512
WIN_PIECE = 256
RANK_GROUP = 16
SSM_LC = 128
CONV_TC = 128
CONV_UNROLL = 4
CONV_PAD = 32


def _gelu_tanh(x):
    return 0.5 * x * (1.0 + jnp.tanh(math.sqrt(2.0 / math.pi) * (x + 0.044715 * (x * x * x))))


def _sigmoid(x):
    return 1.0 / (1.0 + jnp.exp(-x))


def _rms(x, g):
    return x * lax.rsqrt(jnp.mean(x * x, axis=-1, keepdims=True) + EPS) * g


def _dot(a, b):
    return jnp.dot(a, b, preferred_element_type=F32)


def _dot_nt(a, b):
    return lax.dot_general(a, b, (((1,), (1,)), ((), ())), preferred_element_type=F32)


def _rope_rotate(v, cos, sin_signed):
    lane = lax.broadcasted_iota(jnp.int32, v.shape, 1)
    first_half = (lane % HEAD_DIM) < (HEAD_DIM // 2)
    rot = jnp.where(first_half, pltpu.roll(v, LANES - HEAD_DIM // 2, 1), pltpu.roll(v, HEAD_DIM // 2, 1))
    return v * cos + rot * sin_signed


def _in_proj_kernel(x_ref, g_ref, w_ref, cos_ref, sin_ref,
                    q_ref, kc_ref, vc_ref, ksa_ref, vsa_ref, kwa_ref, vwa_ref, gate_ref, ussm_ref, uconv_ref,
                    *, tiles_per_seq):
    tm = x_ref.shape[0]
    h = _rms(x_ref[...], g_ref[...]).astype(BF16)
    cos = cos_ref[...]
    sin = sin_ref[...]

    blocks = {}

    def proj(c0, n):
        b0 = (c0 // PROJ_BLOCK) * PROJ_BLOCK
        if b0 not in blocks:
            blocks[b0] = _dot(h, w_ref[:, b0:min(b0 + PROJ_BLOCK, IN_PACKED)])
        return blocks[b0][:, c0 - b0:c0 - b0 + n]

    lane = lax.broadcasted_iota(jnp.int32, (tm, LANES), 1)
    row = lax.broadcasted_iota(jnp.int32, (tm, LANES), 0)
    t = (pl.program_id(0) % tiles_per_seq) * tm + row
    low = lane < HEAD_DIM
    for j in range(ATTN_WIDTH // LANES):
        qj = _rope_rotate(proj(C_Q + j * LANES, LANES), cos, sin) * Q_SCALE
        q_ref[:, (2 * j) * LANES:(2 * j + 1) * LANES] = jnp.where(low, qj, 0.0).astype(BF16)
        q_ref[:, (2 * j + 1) * LANES:(2 * j + 2) * LANES] = jnp.where(low, pltpu.roll(qj, HEAD_DIM, 1), 0.0).astype(BF16)
    kc_ref[...] = proj(C_KC, KV_WIDTH).astype(BF16)
    vc_ref[...] = proj(C_VC, KV_WIDTH).astype(BF16)
    blk_onehot = jnp.where((t // SLC_BLOCK) == (lane - HEAD_DIM), 1.0, 0.0)
    ones_up = jnp.where(low, 0.0, 1.0)

    def split_groups(v, upper, ref):
        ref[0, 0] = jnp.where(low, v, upper).astype(BF16)
        ref[0, 1] = jnp.where(low, pltpu.roll(v, HEAD_DIM, 1), upper).astype(BF16)

    split_groups(_rope_rotate(proj(C_KS, KV_WIDTH), cos, sin), blk_onehot, ksa_ref)
    split_groups(proj(C_VS, KV_WIDTH), ones_up, vsa_ref)
    split_groups(_rope_rotate(proj(C_KW, KV_WIDTH), cos, sin), 0.0, kwa_ref)
    split_groups(proj(C_VW, KV_WIDTH), ones_up, vwa_ref)

    gate_ref[...] = _sigmoid(proj(C_GATE, GATE_PAD))
    ussm_ref[...] = proj(C_SSM, SSM_WIDTH).astype(BF16)
    a = proj(C_CONV, CONV_WIDTH)
    g = proj(C_CONV + CONV_WIDTH, CONV_WIDTH)
    uconv_ref[...] = a * _sigmoid(g)


def _in_proj(x2d, g, w_packed, cos, sin, batch, seq):
    tm = TM_IN
    tiles_per_seq = seq // tm
    tokens = batch * seq
    row_blk = lambda n, dt: (pl.BlockSpec((tm, n), lambda i: (i, 0)), jax.ShapeDtypeStruct((tokens, n), dt))
    grp_blk = (pl.BlockSpec((1, N_KV_HEADS, tm, LANES), lambda i: (i // tiles_per_seq, 0, i % tiles_per_seq, 0)),
               jax.ShapeDtypeStruct((batch, N_KV_HEADS, seq, LANES), BF16))
    outs = [row_blk(N_Q_HEADS * LANES, BF16), row_blk(KV_WIDTH, BF16), row_blk(KV_WIDTH, BF16),
            grp_blk, grp_blk, grp_blk, grp_blk, row_blk(GATE_PAD, F32),
            (pl.BlockSpec((tm, SSM_WIDTH), lambda i: (i % tiles_per_seq, i // tiles_per_seq)),
             jax.ShapeDtypeStruct((seq, batch * SSM_WIDTH), BF16)),
            row_blk(CONV_WIDTH, F32)]
    return pl.pallas_call(
        functools.partial(_in_proj_kernel, tiles_per_seq=tiles_per_seq),
        grid=(tokens // tm,),
        in_specs=[pl.BlockSpec((tm, D_MODEL), lambda i: (i, 0)),
                  pl.BlockSpec((1, D_MODEL), lambda i: (0, 0)),
                  pl.BlockSpec((D_MODEL, IN_PACKED), lambda i: (0, 0)),
                  pl.BlockSpec((tm, LANES), lambda i: (i % tiles_per_seq, 0)),
                  pl.BlockSpec((tm, LANES), lambda i: (i % tiles_per_seq, 0))],
        out_specs=[o[0] for o in outs],
        out_shape=[o[1] for o in outs],
        compiler_params=pltpu.CompilerParams(dimension_semantics=("parallel",), vmem_limit_bytes=VMEM_LIMIT),
        name="in_proj",
    )(x2d, g, w_packed, cos, sin)


def _compress_kernel(kh_ref, vh_ref, wk_ref, wv_ref, pek_ref, pev_ref, w2k_ref, w2v_ref, cos_ref, sin_ref,
                     kca_ref, vca_ref):
    n_half = kh_ref.shape[1]
    hw = N_KV_HEADS * CMP_HIDDEN

    def compress(h_ref, w_ref, pe_ref, w2_ref):
        ab = _dot(h_ref[0], w_ref[...])
        pe = _dot(pe_ref[...], w_ref[...])
        bias = pe[0:1, 0:hw] + pe[4:5, hw:2 * hw]
        nxt = pltpu.roll(ab[:, hw:2 * hw], n_half - 1, 0)
        hid = _gelu_tanh(ab[:, 0:hw] + nxt + bias).astype(BF16)
        return jnp.concatenate([_dot(hid[:, j * CMP_HIDDEN:(j + 1) * CMP_HIDDEN], w2_ref[...])
                                for j in range(N_KV_HEADS)], axis=1)

    def split_groups(v, upper, ref):
        low = lax.broadcasted_iota(jnp.int32, v.shape, 1) < HEAD_DIM
        ref[0, 0] = jnp.where(low, v, upper).astype(BF16)
        ref[0, 1] = jnp.where(low, pltpu.roll(v, HEAD_DIM, 1), upper).astype(BF16)

    kc = compress(kh_ref, wk_ref, pek_ref, w2k_ref)
    split_groups(_rope_rotate(kc, cos_ref[...], sin_ref[...]), 0.0, kca_ref)
    split_groups(compress(vh_ref, wv_ref, pev_ref, w2v_ref), 1.0, vca_ref)


def _compress(kh, vh, wk, wv, pek, pev, w2k, w2v, cos_c, sin_c):
    batch, n_half, width = kh.shape
    full = lambda a: pl.BlockSpec(a.shape, lambda b: (0,) * a.ndim)
    per_b = pl.BlockSpec((1, n_half, width), lambda b: (b, 0, 0))
    out_blk = pl.BlockSpec((1, N_KV_HEADS, n_half, LANES), lambda b: (b, 0, 0, 0))
    out_shape = jax.ShapeDtypeStruct((batch, N_KV_HEADS, n_half, LANES), BF16)
    return pl.pallas_call(
        _compress_kernel,
        grid=(batch,),
        in_specs=[per_b, per_b, full(wk), full(wv), full(pek), full(pev), full(w2k), full(w2v),
                  full(cos_c), full(sin_c)],
        out_specs=[out_blk, out_blk],
        out_shape=[out_shape, out_shape],
        compiler_params=pltpu.CompilerParams(dimension_semantics=("parallel",), vmem_limit_bytes=VMEM_LIMIT),
        name="compress",
    )(kh, vh, wk, wv, pek, pev, w2k, w2v, cos_c, sin_c)


def _attention_kernel(q_ref, kca_ref, vca_ref, ksa_ref, vsa_ref, kwa_ref, vwa_ref, gate_ref, ovl_ref, gexp_ref,
                      y_ref,
                      qaug_ref, oc_ref, s0_ref, s1_ref, mx_ref, m_ref, acc_ref, val_ref, rank_ref):
    tq = q_ref.shape[0]
    rows = GQA_GROUP * tq
    n_cmp = kca_ref.shape[2]
    seq = ksa_ref.shape[2]
    n_sel = seq // SLC_BLOCK
    t0 = pl.program_id(1) * tq
    row_q = lax.broadcasted_iota(jnp.int32, (rows, 1), 0) % tq
    t_row = t0 + row_q
    t_rep = t0 + lax.broadcasted_iota(jnp.int32, (rows, LANES), 0) % tq
    lane_iota = lax.broadcasted_iota(jnp.int32, (rows, LANES), 1)
    diag_rep = t_rep - lane_iota
    cmp_rep = t_rep - lane_iota * CMP_STRIDE - (CMP_LEN - 1)
    last_blk = (t0 + tq - 1) // SLC_BLOCK

    def row_max_rep(x):
        return jnp.broadcast_to(jnp.max(x, axis=-1, keepdims=True), x.shape)

    def sums_on_all_lanes(a):
        return jnp.where(lane_iota < HEAD_DIM, pltpu.roll(a, HEAD_DIM, 1), a)

    def group_queries(g):
        heads = [q_ref[:, (g * GQA_GROUP + r) * LANES:(g * GQA_GROUP + r + 1) * LANES] for r in range(GQA_GROUP)]
        return jnp.concatenate(heads, axis=0)

    blk = lax.broadcasted_iota(jnp.int32, (n_sel, tq), 0)
    cur = (t0 + lax.broadcasted_iota(jnp.int32, (n_sel, tq), 1)) // SLC_BLOCK
    causal = blk <= cur

    def compressed_branch(n_slabs):
        n_used = n_slabs * LANES
        for g in range(N_KV_HEADS):
            sc = _dot_nt(group_queries(g), kca_ref[0, g, 0:n_used, :])
            slabs = []
            for j in range(n_slabs):
                slabs.append(jnp.where(cmp_rep >= j * LANES * CMP_STRIDE, sc[:, j * LANES:(j + 1) * LANES],
                                       NEG_INF))
            mc = slabs[0]
            for sj in slabs[1:]:
                mc = jnp.maximum(mc, sj)
            mc = row_max_rep(mc)
            pc = jnp.concatenate([jnp.exp2(sj - mc) for sj in slabs], axis=1)
            oc = _dot(pc.astype(BF16), vca_ref[0, g, 0:n_used, :])
            inv = jnp.where(t_rep >= CMP_LEN - 1, 1.0 / jnp.maximum(sums_on_all_lanes(oc), 1e-30), 0.0)
            oc_ref[g] = oc * inv
            pc = pc * jnp.concatenate([inv] * n_slabs, axis=1)

            pc_sum = pc[0:tq]
            for r in range(1, GQA_GROUP):
                pc_sum = pc_sum + pc[r * tq:(r + 1) * tq]
            p_hi = pc_sum.astype(BF16)
            rem = pc_sum - p_hi.astype(F32)
            p_mid = rem.astype(BF16)
            p_lo = (rem - p_mid.astype(F32)).astype(BF16)
            ovl = ovl_ref[:, 0:n_used]
            imp = _dot_nt(ovl, p_hi) + _dot_nt(ovl, p_mid) + _dot_nt(ovl, p_lo)
            val = jnp.where(blk == 0, FORCE, jnp.where(blk >= cur - 1, FORCE, imp))
            val_ref[:, g * tq:(g + 1) * tq] = jnp.where(causal, val, -FORCE)

    compressed_branch(n_cmp // LANES)

    val = val_ref[...]
    rank_ref[...] = jnp.zeros(rank_ref.shape, jnp.int32)
    grp_rows = lax.broadcasted_iota(jnp.int32, (RANK_GROUP, N_KV_HEADS * tq), 0)
    for ig in range(n_sel // RANK_GROUP):
        for jg in range(n_sel // RANK_GROUP):
            @pl.when(max(ig, jg) * RANK_GROUP <= last_blk)
            def _(ig=ig, jg=jg):
                vj = val[jg * RANK_GROUP:(jg + 1) * RANK_GROUP]
                cnt = jnp.zeros(vj.shape, jnp.int32)
                for ii in range(RANK_GROUP):
                    i = ig * RANK_GROUP + ii
                    vi = val[i:i + 1, :]
                    if jg > ig:
                        cnt = cnt + jnp.where(vi >= vj, 1, 0)
                    elif jg < ig:
                        cnt = cnt + jnp.where(vi > vj, 1, 0)
                    else:
                        tie = jnp.where(grp_rows > ii, 1, 0)
                        cnt = cnt + jnp.where(vi > vj, 1, jnp.where(vi == vj, tie, 0))
                sl = pl.ds(jg * RANK_GROUP, RANK_GROUP)
                rank_ref[sl, :] = rank_ref[sl, :] + cnt

    for g in range(N_KV_HEADS):
        rank = rank_ref[:, g * tq:(g + 1) * tq]
        bias_t = jnp.where(causal, jnp.where(rank < SLC_TOPN, 0.0, NEG_INF), NEG_INF)
        pieces = [jnp.zeros((HEAD_DIM, tq), F32), bias_t]
        if n_sel < LANES - HEAD_DIM:
            pieces.append(jnp.zeros((LANES - HEAD_DIM - n_sel, tq), F32))
        bias = jnp.concatenate(pieces, axis=0).T.astype(BF16)
        qaug_ref[g] = group_queries(g) + jnp.concatenate([bias] * GQA_GROUP, axis=0)

    n_chunks = (t0 + tq + CK - 1) // CK
    last = n_chunks - 1
    mx_ref[...] = jnp.full(mx_ref.shape, NEG_INF, F32)
    acc_ref[...] = jnp.zeros(acc_ref.shape, F32)
    s_refs = (s0_ref, s1_ref)

    def chunk_keys(ref, g, c, width=CK):
        return ref[0, g, pl.ds(pl.multiple_of(c * CK, CK), width), :]

    def score_chunk(g, c, diagonal, width=CK):
        s = _dot_nt(qaug_ref[g], chunk_keys(ksa_ref, g, c, width))
        mx = mx_ref[g]
        slabs = []
        for j in range(width // LANES):
            sj = s[:, j * LANES:(j + 1) * LANES]
            if diagonal:
                sj = jnp.where(diag_rep >= c * CK + j * LANES, sj, NEG_INF)
            mx = jnp.maximum(mx, sj)
            slabs.append(sj)
        s_refs[g][c, :, 0:width] = jnp.concatenate(slabs, axis=1)
        mx_ref[g] = mx

    def value_chunk(g, c, width=CK):
        s = s_refs[g][c, :, 0:width]
        m = m_ref[g]
        p = jnp.concatenate([jnp.exp2(s[:, j * LANES:(j + 1) * LANES] - m) for j in range(width // LANES)], axis=1)
        acc_ref[g] = acc_ref[g] + _dot(p.astype(BF16), chunk_keys(vsa_ref, g, c, width))

    last_is_half = t0 + tq - last * CK <= CK // 2

    def sweep(body, final):
        def pair(i, carry):
            body(2 * i)
            body(2 * i + 1)
            return carry

        def single(c, carry):
            body(c)
            return carry
        n_pairs = last // 2
        lax.fori_loop(0, n_pairs, pair, 0)
        lax.fori_loop(2 * n_pairs, last, single, 0)
        pl.when(last_is_half)(functools.partial(final, CK // 2))
        pl.when(jnp.logical_not(last_is_half))(functools.partial(final, CK))

    w0 = pl.multiple_of(jnp.maximum(t0 - WINDOW, 0), tq)
    n_pieces = (WINDOW + tq) // WIN_PIECE

    def window_branch(g, stage_ref):
        mw = None
        for j in range(n_pieces):
            sw = _dot_nt(qaug_ref[g], kwa_ref[0, g, pl.ds(w0 + j * WIN_PIECE, WIN_PIECE), :])
            slabs = []
            for jj in range(WIN_PIECE // LANES):
                slab = j * (WIN_PIECE // LANES) + jj
                k_first = w0 + slab * LANES
                sj = jnp.where(diag_rep >= k_first, sw[:, jj * LANES:(jj + 1) * LANES], NEG_INF)
                if slab < tq // LANES:
                    sj = jnp.where(diag_rep < k_first + WINDOW, sj, NEG_INF)
                mw = sj if mw is None else jnp.maximum(mw, sj)
                slabs.append(sj)
            stage_ref[j, :, 0:WIN_PIECE] = jnp.concatenate(slabs, axis=1)
        mw = row_max_rep(mw)
        aw = None
        for j in range(n_pieces):
            sw = stage_ref[j, :, 0:WIN_PIECE]
            pw = jnp.concatenate([jnp.exp2(sw[:, jj * LANES:(jj + 1) * LANES] - mw)
                                  for jj in range(WIN_PIECE // LANES)], axis=1)
            part = _dot(pw.astype(BF16), vwa_ref[0, g, pl.ds(w0 + j * WIN_PIECE, WIN_PIECE), :])
            aw = part if aw is None else aw + part
        mx_ref[g] = aw

    sweep(lambda c: score_chunk(0, c, False), lambda width: score_chunk(0, last, True, width))
    m_ref[0] = row_max_rep(mx_ref[0])

    def both(c):
        value_chunk(0, c)
        score_chunk(1, c, False)

    def both_last(width):
        value_chunk(0, last, width)
        score_chunk(1, last, True, width)

    sweep(both, both_last)
    m_ref[1] = row_max_rep(mx_ref[1])
    sweep(lambda c: value_chunk(1, c), lambda width: value_chunk(1, last, width))
    for g in range(N_KV_HEADS):
        window_branch(g, s_refs[g])

    gates = gate_ref[...]
    g_hi = gates.astype(BF16)
    g_lo = (gates - g_hi.astype(F32)).astype(BF16)
    lane = lax.broadcasted_iota(jnp.int32, (tq, LANES), 1)
    for g in range(N_KV_HEADS):
        def gate_rep(branch):
            e = gexp_ref[branch, g]
            spread = _dot(g_hi, e) + _dot(g_lo, e)
            return jnp.concatenate([spread[:, r * LANES:(r + 1) * LANES] for r in range(GQA_GROUP)], axis=0)

        asel = acc_ref[g]
        aw = mx_ref[g]
        o = (gate_rep(0) * oc_ref[g]
             + gate_rep(1) * asel / jnp.maximum(pltpu.roll(asel, HEAD_DIM, 1), 1e-30)
             + gate_rep(2) * aw / jnp.maximum(pltpu.roll(aw, HEAD_DIM, 1), 1e-30))
        for pair in range(GQA_GROUP // 2):
            lo = o[(2 * pair) * tq:(2 * pair + 1) * tq]
            hi = o[(2 * pair + 1) * tq:(2 * pair + 2) * tq]
            col = g * GQA_GROUP * HEAD_DIM + pair * LANES
            y_ref[:, col:col + LANES] = jnp.where(lane < HEAD_DIM, lo, pltpu.roll(hi, HEAD_DIM, 1)).astype(BF16)


def _attention(q, kca, vca, ksa, vsa, kwa, vwa, gates, ovl, gexp, batch, seq):
    tq = TQ
    nq = seq // tq
    n_cmp = kca.shape[2]
    rows = GQA_GROUP * tq
    grp = pl.BlockSpec((1, N_KV_HEADS, seq, LANES), lambda b, i: (b, 0, 0, 0), pipeline_mode=pl.Buffered(1))
    cmp_grp = pl.BlockSpec((1, N_KV_HEADS, n_cmp, LANES), lambda b, i: (b, 0, 0, 0))
    grp_rows = lambda dt: pltpu.VMEM((N_KV_HEADS, rows, LANES), dt)
    return pl.pallas_call(
        _attention_kernel,
        grid=(batch, nq),
        in_specs=[pl.BlockSpec((tq, N_Q_HEADS * LANES), lambda b, i: (b * nq + i, 0)),
                  cmp_grp, cmp_grp, grp, grp, grp, grp,
                  pl.BlockSpec((tq, GATE_PAD), lambda b, i: (b * nq + i, 0)),
                  pl.BlockSpec(ovl.shape, lambda b, i: (0, 0)),
                  pl.BlockSpec(gexp.shape, lambda b, i: (0, 0, 0, 0))],
        out_specs=pl.BlockSpec((tq, ATTN_WIDTH), lambda b, i: (b * nq + i, 0)),
        out_shape=jax.ShapeDtypeStruct((batch * seq, ATTN_WIDTH), BF16),
        scratch_shapes=[grp_rows(BF16),
                        grp_rows(F32),
                        pltpu.VMEM((seq // CK, rows, CK), F32),
                        pltpu.VMEM((seq // CK, rows, CK), F32),
                        grp_rows(F32),
                        grp_rows(F32),
                        grp_rows(F32),
                        pltpu.VMEM((seq // SLC_BLOCK, N_KV_HEADS * tq), F32),
                        pltpu.VMEM((seq // SLC_BLOCK, N_KV_HEADS * tq), jnp.int32)],
        compiler_params=pltpu.CompilerParams(dimension_semantics=("parallel", "parallel"),
                                             vmem_limit_bytes=VMEM_LIMIT),
        name="attention",
    )(q, kca, vca, ksa, vsa, kwa, vwa, gates, ovl, gexp)


def _ssm_prep_kernel(a_re_ref, a_im_ref, log_dt_ref, b_re_ref, b_im_ref,
                     lam_re_ref, lam_im_ref, bb_re_ref, bb_im_ref):
    a_re = a_re_ref[...]
    a_im = a_im_ref[...]
    dt = jnp.exp(log_dt_ref[...])
    mag = jnp.exp(a_re * dt)
    lam_re = mag * jnp.cos(a_im * dt)
    lam_im = mag * jnp.sin(a_im * dt)
    den = a_re * a_re + a_im * a_im
    nr, ni = lam_re - 1.0, lam_im
    f_re = (nr * a_re + ni * a_im) / den
    f_im = (ni * a_re - nr * a_im) / den
    lam_re_ref[...] = lam_re
    lam_im_ref[...] = lam_im
    b_re = b_re_ref[...]
    b_im = b_im_ref[...]
    bb_re_ref[...] = f_re[:, None, :] * b_re - f_im[:, None, :] * b_im
    bb_im_ref[...] = f_re[:, None, :] * b_im + f_im[:, None, :] * b_re


def _ssm_prep(a_re, a_im, log_dt, b_re_t, b_im_t):
    gp = jax.ShapeDtypeStruct(a_re.shape, F32)
    gcp = jax.ShapeDtypeStruct(b_re_t.shape, F32)
    return pl.pallas_call(_ssm_prep_kernel, out_shape=[gp, gp, gcp, gcp], name="ssm_prep")(
        a_re, a_im, log_dt, b_re_t, b_im_t)


def _ssm_kernel(u_ref, wb_ref, lam_re_ref, lam_im_ref, wc_ref, d_ref, wglu_ref, y_ref, st_ref, xr_ref, xi_ref,
                *, batch):
    n_st = SSM_STATES_ALL
    steps = u_ref.shape[0] // batch

    @pl.when(pl.program_id(0) == 0)
    def _():
        xr_ref[...] = jnp.zeros(xr_ref.shape, F32)
        xi_ref[...] = jnp.zeros(xi_ref.shape, F32)

    rows = u_ref.shape[0]
    halves = [slice(h * rows // 2, (h + 1) * rows // 2) for h in range(2)]
    for sl in halves:
        st_ref[sl, :] = _dot(u_ref[sl, :], wb_ref[...])

    def step(t, carry):
        xr, xi = carry
        r0 = pl.multiple_of(t * batch, batch)
        lr = lam_re_ref[...]
        li = lam_im_ref[...]
        nr = lr * xr - li * xi + st_ref[pl.ds(r0, batch), 0:n_st]
        ni = lr * xi + li * xr + st_ref[pl.ds(r0, batch), n_st:2 * n_st]
        st_ref[pl.ds(r0, batch), 0:n_st] = nr
        st_ref[pl.ds(r0, batch), n_st:2 * n_st] = ni
        return nr, ni

    xr, xi = lax.fori_loop(0, steps, step, (xr_ref[...], xi_ref[...]))
    xr_ref[...] = xr
    xi_ref[...] = xi

    ys = [_dot(st_ref[sl, 0:n_st].astype(BF16), wc_ref[0:n_st, :])
          + _dot(st_ref[sl, n_st:2 * n_st].astype(BF16), wc_ref[n_st:2 * n_st, :]) for sl in halves]
    for sl, y in zip(halves, ys):
        y = y + d_ref[...] * u_ref[sl, :].astype(F32)
        z = _dot(_gelu_tanh(y).astype(BF16), wglu_ref[...])
        y_ref[sl, :] = (z[:, 0:SSM_WIDTH] * _sigmoid(z[:, SSM_WIDTH:2 * SSM_WIDTH])).astype(BF16)


def _ssm(u_tb, wb, lam_re_b, lam_im_b, wc, d_skip, w_glu, batch, seq):
    rows = SSM_LC * batch
    full = lambda a: pl.BlockSpec(a.shape, lambda i: (0,) * a.ndim)
    return pl.pallas_call(
        functools.partial(_ssm_kernel, batch=batch),
        grid=(seq // SSM_LC,),
        in_specs=[pl.BlockSpec((rows, SSM_WIDTH), lambda i: (i, 0)),
                  full(wb), full(lam_re_b), full(lam_im_b), full(wc), full(d_skip), full(w_glu)],
        out_specs=pl.BlockSpec((rows, SSM_WIDTH), lambda i: (i, 0)),
        out_shape=jax.ShapeDtypeStruct((seq * batch, SSM_WIDTH), BF16),
        scratch_shapes=[pltpu.VMEM((rows, 2 * SSM_STATES_ALL), F32),
                        pltpu.VMEM((batch, SSM_STATES_ALL), F32),
                        pltpu.VMEM((batch, SSM_STATES_ALL), F32)],
        compiler_params=pltpu.CompilerParams(dimension_semantics=("arbitrary",), vmem_limit_bytes=VMEM_LIMIT),
        name="ssm",
    )(u_tb, wb, lam_re_b, lam_im_b, wc, d_skip, w_glu)


def _conv_kernel(u_ref, wdw_ref, bdw_ref, lng_ref, lnb_ref, wpw_ref, bpw_ref, y_ref, pad_ref):
    seq = u_ref.shape[0]
    pad_ref[0:CONV_PAD, :] = jnp.zeros((CONV_PAD, CONV_WIDTH), F32)
    pad_ref[CONV_PAD:CONV_PAD + seq, :] = u_ref[...]
    wdw = wdw_ref[...]

    def tile(i):
        t0 = pl.multiple_of(i * CONV_TC, CONV_TC)
        win_rows = CONV_TC + CONV_PAD
        win = pad_ref[pl.ds(t0, win_rows), :]
        acc = jnp.zeros((CONV_TC, CONV_WIDTH), F32)
        for b in range(SUBLANES):
            rolled = win if b == 0 else pltpu.roll(win, win_rows - b, 0)
            for a in range(win_rows // SUBLANES):
                k = a * SUBLANES + b - (CONV_PAD - (CONV_K - 1))
                if 0 <= k < CONV_K:
                    acc = acc + wdw[k:k + 1, :] * rolled[a * SUBLANES:a * SUBLANES + CONV_TC]
        y = acc + bdw_ref[...]
        mu = jnp.mean(y, axis=-1, keepdims=True)
        yc = y - mu
        var = jnp.mean(yc * yc, axis=-1, keepdims=True)
        yn = yc * lax.rsqrt(var + EPS) * lng_ref[...] + lnb_ref[...]
        act = yn * _sigmoid(yn)
        y_ref[pl.ds(t0, CONV_TC), :] = (_dot(act.astype(BF16), wpw_ref[...]) + bpw_ref[...]).astype(BF16)

    def tile_group(i, carry):
        for j in range(CONV_UNROLL):
            tile(CONV_UNROLL * i + j)
        return carry

    lax.fori_loop(0, seq // (CONV_UNROLL * CONV_TC), tile_group, 0)


def _conv(u, wdw, bdw, lng, lnb, wpw, bpw, batch, seq):
    full = lambda a: pl.BlockSpec(a.shape, lambda b: (0,) * a.ndim)
    return pl.pallas_call(
        _conv_kernel,
        grid=(batch,),
        in_specs=[pl.BlockSpec((seq, CONV_WIDTH), lambda b: (b, 0)),
                  full(wdw), full(bdw), full(lng), full(lnb), full(wpw), full(bpw)],
        out_specs=pl.BlockSpec((seq, CONV_WIDTH), lambda b: (b, 0)),
        out_shape=jax.ShapeDtypeStruct((batch * seq, CONV_WIDTH), BF16),
        scratch_shapes=[pltpu.VMEM((CONV_PAD + seq, CONV_WIDTH), F32)],
        compiler_params=pltpu.CompilerParams(dimension_semantics=("parallel",), vmem_limit_bytes=VMEM_LIMIT),
        name="conv",
    )(u, wdw, bdw, lng, lnb, wpw, bpw)


def _out_ffn_kernel(x_ref, ya_ref, ys_ref, yc_ref, ga_ref, gs_ref, gc_ref, wo_ref, gf_ref,
                    wg_ref, wu_ref, wd_ref, gfin_ref, o_ref, *, final_norm):
    na = _rms(ya_ref[...].astype(F32), ga_ref[...]).astype(BF16)
    ns = _rms(ys_ref[...].astype(F32), gs_ref[...]).astype(BF16)
    nc = _rms(yc_ref[...].astype(F32), gc_ref[...]).astype(BF16)
    o1, o2 = ATTN_WIDTH, ATTN_WIDTH + SSM_WIDTH
    x = x_ref[...] + _dot(na, wo_ref[0:o1, :]) + _dot(ns, wo_ref[o1:o2, :]) + _dot(nc, wo_ref[o2:, :])
    o_ref[...] = x
    h = _rms(x, gf_ref[...]).astype(BF16)
    gate = _dot(h, wg_ref[...])
    up = _dot(h, wu_ref[...])
    x = o_ref[...] + _dot((gate * _sigmoid(gate) * up).astype(BF16), wd_ref[...])
    if final_norm:
        x = _rms(x, gfin_ref[...])
    o_ref[...] = x


def _out_ffn(x2d, y_attn, y_ssm_tm, y_conv, ga, gs, gc, wo, gf, wg, wu, wd, gfin, batch, seq, final_norm):
    tm = TM_PROJ
    tiles_per_seq = seq // tm
    tokens = batch * seq
    row = lambda n: pl.BlockSpec((tm, n), lambda i: (i, 0))
    resident = lambda a: pl.BlockSpec(a.shape, lambda i: (0,) * a.ndim)
    return pl.pallas_call(
        functools.partial(_out_ffn_kernel, final_norm=final_norm),
        grid=(tokens // tm,),
        in_specs=[row(D_MODEL), row(ATTN_WIDTH),
                  pl.BlockSpec((tm, SSM_WIDTH), lambda i: (i % tiles_per_seq, i // tiles_per_seq)),
                  row(CONV_WIDTH),
                  resident(ga), resident(gs), resident(gc), resident(wo), resident(gf),
                  resident(wg), resident(wu), resident(wd), resident(gfin)],
        out_specs=row(D_MODEL),
        out_shape=jax.ShapeDtypeStruct((tokens, D_MODEL), F32),
        compiler_params=pltpu.CompilerParams(dimension_semantics=("parallel",), vmem_limit_bytes=VMEM_LIMIT),
        name="out_ffn",
    )(x2d, y_attn, y_ssm_tm, y_conv, ga, gs, gc, wo, gf, wg, wu, wd, gfin)


def _rope_tables(pos):
    half = HEAD_DIM // 2
    freqs = ROPE_THETA ** (-jnp.arange(half, dtype=F32) / half)
    ang = pos.astype(F32)[:, None] * freqs[None, :]
    cos = jnp.tile(jnp.cos(ang), (1, LANES // half))
    sin = jnp.sin(ang)
    sin_signed = jnp.tile(jnp.concatenate([-sin, sin], axis=1), (1, LANES // HEAD_DIM))
    return cos, sin_signed


def _pack_w_in(w_in):
    sizes = (ATTN_WIDTH,) + (KV_WIDTH,) * 6 + (N_BRANCH * N_Q_HEADS, SSM_WIDTH, 2 * CONV_WIDTH)
    parts = jnp.split(w_in, np.cumsum(sizes)[:-1].tolist(), axis=1)
    gate = jnp.pad(parts[7], ((0, 0), (0, GATE_PAD - N_BRANCH * N_Q_HEADS)))
    return jnp.concatenate(parts[:7] + parts[8:] + [gate], axis=1).astype(BF16)


def _pack_cmp_w1(w1):
    w = w1.reshape(2, CMP_STRIDE, HEAD_DIM, CMP_HIDDEN)
    eye = jnp.eye(N_KV_HEADS, dtype=w1.dtype)
    big = jnp.einsum('rpdh,ab->padrbh', w, eye)
    return big.reshape(CMP_STRIDE * N_KV_HEADS * HEAD_DIM, 2 * N_KV_HEADS * CMP_HIDDEN).astype(BF16)


def _pe_operand(pe):
    lo = jnp.tile(pe[:CMP_STRIDE, None, :], (1, N_KV_HEADS, 1)).reshape(1, -1)
    hi = jnp.tile(pe[CMP_STRIDE:, None, :], (1, N_KV_HEADS, 1)).reshape(1, -1)
    return jnp.concatenate([jnp.tile(lo, (4, 1)), jnp.tile(hi, (4, 1))], axis=0).astype(BF16)


def _block_diag(blocks):
    g, a, b = blocks.shape
    eye = jnp.eye(g, dtype=blocks.dtype)
    return jnp.einsum('gab,gh->gahb', blocks, eye).reshape(g * a, g * b)


def _overlap_matrix(seq):
    n_cmp = seq // CMP_STRIDE
    n_sel = seq // SLC_BLOCK
    ci = np.arange(n_cmp)[None, :]
    sj = np.arange(n_sel)[:, None]
    ovl = (ci * CMP_STRIDE < (sj + 1) * SLC_BLOCK) & (ci * CMP_STRIDE + CMP_LEN > sj * SLC_BLOCK) & (ci < n_cmp - 1)
    return jnp.asarray(ovl.astype(np.float32), dtype=BF16)


def _gate_spread_matrix():
    e = np.zeros((N_BRANCH, N_KV_HEADS, GATE_PAD, GQA_GROUP * LANES), np.float32)
    for br in range(N_BRANCH):
        for g in range(N_KV_HEADS):
            for r in range(GQA_GROUP):
                e[br, g, br * N_Q_HEADS + g * GQA_GROUP + r, r * LANES:(r + 1) * LANES] = 1.0
    return jnp.asarray(e, dtype=BF16)


def _layer(x2d, p, batch, seq, tables, final_norm, norm_final):
    cos, sin, cos_c, sin_c, ovl, gexp = tables
    row = lambda v: v.reshape(1, -1).astype(F32)

    q, kc_raw, vc_raw, ksa, vsa, kwa, vwa, gates, u_ssm, u_conv = _in_proj(
        x2d, row(p['norm_mix']), _pack_w_in(p['w_in']), cos, sin, batch, seq)

    n_half = seq // CMP_STRIDE
    kh = kc_raw.reshape(batch, n_half, CMP_STRIDE * KV_WIDTH)
    vh = vc_raw.reshape(batch, n_half, CMP_STRIDE * KV_WIDTH)

    wk = _pack_cmp_w1(p['cmp_k_w1'])
    wv = _pack_cmp_w1(p['cmp_v_w1'])
    pek = _pe_operand(p['cmp_pe_k'])
    pev = _pe_operand(p['cmp_pe_v'])
    kca, vca = _compress(kh, vh, wk, wv, pek, pev, p['cmp_k_w2'].astype(BF16), p['cmp_v_w2'].astype(BF16),
                       cos_c, sin_c)

    y_attn = _attention(q, kca, vca, ksa, vsa, kwa, vwa, gates, ovl, gexp, batch, seq)

    lam_re, lam_im, bb_re, bb_im = _ssm_prep(
        p['ssm_a_re'], p['ssm_a_im'], p['ssm_log_dt'].reshape(-1, 1),
        p['ssm_b_re'].transpose(0, 2, 1), p['ssm_b_im'].transpose(0, 2, 1))
    wb = jnp.concatenate([_block_diag(bb_re), _block_diag(bb_im)], axis=1).astype(BF16)
    wc = jnp.concatenate([_block_diag(p['ssm_c_re'].transpose(0, 2, 1)),
                          -_block_diag(p['ssm_c_im'].transpose(0, 2, 1))], axis=0).astype(BF16)
    lam_re_b = jnp.broadcast_to(lam_re.reshape(1, -1), (batch, SSM_STATES_ALL))
    lam_im_b = jnp.broadcast_to(lam_im.reshape(1, -1), (batch, SSM_STATES_ALL))
    y_ssm = _ssm(u_ssm.reshape(seq * batch, SSM_WIDTH), wb, lam_re_b, lam_im_b, wc, row(p['ssm_d']),
                 p['ssm_w_glu'].astype(BF16), batch, seq)
    y_ssm_tm = y_ssm.reshape(seq, batch * SSM_WIDTH)

    wdw = jnp.pad(p['conv_w_dw'], ((0, CONV_PAD - CONV_K), (0, 0)))
    y_conv = _conv(u_conv, wdw, row(p['conv_b_dw']), row(p['conv_ln_g']), row(p['conv_ln_b']),
                   p['conv_w_pw'].astype(BF16), row(p['conv_b_pw']), batch, seq)

    return _out_ffn(x2d, y_attn, y_ssm_tm, y_conv, row(p['norm_out_attn']), row(p['norm_out_ssm']),
                    row(p['norm_out_conv']), p['w_out'].astype(BF16), row(p['norm_ffn']),
                    p['w_gate'].astype(BF16), p['w_up'].astype(BF16), p['w_down'].astype(BF16),
                    row(norm_final), batch, seq, final_norm)


_LAYER_KEYS = ('norm_mix', 'w_in', 'cmp_pe_k', 'cmp_k_w1', 'cmp_k_w2', 'cmp_pe_v', 'cmp_v_w1', 'cmp_v_w2',
               'ssm_a_re', 'ssm_a_im', 'ssm_log_dt', 'ssm_b_re', 'ssm_b_im', 'ssm_c_re', 'ssm_c_im', 'ssm_d',
               'ssm_w_glu', 'conv_w_dw', 'conv_b_dw', 'conv_ln_g', 'conv_ln_b', 'conv_w_pw', 'conv_b_pw',
               'norm_out_attn', 'norm_out_ssm', 'norm_out_conv', 'w_out', 'norm_ffn', 'w_gate', 'w_up', 'w_down')


def kernel(x, norm_mix, w_in, cmp_pe_k, cmp_k_w1, cmp_k_w2, cmp_pe_v, cmp_v_w1, cmp_v_w2, ssm_a_re, ssm_a_im, ssm_log_dt, ssm_b_re, ssm_b_im, ssm_c_re, ssm_c_im, ssm_d, ssm_w_glu, conv_w_dw, conv_b_dw, conv_ln_g, conv_ln_b, conv_w_pw, conv_b_pw, norm_out_attn, norm_out_ssm, norm_out_conv, w_out, norm_ffn, w_gate, w_up, w_down, norm_final):
    stacked = dict(zip(_LAYER_KEYS, (norm_mix, w_in, cmp_pe_k, cmp_k_w1, cmp_k_w2, cmp_pe_v, cmp_v_w1, cmp_v_w2,
                                     ssm_a_re, ssm_a_im, ssm_log_dt, ssm_b_re, ssm_b_im, ssm_c_re, ssm_c_im, ssm_d,
                                     ssm_w_glu, conv_w_dw, conv_b_dw, conv_ln_g, conv_ln_b, conv_w_pw, conv_b_pw,
                                     norm_out_attn, norm_out_ssm, norm_out_conv, w_out, norm_ffn, w_gate, w_up,
                                     w_down)))
    batch, seq, _ = x.shape
    depth = norm_mix.shape[0]
    cos, sin = _rope_tables(jnp.arange(seq))
    cos_c, sin_c = _rope_tables(jnp.arange(seq // CMP_STRIDE) * CMP_STRIDE + CMP_LEN - 1)
    tables = (cos, sin, cos_c, sin_c, _overlap_matrix(seq), _gate_spread_matrix())
    x2d = x.reshape(batch * seq, D_MODEL)
    for l in range(depth):
        p = {k: v[l] for k, v in stacked.items()}
        x2d = _layer(x2d, p, batch, seq, tables, l == depth - 1, norm_final)
    return x2d.reshape(batch, seq, D_MODEL)
```

```python
import functools
import math

import numpy as np
import jax
import jax.numpy as jnp
from jax import lax
from jax.experimental import pallas as pl
from jax.experimental.pallas import tpu as pltpu

F32 = jnp.float32
BF16 = jnp.bfloat16

D_MODEL = 1024
HEAD_DIM = 64
N_Q_HEADS = 8
N_KV_HEADS = 2
GQA_GROUP = N_Q_HEADS // N_KV_HEADS
ATTN_WIDTH = N_Q_HEADS * HEAD_DIM
KV_WIDTH = N_KV_HEADS * HEAD_DIM
N_BRANCH = 3
SSM_WIDTH = D_MODEL // 4
SSM_GROUP = 16
SSM_N_GROUPS = SSM_WIDTH // SSM_GROUP
SSM_STATE = 64
SSM_STATES_ALL = SSM_N_GROUPS * SSM_STATE
CONV_WIDTH = D_MODEL // 4
CONV_K = 31
CMP_LEN = 32
CMP_STRIDE = 16
CMP_HIDDEN = 256
SLC_BLOCK = 64
SLC_TOPN = 16
WINDOW = 512
ROPE_THETA = 10000.0
D_FF = ((8 * D_MODEL // 3 + 255) // 256) * 256
EPS = 1e-6
NEG_INF = -1e30
Q_SCALE = HEAD_DIM ** -0.5 * math.log2(math.e)
FORCE = 1e9

LANES = 128
SUBLANES = 8
VMEM_LIMIT = 56 * 1024 * 1024

GATE_PAD = LANES
PROJ_BLOCK = 512
C_Q = 0
C_KC = C_Q + ATTN_WIDTH
C_VC = C_KC + KV_WIDTH
C_KS = C_VC + KV_WIDTH
C_VS = C_KS + KV_WIDTH
C_KW = C_VS + KV_WIDTH
C_VW = C_KW + KV_WIDTH
C_SSM = C_VW + KV_WIDTH
C_CONV = C_SSM + SSM_WIDTH
C_GATE = C_CONV + 2 * CONV_WIDTH
IN_PACKED = C_GATE + GATE_PAD

TM_PROJ = 512
TM_IN = 1024
TQ = 256
CK = 1024
WIN_PIECE = 256
RANK_GROUP = 16
SSM_LC = 128
CONV_TC = 128
CONV_UNROLL = 4
CONV_PAD = 32


def _gelu_tanh(x):
    return 0.5 * x * (1.0 + jnp.tanh(math.sqrt(2.0 / math.pi) * (x + 0.044715 * (x * x * x))))


def _sigmoid(x):
    return 1.0 / (1.0 + jnp.exp(-x))


def _rms(x, g):
    return x * lax.rsqrt(jnp.mean(x * x, axis=-1, keepdims=True) + EPS) * g


def _dot(a, b):
    return jnp.dot(a, b, preferred_element_type=F32)


def _dot_nt(a, b):
    return lax.dot_general(a, b, (((1,), (1,)), ((), ())), preferred_element_type=F32)


def _rope_rotate(v, cos, sin_signed):
    lane = lax.broadcasted_iota(jnp.int32, v.shape, 1)
    first_half = (lane % HEAD_DIM) < (HEAD_DIM // 2)
    rot = jnp.where(first_half, pltpu.roll(v, LANES - HEAD_DIM // 2, 1), pltpu.roll(v, HEAD_DIM // 2, 1))
    return v * cos + rot * sin_signed


def _in_proj_kernel(x_ref, g_ref, w_ref, cos_ref, sin_ref,
                    q_ref, kc_ref, vc_ref, ksa_ref, vsa_ref, kwa_ref, vwa_ref, gate_ref, ussm_ref, uconv_ref,
                    *, tiles_per_seq):
    tm = x_ref.shape[0]
    h = _rms(x_ref[...], g_ref[...]).astype(BF16)
    cos = cos_ref[...]
    sin = sin_ref[...]

    blocks = {}

    def proj(c0, n):
        b0 = (c0 // PROJ_BLOCK) * PROJ_BLOCK
        if b0 not in blocks:
            blocks[b0] = _dot(h, w_ref[:, b0:min(b0 + PROJ_BLOCK, IN_PACKED)])
        return blocks[b0][:, c0 - b0:c0 - b0 + n]

    lane = lax.broadcasted_iota(jnp.int32, (tm, LANES), 1)
    row = lax.broadcasted_iota(jnp.int32, (tm, LANES), 0)
    t = (pl.program_id(0) % tiles_per_seq) * tm + row
    low = lane < HEAD_DIM
    for j in range(ATTN_WIDTH // LANES):
        qj = _rope_rotate(proj(C_Q + j * LANES, LANES), cos, sin) * Q_SCALE
        q_ref[:, (2 * j) * LANES:(2 * j + 1) * LANES] = jnp.where(low, qj, 0.0).astype(BF16)
        q_ref[:, (2 * j + 1) * LANES:(2 * j + 2) * LANES] = jnp.where(low, pltpu.roll(qj, HEAD_DIM, 1), 0.0).astype(BF16)
    kc_ref[...] = proj(C_KC, KV_WIDTH).astype(BF16)
    vc_ref[...] = proj(C_VC, KV_WIDTH).astype(BF16)
    blk_onehot = jnp.where((t // SLC_BLOCK) == (lane - HEAD_DIM), 1.0, 0.0)
    ones_up = jnp.where(low, 0.0, 1.0)

    def split_groups(v, upper, ref):
        ref[0, 0] = jnp.where(low, v, upper).astype(BF16)
        ref[0, 1] = jnp.where(low, pltpu.roll(v, HEAD_DIM, 1), upper).astype(BF16)

    split_groups(_rope_rotate(proj(C_KS, KV_WIDTH), cos, sin), blk_onehot, ksa_ref)
    split_groups(proj(C_VS, KV_WIDTH), ones_up, vsa_ref)
    split_groups(_rope_rotate(proj(C_KW, KV_WIDTH), cos, sin), 0.0, kwa_ref)
    split_groups(proj(C_VW, KV_WIDTH), ones_up, vwa_ref)

    gate_ref[...] = _sigmoid(proj(C_GATE, GATE_PAD))
    ussm_ref[...] = proj(C_SSM, SSM_WIDTH).astype(BF16)
    a = proj(C_CONV, CONV_WIDTH)
    g = proj(C_CONV + CONV_WIDTH, CONV_WIDTH)
    uconv_ref[...] = a * _sigmoid(g)


def _in_proj(x2d, g, w_packed, cos, sin, batch, seq):
    tm = TM_IN
    tiles_per_seq = seq // tm
    tokens = batch * seq
    row_blk = lambda n, dt: (pl.BlockSpec((tm, n), lambda i: (i, 0)), jax.ShapeDtypeStruct((tokens, n), dt))
    grp_blk = (pl.BlockSpec((1, N_KV_HEADS, tm, LANES), lambda i: (i // tiles_per_seq, 0, i % tiles_per_seq, 0)),
               jax.ShapeDtypeStruct((batch, N_KV_HEADS, seq, LANES), BF16))
    outs = [row_blk(N_Q_HEADS * LANES, BF16), row_blk(KV_WIDTH, BF16), row_blk(KV_WIDTH, BF16),
            grp_blk, grp_blk, grp_blk, grp_blk, row_blk(GATE_PAD, F32),
            (pl.BlockSpec((tm, SSM_WIDTH), lambda i: (i % tiles_per_seq, i // tiles_per_seq)),
             jax.ShapeDtypeStruct((seq, batch * SSM_WIDTH), BF16)),
            row_blk(CONV_WIDTH, F32)]
    return pl.pallas_call(
        functools.partial(_in_proj_kernel, tiles_per_seq=tiles_per_seq),
        grid=(tokens // tm,),
        in_specs=[pl.BlockSpec((tm, D_MODEL), lambda i: (i, 0)),
                  pl.BlockSpec((1, D_MODEL), lambda i: (0, 0)),
                  pl.BlockSpec((D_MODEL, IN_PACKED), lambda i: (0, 0)),
                  pl.BlockSpec((tm, LANES), lambda i: (i % tiles_per_seq, 0)),
                  pl.BlockSpec((tm, LANES), lambda i: (i % tiles_per_seq, 0))],
        out_specs=[o[0] for o in outs],
        out_shape=[o[1] for o in outs],
        compiler_params=pltpu.CompilerParams(dimension_semantics=("parallel",), vmem_limit_bytes=VMEM_LIMIT),
        name="in_proj",
    )(x2d, g, w_packed, cos, sin)


def _compress_kernel(kh_ref, vh_ref, wk_ref, wv_ref, pek_ref, pev_ref, w2k_ref, w2v_ref, cos_ref, sin_ref,
                     kca_ref, vca_ref):
    n_half = kh_ref.shape[1]
    hw = N_KV_HEADS * CMP_HIDDEN

    def compress(h_ref, w_ref, pe_ref, w2_ref):
        ab = _dot(h_ref[0], w_ref[...])
        pe = _dot(pe_ref[...], w_ref[...])
        bias = pe[0:1, 0:hw] + pe[4:5, hw:2 * hw]
        nxt = pltpu.roll(ab[:, hw:2 * hw], n_half - 1, 0)
        hid = _gelu_tanh(ab[:, 0:hw] + nxt + bias).astype(BF16)
        return jnp.concatenate([_dot(hid[:, j * CMP_HIDDEN:(j + 1) * CMP_HIDDEN], w2_ref[...])
                                for j in range(N_KV_HEADS)], axis=1)

    def split_groups(v, upper, ref):
        low = lax.broadcasted_iota(jnp.int32, v.shape, 1) < HEAD_DIM
        ref[0, 0] = jnp.where(low, v, upper).astype(BF16)
        ref[0, 1] = jnp.where(low, pltpu.roll(v, HEAD_DIM, 1), upper).astype(BF16)

    kc = compress(kh_ref, wk_ref, pek_ref, w2k_ref)
    split_groups(_rope_rotate(kc, cos_ref[...], sin_ref[...]), 0.0, kca_ref)
    split_groups(compress(vh_ref, wv_ref, pev_ref, w2v_ref), 1.0, vca_ref)


def _compress(kh, vh, wk, wv, pek, pev, w2k, w2v, cos_c, sin_c):
    batch, n_half, width = kh.shape
    full = lambda a: pl.BlockSpec(a.shape, lambda b: (0,) * a.ndim)
    per_b = pl.BlockSpec((1, n_half, width), lambda b: (b, 0, 0))
    out_blk = pl.BlockSpec((1, N_KV_HEADS, n_half, LANES), lambda b: (b, 0, 0, 0))
    out_shape = jax.ShapeDtypeStruct((batch, N_KV_HEADS, n_half, LANES), BF16)
    return pl.pallas_call(
        _compress_kernel,
        grid=(batch,),
        in_specs=[per_b, per_b, full(wk), full(wv), full(pek), full(pev), full(w2k), full(w2v),
                  full(cos_c), full(sin_c)],
        out_specs=[out_blk, out_blk],
        out_shape=[out_shape, out_shape],
        compiler_params=pltpu.CompilerParams(dimension_semantics=("parallel",), vmem_limit_bytes=VMEM_LIMIT),
        name="compress",
    )(kh, vh, wk, wv, pek, pev, w2k, w2v, cos_c, sin_c)


def _attention_kernel(q_ref, kca_ref, vca_ref, ksa_ref, vsa_ref, kwa_ref, vwa_ref, gate_ref, ovl_ref, gexp_ref,
                      y_ref,
                      qaug_ref, oc_ref, s0_ref, s1_ref, mx_ref, m_ref, acc_ref, val_ref, rank_ref):
    tq = q_ref.shape[0]
    rows = GQA_GROUP * tq
    n_cmp = kca_ref.shape[2]
    seq = ksa_ref.shape[2]
    n_sel = seq // SLC_BLOCK
    t0 = pl.program_id(1) * tq
    row_q = lax.broadcasted_iota(jnp.int32, (rows, 1), 0) % tq
    t_row = t0 + row_q
    t_rep = t0 + lax.broadcasted_iota(jnp.int32, (rows, LANES), 0) % tq
    lane_iota = lax.broadcasted_iota(jnp.int32, (rows, LANES), 1)
    diag_rep = t_rep - lane_iota
    cmp_rep = t_rep - lane_iota * CMP_STRIDE - (CMP_LEN - 1)
    last_blk = (t0 + tq - 1) // SLC_BLOCK

    def row_max_rep(x):
        return jnp.broadcast_to(jnp.max(x, axis=-1, keepdims=True), x.shape)

    def sums_on_all_lanes(a):
        return jnp.where(lane_iota < HEAD_DIM, pltpu.roll(a, HEAD_DIM, 1), a)

    def group_queries(g):
        heads = [q_ref[:, (g * GQA_GROUP + r) * LANES:(g * GQA_GROUP + r + 1) * LANES] for r in range(GQA_GROUP)]
        return jnp.concatenate(heads, axis=0)

    blk = lax.broadcasted_iota(jnp.int32, (n_sel, tq), 0)
    cur = (t0 + lax.broadcasted_iota(jnp.int32, (n_sel, tq), 1)) // SLC_BLOCK
    causal = blk <= cur

    def compressed_branch(n_slabs):
        n_used = n_slabs * LANES
        for g in range(N_KV_HEADS):
            sc = _dot_nt(group_queries(g), kca_ref[0, g, 0:n_used, :])
            slabs = []
            for j in range(n_slabs):
                slabs.append(jnp.where(cmp_rep >= j * LANES * CMP_STRIDE, sc[:, j * LANES:(j + 1) * LANES],
                                       NEG_INF))
            mc = slabs[0]
            for sj in slabs[1:]:
                mc = jnp.maximum(mc, sj)
            mc = row_max_rep(mc)
            pc = jnp.concatenate([jnp.exp2(sj - mc) for sj in slabs], axis=1)
            oc = _dot(pc.astype(BF16), vca_ref[0, g, 0:n_used, :])
            inv = jnp.where(t_rep >= CMP_LEN - 1, 1.0 / jnp.maximum(sums_on_all_lanes(oc), 1e-30), 0.0)
            oc_ref[g] = oc * inv
            pc = pc * jnp.concatenate([inv] * n_slabs, axis=1)

            pc_sum = pc[0:tq]
            for r in range(1, GQA_GROUP):
                pc_sum = pc_sum + pc[r * tq:(r + 1) * tq]
            p_hi = pc_sum.astype(BF16)
            rem = pc_sum - p_hi.astype(F32)
            p_mid = rem.astype(BF16)
            p_lo = (rem - p_mid.astype(F32)).astype(BF16)
            ovl = ovl_ref[:, 0:n_used]
            imp = _dot_nt(ovl, p_hi) + _dot_nt(ovl, p_mid) + _dot_nt(ovl, p_lo)
            val = jnp.where(blk == 0, FORCE, jnp.where(blk >= cur - 1, FORCE, imp))
            val_ref[:, g * tq:(g + 1) * tq] = jnp.where(causal, val, -FORCE)

    compressed_branch(n_cmp // LANES)

    val = val_ref[...]
    rank_ref[...] = jnp.zeros(rank_ref.shape, jnp.int32)
    grp_rows = lax.broadcasted_iota(jnp.int32, (RANK_GROUP, N_KV_HEADS * tq), 0)
    for ig in range(n_sel // RANK_GROUP):
        for jg in range(n_sel // RANK_GROUP):
            @pl.when(max(ig, jg) * RANK_GROUP <= last_blk)
            def _(ig=ig, jg=jg):
                vj = val[jg * RANK_GROUP:(jg + 1) * RANK_GROUP]
                cnt = jnp.zeros(vj.shape, jnp.int32)
                for ii in range(RANK_GROUP):
                    i = ig * RANK_GROUP + ii
                    vi = val[i:i + 1, :]
                    if jg > ig:
                        cnt = cnt + jnp.where(vi >= vj, 1, 0)
                    elif jg < ig:
                        cnt = cnt + jnp.where(vi > vj, 1, 0)
                    else:
                        tie = jnp.where(grp_rows > ii, 1, 0)
                        cnt = cnt + jnp.where(vi > vj, 1, jnp.where(vi == vj, tie, 0))
                sl = pl.ds(jg * RANK_GROUP, RANK_GROUP)
                rank_ref[sl, :] = rank_ref[sl, :] + cnt

    for g in range(N_KV_HEADS):
        rank = rank_ref[:, g * tq:(g + 1) * tq]
        bias_t = jnp.where(causal, jnp.where(rank < SLC_TOPN, 0.0, NEG_INF), NEG_INF)
        pieces = [jnp.zeros((HEAD_DIM, tq), F32), bias_t]
        if n_sel < LANES - HEAD_DIM:
            pieces.append(jnp.zeros((LANES - HEAD_DIM - n_sel, tq), F32))
        bias = jnp.concatenate(pieces, axis=0).T.astype(BF16)
        qaug_ref[g] = group_queries(g) + jnp.concatenate([bias] * GQA_GROUP, axis=0)

    n_chunks = (t0 + tq + CK - 1) // CK
    last = n_chunks - 1
    mx_ref[...] = jnp.full(mx_ref.shape, NEG_INF, F32)
    acc_ref[...] = jnp.zeros(acc_ref.shape, F32)
    s_refs = (s0_ref, s1_ref)

    def chunk_keys(ref, g, c, width=CK):
        return ref[0, g, pl.ds(pl.multiple_of(c * CK, CK), width), :]

    def score_chunk(g, c, diagonal, width=CK):
        s = _dot_nt(qaug_ref[g], chunk_keys(ksa_ref, g, c, width))
        mx = mx_ref[g]
        slabs = []
        for j in range(width // LANES):
            sj = s[:, j * LANES:(j + 1) * LANES]
            if diagonal:
                sj = jnp.where(diag_rep >= c * CK + j * LANES, sj, NEG_INF)
            mx = jnp.maximum(mx, sj)
            slabs.append(sj)
        s_refs[g][c, :, 0:width] = jnp.concatenate(slabs, axis=1)
        mx_ref[g] = mx

    def value_chunk(g, c, width=CK):
        s = s_refs[g][c, :, 0:width]
        m = m_ref[g]
        p = jnp.concatenate([jnp.exp2(s[:, j * LANES:(j + 1) * LANES] - m) for j in range(width // LANES)], axis=1)
        acc_ref[g] = acc_ref[g] + _dot(p.astype(BF16), chunk_keys(vsa_ref, g, c, width))

    last_is_half = t0 + tq - last * CK <= CK // 2

    def sweep(body, final):
        def pair(i, carry):
            body(2 * i)
            body(2 * i + 1)
            return carry

        def single(c, carry):
            body(c)
            return carry
        n_pairs = last // 2
        lax.fori_loop(0, n_pairs, pair, 0)
        lax.fori_loop(2 * n_pairs, last, single, 0)
        pl.when(last_is_half)(functools.partial(final, CK // 2))
        pl.when(jnp.logical_not(last_is_half))(functools.partial(final, CK))

    w0 = pl.multiple_of(jnp.maximum(t0 - WINDOW, 0), tq)
    n_pieces = (WINDOW + tq) // WIN_PIECE

    def window_branch(g, stage_ref):
        mw = None
        for j in range(n_pieces):
            sw = _dot_nt(qaug_ref[g], kwa_ref[0, g, pl.ds(w0 + j * WIN_PIECE, WIN_PIECE), :])
            slabs = []
            for jj in range(WIN_PIECE // LANES):
                slab = j * (WIN_PIECE // LANES) + jj
                k_first = w0 + slab * LANES
                sj = jnp.where(diag_rep >= k_first, sw[:, jj * LANES:(jj + 1) * LANES], NEG_INF)
                if slab < tq // LANES:
                    sj = jnp.where(diag_rep < k_first + WINDOW, sj, NEG_INF)
                mw = sj if mw is None else jnp.maximum(mw, sj)
                slabs.append(sj)
            stage_ref[j, :, 0:WIN_PIECE] = jnp.concatenate(slabs, axis=1)
        mw = row_max_rep(mw)
        aw = None
        for j in range(n_pieces):
            sw = stage_ref[j, :, 0:WIN_PIECE]
            pw = jnp.concatenate([jnp.exp2(sw[:, jj * LANES:(jj + 1) * LANES] - mw)
                                  for jj in range(WIN_PIECE // LANES)], axis=1)
            part = _dot(pw.astype(BF16), vwa_ref[0, g, pl.ds(w0 + j * WIN_PIECE, WIN_PIECE), :])
            aw = part if aw is None else aw + part
        mx_ref[g] = aw

    sweep(lambda c: score_chunk(0, c, False), lambda width: score_chunk(0, last, True, width))
    m_ref[0] = row_max_rep(mx_ref[0])

    def both(c):
        value_chunk(0, c)
        score_chunk(1, c, False)

    def both_last(width):
        value_chunk(0, last, width)
        score_chunk(1, last, True, width)

    sweep(both, both_last)
    m_ref[1] = row_max_rep(mx_ref[1])
    sweep(lambda c: value_chunk(1, c), lambda width: value_chunk(1, last, width))
    for g in range(N_KV_HEADS):
        window_branch(g, s_refs[g])

    gates = gate_ref[...]
    g_hi = gates.astype(BF16)
    g_lo = (gates - g_hi.astype(F32)).astype(BF16)
    lane = lax.broadcasted_iota(jnp.int32, (tq, LANES), 1)
    for g in range(N_KV_HEADS):
        def gate_rep(branch):
            e = gexp_ref[branch, g]
            spread = _dot(g_hi, e) + _dot(g_lo, e)
            return jnp.concatenate([spread[:, r * LANES:(r + 1) * LANES] for r in range(GQA_GROUP)], axis=0)

        asel = acc_ref[g]
        aw = mx_ref[g]
        o = (gate_rep(0) * oc_ref[g]
             + gate_rep(1) * asel / jnp.maximum(pltpu.roll(asel, HEAD_DIM, 1), 1e-30)
             + gate_rep(2) * aw / jnp.maximum(pltpu.roll(aw, HEAD_DIM, 1), 1e-30))
        for pair in range(GQA_GROUP // 2):
            lo = o[(2 * pair) * tq:(2 * pair + 1) * tq]
            hi = o[(2 * pair + 1) * tq:(2 * pair + 2) * tq]
            col = g * GQA_GROUP * HEAD_DIM + pair * LANES
            y_ref[:, col:col + LANES] = jnp.where(lane < HEAD_DIM, lo, pltpu.roll(hi, HEAD_DIM, 1)).astype(BF16)


def _attention(q, kca, vca, ksa, vsa, kwa, vwa, gates, ovl, gexp, batch, seq):
    tq = TQ
    nq = seq // tq
    n_cmp = kca.shape[2]
    rows = GQA_GROUP * tq
    grp = pl.BlockSpec((1, N_KV_HEADS, seq, LANES), lambda b, i: (b, 0, 0, 0), pipeline_mode=pl.Buffered(1))
    cmp_grp = pl.BlockSpec((1, N_KV_HEADS, n_cmp, LANES), lambda b, i: (b, 0, 0, 0))
    grp_rows = lambda dt: pltpu.VMEM((N_KV_HEADS, rows, LANES), dt)
    return pl.pallas_call(
        _attention_kernel,
        grid=(batch, nq),
        in_specs=[pl.BlockSpec((tq, N_Q_HEADS * LANES), lambda b, i: (b * nq + i, 0)),
                  cmp_grp, cmp_grp, grp, grp, grp, grp,
                  pl.BlockSpec((tq, GATE_PAD), lambda b, i: (b * nq + i, 0)),
                  pl.BlockSpec(ovl.shape, lambda b, i: (0, 0)),
                  pl.BlockSpec(gexp.shape, lambda b, i: (0, 0, 0, 0))],
        out_specs=pl.BlockSpec((tq, ATTN_WIDTH), lambda b, i: (b * nq + i, 0)),
        out_shape=jax.ShapeDtypeStruct((batch * seq, ATTN_WIDTH), BF16),
        scratch_shapes=[grp_rows(BF16),
                        grp_rows(F32),
                        pltpu.VMEM((seq // CK, rows, CK), F32),
                        pltpu.VMEM((seq // CK, rows, CK), F32),
                        grp_rows(F32),
                        grp_rows(F32),
                        grp_rows(F32),
                        pltpu.VMEM((seq // SLC_BLOCK, N_KV_HEADS * tq), F32),
                        pltpu.VMEM((seq // SLC_BLOCK, N_KV_HEADS * tq), jnp.int32)],
        compiler_params=pltpu.CompilerParams(dimension_semantics=("parallel", "parallel"),
                                             vmem_limit_bytes=VMEM_LIMIT),
        name="attention",
    )(q, kca, vca, ksa, vsa, kwa, vwa, gates, ovl, gexp)


def _ssm_prep_kernel(a_re_ref, a_im_ref, log_dt_ref, b_re_ref, b_im_ref,
                     lam_re_ref, lam_im_ref, bb_re_ref, bb_im_ref):
    a_re = a_re_ref[...]
    a_im = a_im_ref[...]
    dt = jnp.exp(log_dt_ref[...])
    mag = jnp.exp(a_re * dt)
    lam_re = mag * jnp.cos(a_im * dt)
    lam_im = mag * jnp.sin(a_im * dt)
    den = a_re * a_re + a_im * a_im
    nr, ni = lam_re - 1.0, lam_im
    f_re = (nr * a_re + ni * a_im) / den
    f_im = (ni * a_re - nr * a_im) / den
    lam_re_ref[...] = lam_re
    lam_im_ref[...] = lam_im
    b_re = b_re_ref[...]
    b_im = b_im_ref[...]
    bb_re_ref[...] = f_re[:, None, :] * b_re - f_im[:, None, :] * b_im
    bb_im_ref[...] = f_re[:, None, :] * b_im + f_im[:, None, :] * b_re


def _ssm_prep(a_re, a_im, log_dt, b_re_t, b_im_t):
    gp = jax.ShapeDtypeStruct(a_re.shape, F32)
    gcp = jax.ShapeDtypeStruct(b_re_t.shape, F32)
    return pl.pallas_call(_ssm_prep_kernel, out_shape=[gp, gp, gcp, gcp], name="ssm_prep")(
        a_re, a_im, log_dt, b_re_t, b_im_t)


def _ssm_kernel(u_ref, wb_ref, lam_re_ref, lam_im_ref, wc_ref, d_ref, wglu_ref, y_ref, st_ref, xr_ref, xi_ref,
                *, batch):
    n_st = SSM_STATES_ALL
    steps = u_ref.shape[0] // batch

    @pl.when(pl.program_id(0) == 0)
    def _():
        xr_ref[...] = jnp.zeros(xr_ref.shape, F32)
        xi_ref[...] = jnp.zeros(xi_ref.shape, F32)

    rows = u_ref.shape[0]
    halves = [slice(h * rows // 2, (h + 1) * rows // 2) for h in range(2)]
    for sl in halves:
        st_ref[sl, :] = _dot(u_ref[sl, :], wb_ref[...])

    def step(t, carry):
        xr, xi = carry
        r0 = pl.multiple_of(t * batch, batch)
        lr = lam_re_ref[...]
        li = lam_im_ref[...]
        nr = lr * xr - li * xi + st_ref[pl.ds(r0, batch), 0:n_st]
        ni = lr * xi + li * xr + st_ref[pl.ds(r0, batch), n_st:2 * n_st]
        st_ref[pl.ds(r0, batch), 0:n_st] = nr
        st_ref[pl.ds(r0, batch), n_st:2 * n_st] = ni
        return nr, ni

    xr, xi = lax.fori_loop(0, steps, step, (xr_ref[...], xi_ref[...]))
    xr_ref[...] = xr
    xi_ref[...] = xi

    ys = [_dot(st_ref[sl, 0:n_st].astype(BF16), wc_ref[0:n_st, :])
          + _dot(st_ref[sl, n_st:2 * n_st].astype(BF16), wc_ref[n_st:2 * n_st, :]) for sl in halves]
    for sl, y in zip(halves, ys):
        y = y + d_ref[...] * u_ref[sl, :].astype(F32)
        z = _dot(_gelu_tanh(y).astype(BF16), wglu_ref[...])
        y_ref[sl, :] = (z[:, 0:SSM_WIDTH] * _sigmoid(z[:, SSM_WIDTH:2 * SSM_WIDTH])).astype(BF16)


def _ssm(u_tb, wb, lam_re_b, lam_im_b, wc, d_skip, w_glu, batch, seq):
    rows = SSM_LC * batch
    full = lambda a: pl.BlockSpec(a.shape, lambda i: (0,) * a.ndim)
    return pl.pallas_call(
        functools.partial(_ssm_kernel, batch=batch),
        grid=(seq // SSM_LC,),
        in_specs=[pl.BlockSpec((rows, SSM_WIDTH), lambda i: (i, 0)),
                  full(wb), full(lam_re_b), full(lam_im_b), full(wc), full(d_skip), full(w_glu)],
        out_specs=pl.BlockSpec((rows, SSM_WIDTH), lambda i: (i, 0)),
        out_shape=jax.ShapeDtypeStruct((seq * batch, SSM_WIDTH), BF16),
        scratch_shapes=[pltpu.VMEM((rows, 2 * SSM_STATES_ALL), F32),
                        pltpu.VMEM((batch, SSM_STATES_ALL), F32),
                        pltpu.VMEM((batch, SSM_STATES_ALL), F32)],
        compiler_params=pltpu.CompilerParams(dimension_semantics=("arbitrary",), vmem_limit_bytes=VMEM_LIMIT),
        name="ssm",
    )(u_tb, wb, lam_re_b, lam_im_b, wc, d_skip, w_glu)


def _conv_kernel(u_ref, wdw_ref, bdw_ref, lng_ref, lnb_ref, wpw_ref, bpw_ref, y_ref, pad_ref):
    seq = u_ref.shape[0]
    pad_ref[0:CONV_PAD, :] = jnp.zeros((CONV_PAD, CONV_WIDTH), F32)
    pad_ref[CONV_PAD:CONV_PAD + seq, :] = u_ref[...]
    wdw = wdw_ref[...]

    def tile(i):
        t0 = pl.multiple_of(i * CONV_TC, CONV_TC)
        win_rows = CONV_TC + CONV_PAD
        win = pad_ref[pl.ds(t0, win_rows), :]
        acc = jnp.zeros((CONV_TC, CONV_WIDTH), F32)
        for b in range(SUBLANES):
            rolled = win if b == 0 else pltpu.roll(win, win_rows - b, 0)
            for a in range(win_rows // SUBLANES):
                k = a * SUBLANES + b - (CONV_PAD - (CONV_K - 1))
                if 0 <= k < CONV_K:
                    acc = acc + wdw[k:k + 1, :] * rolled[a * SUBLANES:a * SUBLANES + CONV_TC]
        y = acc + bdw_ref[...]
        mu = jnp.mean(y, axis=-1, keepdims=True)
        yc = y - mu
        var = jnp.mean(yc * yc, axis=-1, keepdims=True)
        yn = yc * lax.rsqrt(var + EPS) * lng_ref[...] + lnb_ref[...]
        act = yn * _sigmoid(yn)
        y_ref[pl.ds(t0, CONV_TC), :] = (_dot(act.astype(BF16), wpw_ref[...]) + bpw_ref[...]).astype(BF16)

    def tile_group(i, carry):
        for j in range(CONV_UNROLL):
            tile(CONV_UNROLL * i + j)
        return carry

    lax.fori_loop(0, seq // (CONV_UNROLL * CONV_TC), tile_group, 0)


def _conv(u, wdw, bdw, lng, lnb, wpw, bpw, batch, seq):
    full = lambda a: pl.BlockSpec(a.shape, lambda b: (0,) * a.ndim)
    return pl.pallas_call(
        _conv_kernel,
        grid=(batch,),
        in_specs=[pl.BlockSpec((seq, CONV_WIDTH), lambda b: (b, 0)),
                  full(wdw), full(bdw), full(lng), full(lnb), full(wpw), full(bpw)],
        out_specs=pl.BlockSpec((seq, CONV_WIDTH), lambda b: (b, 0)),
        out_shape=jax.ShapeDtypeStruct((batch * seq, CONV_WIDTH), BF16),
        scratch_shapes=[pltpu.VMEM((CONV_PAD + seq, CONV_WIDTH), F32)],
        compiler_params=pltpu.CompilerParams(dimension_semantics=("parallel",), vmem_limit_bytes=VMEM_LIMIT),
        name="conv",
    )(u, wdw, bdw, lng, lnb, wpw, bpw)


def _out_ffn_kernel(x_ref, ya_ref, ys_ref, yc_ref, ga_ref, gs_ref, gc_ref, wo_ref, gf_ref,
                    wg_ref, wu_ref, wd_ref, gfin_ref, o_ref, *, final_norm):
    na = _rms(ya_ref[...].astype(F32), ga_ref[...]).astype(BF16)
    ns = _rms(ys_ref[...].astype(F32), gs_ref[...]).astype(BF16)
    nc = _rms(yc_ref[...].astype(F32), gc_ref[...]).astype(BF16)
    o1, o2 = ATTN_WIDTH, ATTN_WIDTH + SSM_WIDTH
    x = x_ref[...] + _dot(na, wo_ref[0:o1, :]) + _dot(ns, wo_ref[o1:o2, :]) + _dot(nc, wo_ref[o2:, :])
    o_ref[...] = x
    h = _rms(x, gf_ref[...]).astype(BF16)
    gate = _dot(h, wg_ref[...])
    up = _dot(h, wu_ref[...])
    x = o_ref[...] + _dot((gate * _sigmoid(gate) * up).astype(BF16), wd_ref[...])
    if final_norm:
        x = _rms(x, gfin_ref[...])
    o_ref[...] = x


def _out_ffn(x2d, y_attn, y_ssm_tm, y_conv, ga, gs, gc, wo, gf, wg, wu, wd, gfin, batch, seq, final_norm):
    tm = TM_PROJ
    tiles_per_seq = seq // tm
    tokens = batch * seq
    row = lambda n: pl.BlockSpec((tm, n), lambda i: (i, 0))
    resident = lambda a: pl.BlockSpec(a.shape, lambda i: (0,) * a.ndim)
    return pl.pallas_call(
        functools.partial(_out_ffn_kernel, final_norm=final_norm),
        grid=(tokens // tm,),
        in_specs=[row(D_MODEL), row(ATTN_WIDTH),
                  pl.BlockSpec((tm, SSM_WIDTH), lambda i: (i % tiles_per_seq, i // tiles_per_seq)),
                  row(CONV_WIDTH),
                  resident(ga), resident(gs), resident(gc), resident(wo), resident(gf),
                  resident(wg), resident(wu), resident(wd), resident(gfin)],
        out_specs=row(D_MODEL),
        out_shape=jax.ShapeDtypeStruct((tokens, D_MODEL), F32),
        compiler_params=pltpu.CompilerParams(dimension_semantics=("parallel",), vmem_limit_bytes=VMEM_LIMIT),
        name="out_ffn",
    )(x2d, y_attn, y_ssm_tm, y_conv, ga, gs, gc, wo, gf, wg, wu, wd, gfin)


def _rope_tables(pos):
    half = HEAD_DIM // 2
    freqs = ROPE_THETA ** (-jnp.arange(half, dtype=F32) / half)
    ang = pos.astype(F32)[:, None] * freqs[None, :]
    cos = jnp.tile(jnp.cos(ang), (1, LANES // half))
    sin = jnp.sin(ang)
    sin_signed = jnp.tile(jnp.concatenate([-sin, sin], axis=1), (1, LANES // HEAD_DIM))
    return cos, sin_signed


def _pack_w_in(w_in):
    sizes = (ATTN_WIDTH,) + (KV_WIDTH,) * 6 + (N_BRANCH * N_Q_HEADS, SSM_WIDTH, 2 * CONV_WIDTH)
    parts = jnp.split(w_in, np.cumsum(sizes)[:-1].tolist(), axis=1)
    gate = jnp.pad(parts[7], ((0, 0), (0, GATE_PAD - N_BRANCH * N_Q_HEADS)))
    return jnp.concatenate(parts[:7] + parts[8:] + [gate], axis=1).astype(BF16)


def _pack_cmp_w1(w1):
    w = w1.reshape(2, CMP_STRIDE, HEAD_DIM, CMP_HIDDEN)
    eye = jnp.eye(N_KV_HEADS, dtype=w1.dtype)
    big = jnp.einsum('rpdh,ab->padrbh', w, eye)
    return big.reshape(CMP_STRIDE * N_KV_HEADS * HEAD_DIM, 2 * N_KV_HEADS * CMP_HIDDEN).astype(BF16)


def _pe_operand(pe):
    lo = jnp.tile(pe[:CMP_STRIDE, None, :], (1, N_KV_HEADS, 1)).reshape(1, -1)
    hi = jnp.tile(pe[CMP_STRIDE:, None, :], (1, N_KV_HEADS, 1)).reshape(1, -1)
    return jnp.concatenate([jnp.tile(lo, (4, 1)), jnp.tile(hi, (4, 1))], axis=0).astype(BF16)


def _block_diag(blocks):
    g, a, b = blocks.shape
    eye = jnp.eye(g, dtype=blocks.dtype)
    return jnp.einsum('gab,gh->gahb', blocks, eye).reshape(g * a, g * b)


def _overlap_matrix(seq):
    n_cmp = seq // CMP_STRIDE
    n_sel = seq // SLC_BLOCK
    ci = np.arange(n_cmp)[None, :]
    sj = np.arange(n_sel)[:, None]
    ovl = (ci * CMP_STRIDE < (sj + 1) * SLC_BLOCK) & (ci * CMP_STRIDE + CMP_LEN > sj * SLC_BLOCK) & (ci < n_cmp - 1)
    return jnp.asarray(ovl.astype(np.float32), dtype=BF16)


def _gate_spread_matrix():
    e = np.zeros((N_BRANCH, N_KV_HEADS, GATE_PAD, GQA_GROUP * LANES), np.float32)
    for br in range(N_BRANCH):
        for g in range(N_KV_HEADS):
            for r in range(GQA_GROUP):
                e[br, g, br * N_Q_HEADS + g * GQA_GROUP + r, r * LANES:(r + 1) * LANES] = 1.0
    return jnp.asarray(e, dtype=BF16)


def _layer(x2d, p, batch, seq, tables, final_norm, norm_final):
    cos, sin, cos_c, sin_c, ovl, gexp = tables
    row = lambda v: v.reshape(1, -1).astype(F32)

    q, kc_raw, vc_raw, ksa, vsa, kwa, vwa, gates, u_ssm, u_conv = _in_proj(
        x2d, row(p['norm_mix']), _pack_w_in(p['w_in']), cos, sin, batch, seq)

    n_half = seq // CMP_STRIDE
    kh = kc_raw.reshape(batch, n_half, CMP_STRIDE * KV_WIDTH)
    vh = vc_raw.reshape(batch, n_half, CMP_STRIDE * KV_WIDTH)

    wk = _pack_cmp_w1(p['cmp_k_w1'])
    wv = _pack_cmp_w1(p['cmp_v_w1'])
    pek = _pe_operand(p['cmp_pe_k'])
    pev = _pe_operand(p['cmp_pe_v'])
    kca, vca = _compress(kh, vh, wk, wv, pek, pev, p['cmp_k_w2'].astype(BF16), p['cmp_v_w2'].astype(BF16),
                       cos_c, sin_c)

    y_attn = _attention(q, kca, vca, ksa, vsa, kwa, vwa, gates, ovl, gexp, batch, seq)

    lam_re, lam_im, bb_re, bb_im = _ssm_prep(
        p['ssm_a_re'], p['ssm_a_im'], p['ssm_log_dt'].reshape(-1, 1),
        p['ssm_b_re'].transpose(0, 2, 1), p['ssm_b_im'].transpose(0, 2, 1))
    wb = jnp.concatenate([_block_diag(bb_re), _block_diag(bb_im)], axis=1).astype(BF16)
    wc = jnp.concatenate([_block_diag(p['ssm_c_re'].transpose(0, 2, 1)),
                          -_block_diag(p['ssm_c_im'].transpose(0, 2, 1))], axis=0).astype(BF16)
    lam_re_b = jnp.broadcast_to(lam_re.reshape(1, -1), (batch, SSM_STATES_ALL))
    lam_im_b = jnp.broadcast_to(lam_im.reshape(1, -1), (batch, SSM_STATES_ALL))
    y_ssm = _ssm(u_ssm.reshape(seq * batch, SSM_WIDTH), wb, lam_re_b, lam_im_b, wc, row(p['ssm_d']),
                 p['ssm_w_glu'].astype(BF16), batch, seq)
    y_ssm_tm = y_ssm.reshape(seq, batch * SSM_WIDTH)

    wdw = jnp.pad(p['conv_w_dw'], ((0, CONV_PAD - CONV_K), (0, 0)))
    y_conv = _conv(u_conv, wdw, row(p['conv_b_dw']), row(p['conv_ln_g']), row(p['conv_ln_b']),
                   p['conv_w_pw'].astype(BF16), row(p['conv_b_pw']), batch, seq)

    return _out_ffn(x2d, y_attn, y_ssm_tm, y_conv, row(p['norm_out_attn']), row(p['norm_out_ssm']),
                    row(p['norm_out_conv']), p['w_out'].astype(BF16), row(p['norm_ffn']),
                    p['w_gate'].astype(BF16), p['w_up'].astype(BF16), p['w_down'].astype(BF16),
                    row(norm_final), batch, seq, final_norm)


_LAYER_KEYS = ('norm_mix', 'w_in', 'cmp_pe_k', 'cmp_k_w1', 'cmp_k_w2', 'cmp_pe_v', 'cmp_v_w1', 'cmp_v_w2',
               'ssm_a_re', 'ssm_a_im', 'ssm_log_dt', 'ssm_b_re', 'ssm_b_im', 'ssm_c_re', 'ssm_c_im', 'ssm_d',
               'ssm_w_glu', 'conv_w_dw', 'conv_b_dw', 'conv_ln_g', 'conv_ln_b', 'conv_w_pw', 'conv_b_pw',
               'norm_out_attn', 'norm_out_ssm', 'norm_out_conv', 'w_out', 'norm_ffn', 'w_gate', 'w_up', 'w_down')


def kernel(x, norm_mix, w_in, cmp_pe_k, cmp_k_w1, cmp_k_w2, cmp_pe_v, cmp_v_w1, cmp_v_w2, ssm_a_re, ssm_a_im, ssm_log_dt, ssm_b_re, ssm_b_im, ssm_c_re, ssm_c_im, ssm_d, ssm_w_glu, conv_w_dw, conv_b_dw, conv_ln_g, conv_ln_b, conv_w_pw, conv_b_pw, norm_out_attn, norm_out_ssm, norm_out_conv, w_out, norm_ffn, w_gate, w_up, w_down, norm_final):
    stacked = dict(zip(_LAYER_KEYS, (norm_mix, w_in, cmp_pe_k, cmp_k_w1, cmp_k_w2, cmp_pe_v, cmp_v_w1, cmp_v_w2,
                                     ssm_a_re, ssm_a_im, ssm_log_dt, ssm_b_re, ssm_b_im, ssm_c_re, ssm_c_im, ssm_d,
                                     ssm_w_glu, conv_w_dw, conv_b_dw, conv_ln_g, conv_ln_b, conv_w_pw, conv_b_pw,
                                     norm_out_attn, norm_out_ssm, norm_out_conv, w_out, norm_ffn, w_gate, w_up,
                                     w_down)))
    batch, seq, _ = x.shape
    depth = norm_mix.shape[0]
    cos, sin = _rope_tables(jnp.arange(seq))
    cos_c, sin_c = _rope_tables(jnp.arange(seq // CMP_STRIDE) * CMP_STRIDE + CMP_LEN - 1)
    tables = (cos, sin, cos_c, sin_c, _overlap_matrix(seq), _gate_spread_matrix())
    x2d = x.reshape(batch * seq, D_MODEL)
    for l in range(depth):
        p = {k: v[l] for k, v in stacked.items()}
        x2d = _layer(x2d, p, batch, seq, tables, l == depth - 1, norm_final)
    return x2d.reshape(batch, seq, D_MODEL)
```

```python
import functools
import math

import numpy as np
import jax
import jax.numpy as jnp
from jax import lax
from jax.experimental import pallas as pl
from jax.experimental.pallas import tpu as pltpu

F32 = jnp.float32
BF16 = jnp.bfloat16

D_MODEL = 1024
HEAD_DIM = 64
N_Q_HEADS = 8
N_KV_HEADS = 2
GQA_GROUP = N_Q_HEADS // N_KV_HEADS
ATTN_WIDTH = N_Q_HEADS * HEAD_DIM
KV_WIDTH = N_KV_HEADS * HEAD_DIM
N_BRANCH = 3
SSM_WIDTH = D_MODEL // 4
SSM_GROUP = 16
SSM_N_GROUPS = SSM_WIDTH // SSM_GROUP
SSM_STATE = 64
SSM_STATES_ALL = SSM_N_GROUPS * SSM_STATE
CONV_WIDTH = D_MODEL // 4
CONV_K = 31
CMP_LEN = 32
CMP_STRIDE = 16
CMP_HIDDEN = 256
SLC_BLOCK = 64
SLC_TOPN = 16
WINDOW = 512
ROPE_THETA = 10000.0
D_FF = ((8 * D_MODEL // 3 + 255) // 256) * 256
EPS = 1e-6
NEG_INF = -1e30
Q_SCALE = HEAD_DIM ** -0.5 * math.log2(math.e)
FORCE = 1e9

LANES = 128
SUBLANES = 8
VMEM_LIMIT = 56 * 1024 * 1024

GATE_PAD = LANES
PROJ_BLOCK = 512
C_Q = 0
C_KC = C_Q + ATTN_WIDTH
C_VC = C_KC + KV_WIDTH
C_KS = C_VC + KV_WIDTH
C_VS = C_KS + KV_WIDTH
C_KW = C_VS + KV_WIDTH
C_VW = C_KW + KV_WIDTH
C_SSM = C_VW + KV_WIDTH
C_CONV = C_SSM + SSM_WIDTH
C_GATE = C_CONV + 2 * CONV_WIDTH
IN_PACKED = C_GATE + GATE_PAD

TM_PROJ = 512
TM_IN = 1024
TQ = 256
CK = 1024
WIN_PIECE = 256
RANK_GROUP = 16
SSM_LC = 128
CONV_TC = 128
CONV_UNROLL = 4
CONV_PAD = 32


def _gelu_tanh(x):
    return 0.5 * x * (1.0 + jnp.tanh(math.sqrt(2.0 / math.pi) * (x + 0.044715 * (x * x * x))))


def _sigmoid(x):
    return 1.0 / (1.0 + jnp.exp(-x))


def _rms(x, g):
    return x * lax.rsqrt(jnp.mean(x * x, axis=-1, keepdims=True) + EPS) * g


def _dot(a, b):
    return jnp.dot(a, b, preferred_element_type=F32)


def _dot_nt(a, b):
    return lax.dot_general(a, b, (((1,), (1,)), ((), ())), preferred_element_type=F32)


def _rope_rotate(v, cos, sin_signed):
    lane = lax.broadcasted_iota(jnp.int32, v.shape, 1)
    first_half = (lane % HEAD_DIM) < (HEAD_DIM // 2)
    rot = jnp.where(first_half, pltpu.roll(v, LANES - HEAD_DIM // 2, 1), pltpu.roll(v, HEAD_DIM // 2, 1))
    return v * cos + rot * sin_signed


def _in_proj_kernel(x_ref, g_ref, w_ref, cos_ref, sin_ref,
                    q_ref, kc_ref, vc_ref, ksa_ref, vsa_ref, kwa_ref, vwa_ref, gate_ref, ussm_ref, uconv_ref,
                    *, tiles_per_seq):
    tm = x_ref.shape[0]
    h = _rms(x_ref[...], g_ref[...]).astype(BF16)
    cos = cos_ref[...]
    sin = sin_ref[...]

    blocks = {}

    def proj(c0, n):
        b0 = (c0 // PROJ_BLOCK) * PROJ_BLOCK
        if b0 not in blocks:
            blocks[b0] = _dot(h, w_ref[:, b0:min(b0 + PROJ_BLOCK, IN_PACKED)])
        return blocks[b0][:, c0 - b0:c0 - b0 + n]

    lane = lax.broadcasted_iota(jnp.int32, (tm, LANES), 1)
    row = lax.broadcasted_iota(jnp.int32, (tm, LANES), 0)
    t = (pl.program_id(0) % tiles_per_seq) * tm + row
    low = lane < HEAD_DIM
    for j in range(ATTN_WIDTH // LANES):
        qj = _rope_rotate(proj(C_Q + j * LANES, LANES), cos, sin) * Q_SCALE
        q_ref[:, (2 * j) * LANES:(2 * j + 1) * LANES] = jnp.where(low, qj, 0.0).astype(BF16)
        q_ref[:, (2 * j + 1) * LANES:(2 * j + 2) * LANES] = jnp.where(low, pltpu.roll(qj, HEAD_DIM, 1), 0.0).astype(BF16)
    kc_ref[...] = proj(C_KC, KV_WIDTH).astype(BF16)
    vc_ref[...] = proj(C_VC, KV_WIDTH).astype(BF16)
    blk_onehot = jnp.where((t // SLC_BLOCK) == (lane - HEAD_DIM), 1.0, 0.0)
    ones_up = jnp.where(low, 0.0, 1.0)

    def split_groups(v, upper, ref):
        ref[0, 0] = jnp.where(low, v, upper).astype(BF16)
        ref[0, 1] = jnp.where(low, pltpu.roll(v, HEAD_DIM, 1), upper).astype(BF16)

    split_groups(_rope_rotate(proj(C_KS, KV_WIDTH), cos, sin), blk_onehot, ksa_ref)
    split_groups(proj(C_VS, KV_WIDTH), ones_up, vsa_ref)
    split_groups(_rope_rotate(proj(C_KW, KV_WIDTH), cos, sin), 0.0, kwa_ref)
    split_groups(proj(C_VW, KV_WIDTH), ones_up, vwa_ref)

    gate_ref[...] = _sigmoid(proj(C_GATE, GATE_PAD))
    ussm_ref[...] = proj(C_SSM, SSM_WIDTH).astype(BF16)
    a = proj(C_CONV, CONV_WIDTH)
    g = proj(C_CONV + CONV_WIDTH, CONV_WIDTH)
    uconv_ref[...] = a * _sigmoid(g)


def _in_proj(x2d, g, w_packed, cos, sin, batch, seq):
    tm = TM_IN
    tiles_per_seq = seq // tm
    tokens = batch * seq
    row_blk = lambda n, dt: (pl.BlockSpec((tm, n), lambda i: (i, 0)), jax.ShapeDtypeStruct((tokens, n), dt))
    grp_blk = (pl.BlockSpec((1, N_KV_HEADS, tm, LANES), lambda i: (i // tiles_per_seq, 0, i % tiles_per_seq, 0)),
               jax.ShapeDtypeStruct((batch, N_KV_HEADS, seq, LANES), BF16))
    outs = [row_blk(N_Q_HEADS * LANES, BF16), row_blk(KV_WIDTH, BF16), row_blk(KV_WIDTH, BF16),
            grp_blk, grp_blk, grp_blk, grp_blk, row_blk(GATE_PAD, F32),
            (pl.BlockSpec((tm, SSM_WIDTH), lambda i: (i % tiles_per_seq, i // tiles_per_seq)),
             jax.ShapeDtypeStruct((seq, batch * SSM_WIDTH), BF16)),
            row_blk(CONV_WIDTH, F32)]
    return pl.pallas_call(
        functools.partial(_in_proj_kernel, tiles_per_seq=tiles_per_seq),
        grid=(tokens // tm,),
        in_specs=[pl.BlockSpec((tm, D_MODEL), lambda i: (i, 0)),
                  pl.BlockSpec((1, D_MODEL), lambda i: (0, 0)),
                  pl.BlockSpec((D_MODEL, IN_PACKED), lambda i: (0, 0)),
                  pl.BlockSpec((tm, LANES), lambda i: (i % tiles_per_seq, 0)),
                  pl.BlockSpec((tm, LANES), lambda i: (i % tiles_per_seq, 0))],
        out_specs=[o[0] for o in outs],
        out_shape=[o[1] for o in outs],
        compiler_params=pltpu.CompilerParams(dimension_semantics=("parallel",), vmem_limit_bytes=VMEM_LIMIT),
        name="in_proj",
    )(x2d, g, w_packed, cos, sin)


def _compress_kernel(kh_ref, vh_ref, wk_ref, wv_ref, pek_ref, pev_ref, w2k_ref, w2v_ref, cos_ref, sin_ref,
                     kca_ref, vca_ref):
    n_half = kh_ref.shape[1]
    hw = N_KV_HEADS * CMP_HIDDEN

    def compress(h_ref, w_ref, pe_ref, w2_ref):
        ab = _dot(h_ref[0], w_ref[...])
        pe = _dot(pe_ref[...], w_ref[...])
        bias = pe[0:1, 0:hw] + pe[4:5, hw:2 * hw]
        nxt = pltpu.roll(ab[:, hw:2 * hw], n_half - 1, 0)
        hid = _gelu_tanh(ab[:, 0:hw] + nxt + bias).astype(BF16)
        return jnp.concatenate([_dot(hid[:, j * CMP_HIDDEN:(j + 1) * CMP_HIDDEN], w2_ref[...])
                                for j in range(N_KV_HEADS)], axis=1)

    def split_groups(v, upper, ref):
        low = lax.broadcasted_iota(jnp.int32, v.shape, 1) < HEAD_DIM
        ref[0, 0] = jnp.where(low, v, upper).astype(BF16)
        ref[0, 1] = jnp.where(low, pltpu.roll(v, HEAD_DIM, 1), upper).astype(BF16)

    kc = compress(kh_ref, wk_ref, pek_ref, w2k_ref)
    split_groups(_rope_rotate(kc, cos_ref[...], sin_ref[...]), 0.0, kca_ref)
    split_groups(compress(vh_ref, wv_ref, pev_ref, w2v_ref), 1.0, vca_ref)


def _compress(kh, vh, wk, wv, pek, pev, w2k, w2v, cos_c, sin_c):
    batch, n_half, width = kh.shape
    full = lambda a: pl.BlockSpec(a.shape, lambda b: (0,) * a.ndim)
    per_b = pl.BlockSpec((1, n_half, width), lambda b: (b, 0, 0))
    out_blk = pl.BlockSpec((1, N_KV_HEADS, n_half, LANES), lambda b: (b, 0, 0, 0))
    out_shape = jax.ShapeDtypeStruct((batch, N_KV_HEADS, n_half, LANES), BF16)
    return pl.pallas_call(
        _compress_kernel,
        grid=(batch,),
        in_specs=[per_b, per_b, full(wk), full(wv), full(pek), full(pev), full(w2k), full(w2v),
                  full(cos_c), full(sin_c)],
        out_specs=[out_blk, out_blk],
        out_shape=[out_shape, out_shape],
        compiler_params=pltpu.CompilerParams(dimension_semantics=("parallel",), vmem_limit_bytes=VMEM_LIMIT),
        name="compress",
    )(kh, vh, wk, wv, pek, pev, w2k, w2v, cos_c, sin_c)


def _attention_kernel(q_ref, kca_ref, vca_ref, ksa_ref, vsa_ref, kwa_ref, vwa_ref, gate_ref, ovl_ref, gexp_ref,
                      y_ref,
                      qaug_ref, oc_ref, s0_ref, s1_ref, mx_ref, m_ref, acc_ref, val_ref, rank_ref):
    tq = q_ref.shape[0]
    rows = GQA_GROUP * tq
    n_cmp = kca_ref.shape[2]
    seq = ksa_ref.shape[2]
    n_sel = seq // SLC_BLOCK
    t0 = pl.program_id(1) * tq
    row_q = lax.broadcasted_iota(jnp.int32, (rows, 1), 0) % tq
    t_row = t0 + row_q
    t_rep = t0 + lax.broadcasted_iota(jnp.int32, (rows, LANES), 0) % tq
    lane_iota = lax.broadcasted_iota(jnp.int32, (rows, LANES), 1)
    diag_rep = t_rep - lane_iota
    cmp_rep = t_rep - lane_iota * CMP_STRIDE - (CMP_LEN - 1)
    last_blk = (t0 + tq - 1) // SLC_BLOCK

    def row_max_rep(x):
        return jnp.broadcast_to(jnp.max(x, axis=-1, keepdims=True), x.shape)

    def sums_on_all_lanes(a):
        return jnp.where(lane_iota < HEAD_DIM, pltpu.roll(a, HEAD_DIM, 1), a)

    def group_queries(g):
        heads = [q_ref[:, (g * GQA_GROUP + r) * LANES:(g * GQA_GROUP + r + 1) * LANES] for r in range(GQA_GROUP)]
        return jnp.concatenate(heads, axis=0)

    blk = lax.broadcasted_iota(jnp.int32, (n_sel, tq), 0)
    cur = (t0 + lax.broadcasted_iota(jnp.int32, (n_sel, tq), 1)) // SLC_BLOCK
    causal = blk <= cur

    def compressed_branch(n_slabs):
        n_used = n_slabs * LANES
        half = rows // 2
        heads_per_half = GQA_GROUP // 2
        lane_half = lax.broadcasted_iota(jnp.int32, (half, LANES), 1)
        t_half = t0 + lax.broadcasted_iota(jnp.int32, (half, LANES), 0) % tq
        cmp_half = t_half - lane_half * CMP_STRIDE - (CMP_LEN - 1)
        low_half = lane_half < HEAD_DIM
        for g in range(N_KV_HEADS):
            pc_sum = None
            for h in range(2):
                rs = slice(h * half, (h + 1) * half)
                q_half = jnp.concatenate(
                    [q_ref[:, (g * GQA_GROUP + h * heads_per_half + r) * LANES:
                              (g * GQA_GROUP + h * heads_per_half + r + 1) * LANES] for r in range(heads_per_half)],
                    axis=0)
                sc = _dot_nt(q_half, kca_ref[0, g, 0:n_used, :])
                slabs = []
                for j in range(n_slabs):
                    slabs.append(jnp.where(cmp_half >= j * LANES * CMP_STRIDE,
                                           sc[:, j * LANES:(j + 1) * LANES], NEG_INF))
                mc = slabs[0]
                for sj in slabs[1:]:
                    mc = jnp.maximum(mc, sj)
                mc = row_max_rep(mc)
                pc = jnp.concatenate([jnp.exp2(sj - mc) for sj in slabs], axis=1)
                oc = _dot(pc.astype(BF16), vca_ref[0, g, 0:n_used, :])
                sums = jnp.where(low_half, pltpu.roll(oc, HEAD_DIM, 1), oc)
                inv = jnp.where(t_half >= CMP_LEN - 1, 1.0 / jnp.maximum(sums, 1e-30), 0.0)
                oc_ref[g, rs, :] = oc * inv
                pc = pc * jnp.concatenate([inv] * n_slabs, axis=1)
                for r in range(heads_per_half):
                    part = pc[r * tq:(r + 1) * tq]
                    pc_sum = part if pc_sum is None else pc_sum + part
            p_hi = pc_sum.astype(BF16)
            rem = pc_sum - p_hi.astype(F32)
            p_mid = rem.astype(BF16)
            p_lo = (rem - p_mid.astype(F32)).astype(BF16)
            ovl = ovl_ref[:, 0:n_used]
            imp = _dot_nt(ovl, p_hi) + _dot_nt(ovl, p_mid) + _dot_nt(ovl, p_lo)
            val = jnp.where(blk == 0, FORCE, jnp.where(blk >= cur - 1, FORCE, imp))
            val_ref[:, g * tq:(g + 1) * tq] = jnp.where(causal, val, -FORCE)

    compressed_branch(n_cmp // LANES)

    val = val_ref[...]
    rank_ref[...] = jnp.zeros(rank_ref.shape, jnp.int32)
    grp_rows = lax.broadcasted_iota(jnp.int32, (RANK_GROUP, N_KV_HEADS * tq), 0)
    for ig in range(n_sel // RANK_GROUP):
        for jg in range(n_sel // RANK_GROUP):
            @pl.when(max(ig, jg) * RANK_GROUP <= last_blk)
            def _(ig=ig, jg=jg):
                vj = val[jg * RANK_GROUP:(jg + 1) * RANK_GROUP]
                cnt = jnp.zeros(vj.shape, jnp.int32)
                for ii in range(RANK_GROUP):
                    i = ig * RANK_GROUP + ii
                    vi = val[i:i + 1, :]
                    if jg > ig:
                        cnt = cnt + jnp.where(vi >= vj, 1, 0)
                    elif jg < ig:
                        cnt = cnt + jnp.where(vi > vj, 1, 0)
                    else:
                        tie = jnp.where(grp_rows > ii, 1, 0)
                        cnt = cnt + jnp.where(vi > vj, 1, jnp.where(vi == vj, tie, 0))
                sl = pl.ds(jg * RANK_GROUP, RANK_GROUP)
                rank_ref[sl, :] = rank_ref[sl, :] + cnt

    for g in range(N_KV_HEADS):
        rank = rank_ref[:, g * tq:(g + 1) * tq]
        bias_t = jnp.where(causal, jnp.where(rank < SLC_TOPN, 0.0, NEG_INF), NEG_INF)
        pieces = [jnp.zeros((HEAD_DIM, tq), F32), bias_t]
        if n_sel < LANES - HEAD_DIM:
            pieces.append(jnp.zeros((LANES - HEAD_DIM - n_sel, tq), F32))
        bias = jnp.concatenate(pieces, axis=0).T.astype(BF16)
        qaug_ref[g] = group_queries(g) + jnp.concatenate([bias] * GQA_GROUP, axis=0)

    n_chunks = (t0 + tq + CK - 1) // CK
    last = n_chunks - 1
    mx_ref[...] = jnp.full(mx_ref.shape, NEG_INF, F32)
    acc_ref[...] = jnp.zeros(acc_ref.shape, F32)
    s_refs = (s0_ref, s1_ref)

    def chunk_keys(ref, g, c, width=CK):
        return ref[0, g, pl.ds(pl.multiple_of(c * CK, CK), width), :]

    def score_chunk(g, c, diagonal, width=CK):
        s = _dot_nt(qaug_ref[g], chunk_keys(ksa_ref, g, c, width))
        mx = mx_ref[g]
        slabs = []
        for j in range(width // LANES):
            sj = s[:, j * LANES:(j + 1) * LANES]
            if diagonal:
                sj = jnp.where(diag_rep >= c * CK + j * LANES, sj, NEG_INF)
            mx = jnp.maximum(mx, sj)
            slabs.append(sj)
        s_refs[g][c, :, 0:width] = jnp.concatenate(slabs, axis=1)
        mx_ref[g] = mx

    def value_chunk(g, c, width=CK):
        s = s_refs[g][c, :, 0:width]
        m = m_ref[g]
        p = jnp.concatenate([jnp.exp2(s[:, j * LANES:(j + 1) * LANES] - m) for j in range(width // LANES)], axis=1)
        acc_ref[g] = acc_ref[g] + _dot(p.astype(BF16), chunk_keys(vsa_ref, g, c, width))

    last_is_half = t0 + tq - last * CK <= CK // 2

    def sweep(body, final):
        def pair(i, carry):
            body(2 * i)
            body(2 * i + 1)
            return carry

        def single(c, carry):
            body(c)
            return carry
        n_pairs = last // 2
        lax.fori_loop(0, n_pairs, pair, 0)
        lax.fori_loop(2 * n_pairs, last, single, 0)
        pl.when(last_is_half)(functools.partial(final, CK // 2))
        pl.when(jnp.logical_not(last_is_half))(functools.partial(final, CK))

    w0 = pl.multiple_of(jnp.maximum(t0 - WINDOW, 0), tq)
    n_pieces = (WINDOW + tq) // WIN_PIECE

    def stage_slot(j):
        return (j * WIN_PIECE) // CK

    def stage_lane(j):
        return (j * WIN_PIECE) % CK

    def window_branch(g, stage_ref):
        mw = None
        for j in range(n_pieces):
            sw = _dot_nt(qaug_ref[g], kwa_ref[0, g, pl.ds(w0 + j * WIN_PIECE, WIN_PIECE), :])
            slabs = []
            for jj in range(WIN_PIECE // LANES):
                slab = j * (WIN_PIECE // LANES) + jj
                k_first = w0 + slab * LANES
                sj = jnp.where(diag_rep >= k_first, sw[:, jj * LANES:(jj + 1) * LANES], NEG_INF)
                if slab < tq // LANES:
                    sj = jnp.where(diag_rep < k_first + WINDOW, sj, NEG_INF)
                mw = sj if mw is None else jnp.maximum(mw, sj)
                slabs.append(sj)
            stage_ref[stage_slot(j), :, stage_lane(j):stage_lane(j) + WIN_PIECE] = jnp.concatenate(slabs, axis=1)
        mw = row_max_rep(mw)
        aw = None
        for j in range(n_pieces):
            sw = stage_ref[stage_slot(j), :, stage_lane(j):stage_lane(j) + WIN_PIECE]
            pw = jnp.concatenate([jnp.exp2(sw[:, jj * LANES:(jj + 1) * LANES] - mw)
                                  for jj in range(WIN_PIECE // LANES)], axis=1)
            part = _dot(pw.astype(BF16), vwa_ref[0, g, pl.ds(w0 + j * WIN_PIECE, WIN_PIECE), :])
            aw = part if aw is None else aw + part
        mx_ref[g] = aw

    sweep(lambda c: score_chunk(0, c, False), lambda width: score_chunk(0, last, True, width))
    m_ref[0] = row_max_rep(mx_ref[0])

    def both(c):
        value_chunk(0, c)
        score_chunk(1, c, False)

    def both_last(width):
        value_chunk(0, last, width)
        score_chunk(1, last, True, width)

    sweep(both, both_last)
    m_ref[1] = row_max_rep(mx_ref[1])
    sweep(lambda c: value_chunk(1, c), lambda width: value_chunk(1, last, width))
    for g in range(N_KV_HEADS):
        window_branch(g, s_refs[g])

    gates = gate_ref[...]
    g_hi = gates.astype(BF16)
    g_lo = (gates - g_hi.astype(F32)).astype(BF16)
    lane = lax.broadcasted_iota(jnp.int32, (tq, LANES), 1)
    for g in range(N_KV_HEADS):
        def gate_rep(branch):
            e = gexp_ref[branch, g]
            spread = _dot(g_hi, e) + _dot(g_lo, e)
            return jnp.concatenate([spread[:, r * LANES:(r + 1) * LANES] for r in range(GQA_GROUP)], axis=0)

        asel = acc_ref[g]
        aw = mx_ref[g]
        o = (gate_rep(0) * oc_ref[g]
             + gate_rep(1) * asel / jnp.maximum(pltpu.roll(asel, HEAD_DIM, 1), 1e-30)
             + gate_rep(2) * aw / jnp.maximum(pltpu.roll(aw, HEAD_DIM, 1), 1e-30))
        for pair in range(GQA_GROUP // 2):
            lo = o[(2 * pair) * tq:(2 * pair + 1) * tq]
            hi = o[(2 * pair + 1) * tq:(2 * pair + 2) * tq]
            col = g * GQA_GROUP * HEAD_DIM + pair * LANES
            y_ref[:, col:col + LANES] = jnp.where(lane < HEAD_DIM, lo, pltpu.roll(hi, HEAD_DIM, 1)).astype(BF16)


def _attention(q, kca, vca, ksa, vsa, kwa, vwa, gates, ovl, gexp, batch, seq):
    tq = TQ
    nq = seq // tq
    n_cmp = kca.shape[2]
    rows = GQA_GROUP * tq
    grp = pl.BlockSpec((1, N_KV_HEADS, seq, LANES), lambda b, i: (b, 0, 0, 0), pipeline_mode=pl.Buffered(1))
    cmp_grp = pl.BlockSpec((1, N_KV_HEADS, n_cmp, LANES), lambda b, i: (b, 0, 0, 0))
    grp_rows = lambda dt: pltpu.VMEM((N_KV_HEADS, rows, LANES), dt)
    return pl.pallas_call(
        _attention_kernel,
        grid=(batch, nq),
        in_specs=[pl.BlockSpec((tq, N_Q_HEADS * LANES), lambda b, i: (b * nq + i, 0)),
                  cmp_grp, cmp_grp, grp, grp, grp, grp,
                  pl.BlockSpec((tq, GATE_PAD), lambda b, i: (b * nq + i, 0)),
                  pl.BlockSpec(ovl.shape, lambda b, i: (0, 0)),
                  pl.BlockSpec(gexp.shape, lambda b, i: (0, 0, 0, 0))],
        out_specs=pl.BlockSpec((tq, ATTN_WIDTH), lambda b, i: (b * nq + i, 0)),
        out_shape=jax.ShapeDtypeStruct((batch * seq, ATTN_WIDTH), BF16),
        scratch_shapes=[grp_rows(BF16),
                        grp_rows(F32),
                        pltpu.VMEM((seq // CK, rows, CK), F32),
                        pltpu.VMEM((seq // CK, rows, CK), F32),
                        grp_rows(F32),
                        grp_rows(F32),
                        grp_rows(F32),
                        pltpu.VMEM((seq // SLC_BLOCK, N_KV_HEADS * tq), F32),
                        pltpu.VMEM((seq // SLC_BLOCK, N_KV_HEADS * tq), jnp.int32)],
        compiler_params=pltpu.CompilerParams(dimension_semantics=("parallel", "parallel"),
                                             vmem_limit_bytes=VMEM_LIMIT),
        name="attention",
    )(q, kca, vca, ksa, vsa, kwa, vwa, gates, ovl, gexp)


def _ssm_prep_kernel(a_re_ref, a_im_ref, log_dt_ref, b_re_ref, b_im_ref,
                     lam_re_ref, lam_im_ref, bb_re_ref, bb_im_ref):
    a_re = a_re_ref[...]
    a_im = a_im_ref[...]
    dt = jnp.exp(log_dt_ref[...])
    mag = jnp.exp(a_re * dt)
    lam_re = mag * jnp.cos(a_im * dt)
    lam_im = mag * jnp.sin(a_im * dt)
    den = a_re * a_re + a_im * a_im
    nr, ni = lam_re - 1.0, lam_im
    f_re = (nr * a_re + ni * a_im) / den
    f_im = (ni * a_re - nr * a_im) / den
    lam_re_ref[...] = lam_re
    lam_im_ref[...] = lam_im
    b_re = b_re_ref[...]
    b_im = b_im_ref[...]
    bb_re_ref[...] = f_re[:, None, :] * b_re - f_im[:, None, :] * b_im
    bb_im_ref[...] = f_re[:, None, :] * b_im + f_im[:, None, :] * b_re


def _ssm_prep(a_re, a_im, log_dt, b_re_t, b_im_t):
    gp = jax.ShapeDtypeStruct(a_re.shape, F32)
    gcp = jax.ShapeDtypeStruct(b_re_t.shape, F32)
    return pl.pallas_call(_ssm_prep_kernel, out_shape=[gp, gp, gcp, gcp], name="ssm_prep")(
        a_re, a_im, log_dt, b_re_t, b_im_t)


def _ssm_kernel(u_ref, wb_ref, lam_re_ref, lam_im_ref, wc_ref, d_ref, wglu_ref, y_ref, st_ref, xr_ref, xi_ref,
                *, batch):
    n_st = SSM_STATES_ALL
    steps = u_ref.shape[0] // batch

    @pl.when(pl.program_id(0) == 0)
    def _():
        xr_ref[...] = jnp.zeros(xr_ref.shape, F32)
        xi_ref[...] = jnp.zeros(xi_ref.shape, F32)

    rows = u_ref.shape[0]
    halves = [slice(h * rows // 2, (h + 1) * rows // 2) for h in range(2)]
    for sl in halves:
        st_ref[sl, :] = _dot(u_ref[sl, :], wb_ref[...])

    def step(t, carry):
        xr, xi = carry
        r0 = pl.multiple_of(t * batch, batch)
        lr = lam_re_ref[...]
        li = lam_im_ref[...]
        nr = lr * xr - li * xi + st_ref[pl.ds(r0, batch), 0:n_st]
        ni = lr * xi + li * xr + st_ref[pl.ds(r0, batch), n_st:2 * n_st]
        st_ref[pl.ds(r0, batch), 0:n_st] = nr
        st_ref[pl.ds(r0, batch), n_st:2 * n_st] = ni
        return nr, ni

    xr, xi = lax.fori_loop(0, steps, step, (xr_ref[...], xi_ref[...]))
    xr_ref[...] = xr
    xi_ref[...] = xi

    ys = [_dot(st_ref[sl, 0:n_st].astype(BF16), wc_ref[0:n_st, :])
          + _dot(st_ref[sl, n_st:2 * n_st].astype(BF16), wc_ref[n_st:2 * n_st, :]) for sl in halves]
    for sl, y in zip(halves, ys):
        y = y + d_ref[...] * u_ref[sl, :].astype(F32)
        z = _dot(_gelu_tanh(y).astype(BF16), wglu_ref[...])
        y_ref[sl, :] = (z[:, 0:SSM_WIDTH] * _sigmoid(z[:, SSM_WIDTH:2 * SSM_WIDTH])).astype(BF16)


def _ssm(u_tb, wb, lam_re_b, lam_im_b, wc, d_skip, w_glu, batch, seq):
    rows = SSM_LC * batch
    full = lambda a: pl.BlockSpec(a.shape, lambda i: (0,) * a.ndim)
    return pl.pallas_call(
        functools.partial(_ssm_kernel, batch=batch),
        grid=(seq // SSM_LC,),
        in_specs=[pl.BlockSpec((rows, SSM_WIDTH), lambda i: (i, 0)),
                  full(wb), full(lam_re_b), full(lam_im_b), full(wc), full(d_skip), full(w_glu)],
        out_specs=pl.BlockSpec((rows, SSM_WIDTH), lambda i: (i, 0)),
        out_shape=jax.ShapeDtypeStruct((seq * batch, SSM_WIDTH), BF16),
        scratch_shapes=[pltpu.VMEM((rows, 2 * SSM_STATES_ALL), F32),
                        pltpu.VMEM((batch, SSM_STATES_ALL), F32),
                        pltpu.VMEM((batch, SSM_STATES_ALL), F32)],
        compiler_params=pltpu.CompilerParams(dimension_semantics=("arbitrary",), vmem_limit_bytes=VMEM_LIMIT),
        name="ssm",
    )(u_tb, wb, lam_re_b, lam_im_b, wc, d_skip, w_glu)


def _conv_kernel(u_ref, wdw_ref, bdw_ref, lng_ref, lnb_ref, wpw_ref, bpw_ref, y_ref, pad_ref):
    seq = u_ref.shape[0]
    pad_ref[0:CONV_PAD, :] = jnp.zeros((CONV_PAD, CONV_WIDTH), F32)
    pad_ref[CONV_PAD:CONV_PAD + seq, :] = u_ref[...]
    wdw = wdw_ref[...]

    def tile(i):
        t0 = pl.multiple_of(i * CONV_TC, CONV_TC)
        win_rows = CONV_TC + CONV_PAD
        win = pad_ref[pl.ds(t0, win_rows), :]
        acc = jnp.zeros((CONV_TC, CONV_WIDTH), F32)
        for b in range(SUBLANES):
            rolled = win if b == 0 else pltpu.roll(win, win_rows - b, 0)
            for a in range(win_rows // SUBLANES):
                k = a * SUBLANES + b - (CONV_PAD - (CONV_K - 1))
                if 0 <= k < CONV_K:
                    acc = acc + wdw[k:k + 1, :] * rolled[a * SUBLANES:a * SUBLANES + CONV_TC]
        y = acc + bdw_ref[...]
        mu = jnp.mean(y, axis=-1, keepdims=True)
        yc = y - mu
        var = jnp.mean(yc * yc, axis=-1, keepdims=True)
        yn = yc * lax.rsqrt(var + EPS) * lng_ref[...] + lnb_ref[...]
        act = yn * _sigmoid(yn)
        y_ref[pl.ds(t0, CONV_TC), :] = (_dot(act.astype(BF16), wpw_ref[...]) + bpw_ref[...]).astype(BF16)

    def tile_group(i, carry):
        for j in range(CONV_UNROLL):
            tile(CONV_UNROLL * i + j)
        return carry

    lax.fori_loop(0, seq // (CONV_UNROLL * CONV_TC), tile_group, 0)


def _conv(u, wdw, bdw, lng, lnb, wpw, bpw, batch, seq):
    full = lambda a: pl.BlockSpec(a.shape, lambda b: (0,) * a.ndim)
    return pl.pallas_call(
        _conv_kernel,
        grid=(batch,),
        in_specs=[pl.BlockSpec((seq, CONV_WIDTH), lambda b: (b, 0)),
                  full(wdw), full(bdw), full(lng), full(lnb), full(wpw), full(bpw)],
        out_specs=pl.BlockSpec((seq, CONV_WIDTH), lambda b: (b, 0)),
        out_shape=jax.ShapeDtypeStruct((batch * seq, CONV_WIDTH), BF16),
        scratch_shapes=[pltpu.VMEM((CONV_PAD + seq, CONV_WIDTH), F32)],
        compiler_params=pltpu.CompilerParams(dimension_semantics=("parallel",), vmem_limit_bytes=VMEM_LIMIT),
        name="conv",
    )(u, wdw, bdw, lng, lnb, wpw, bpw)


def _out_ffn_kernel(x_ref, ya_ref, ys_ref, yc_ref, ga_ref, gs_ref, gc_ref, wo_ref, gf_ref,
                    wg_ref, wu_ref, wd_ref, gfin_ref, o_ref, *, final_norm):
    na = _rms(ya_ref[...].astype(F32), ga_ref[...]).astype(BF16)
    ns = _rms(ys_ref[...].astype(F32), gs_ref[...]).astype(BF16)
    nc = _rms(yc_ref[...].astype(F32), gc_ref[...]).astype(BF16)
    o1, o2 = ATTN_WIDTH, ATTN_WIDTH + SSM_WIDTH
    x = x_ref[...] + _dot(na, wo_ref[0:o1, :]) + _dot(ns, wo_ref[o1:o2, :]) + _dot(nc, wo_ref[o2:, :])
    o_ref[...] = x
    h = _rms(x, gf_ref[...]).astype(BF16)
    gate = _dot(h, wg_ref[...])
    up = _dot(h, wu_ref[...])
    x = o_ref[...] + _dot((gate * _sigmoid(gate) * up).astype(BF16), wd_ref[...])
    if final_norm:
        x = _rms(x, gfin_ref[...])
    o_ref[...] = x


def _out_ffn(x2d, y_attn, y_ssm_tm, y_conv, ga, gs, gc, wo, gf, wg, wu, wd, gfin, batch, seq, final_norm):
    tm = TM_PROJ
    tiles_per_seq = seq // tm
    tokens = batch * seq
    row = lambda n: pl.BlockSpec((tm, n), lambda i: (i, 0))
    resident = lambda a: pl.BlockSpec(a.shape, lambda i: (0,) * a.ndim)
    return pl.pallas_call(
        functools.partial(_out_ffn_kernel, final_norm=final_norm),
        grid=(tokens // tm,),
        in_specs=[row(D_MODEL), row(ATTN_WIDTH),
                  pl.BlockSpec((tm, SSM_WIDTH), lambda i: (i % tiles_per_seq, i // tiles_per_seq)),
                  row(CONV_WIDTH),
                  resident(ga), resident(gs), resident(gc), resident(wo), resident(gf),
                  resident(wg), resident(wu), resident(wd), resident(gfin)],
        out_specs=row(D_MODEL),
        out_shape=jax.ShapeDtypeStruct((tokens, D_MODEL), F32),
        compiler_params=pltpu.CompilerParams(dimension_semantics=("parallel",), vmem_limit_bytes=VMEM_LIMIT),
        name="out_ffn",
    )(x2d, y_attn, y_ssm_tm, y_conv, ga, gs, gc, wo, gf, wg, wu, wd, gfin)


def _rope_tables(pos):
    half = HEAD_DIM // 2
    freqs = ROPE_THETA ** (-jnp.arange(half, dtype=F32) / half)
    ang = pos.astype(F32)[:, None] * freqs[None, :]
    cos = jnp.tile(jnp.cos(ang), (1, LANES // half))
    sin = jnp.sin(ang)
    sin_signed = jnp.tile(jnp.concatenate([-sin, sin], axis=1), (1, LANES // HEAD_DIM))
    return cos, sin_signed


def _pack_w_in(w_in):
    sizes = (ATTN_WIDTH,) + (KV_WIDTH,) * 6 + (N_BRANCH * N_Q_HEADS, SSM_WIDTH, 2 * CONV_WIDTH)
    parts = jnp.split(w_in, np.cumsum(sizes)[:-1].tolist(), axis=1)
    gate = jnp.pad(parts[7], ((0, 0), (0, GATE_PAD - N_BRANCH * N_Q_HEADS)))
    return jnp.concatenate(parts[:7] + parts[8:] + [gate], axis=1).astype(BF16)


def _pack_cmp_w1(w1):
    w = w1.reshape(2, CMP_STRIDE, HEAD_DIM, CMP_HIDDEN)
    eye = jnp.eye(N_KV_HEADS, dtype=w1.dtype)
    big = jnp.einsum('rpdh,ab->padrbh', w, eye)
    return big.reshape(CMP_STRIDE * N_KV_HEADS * HEAD_DIM, 2 * N_KV_HEADS * CMP_HIDDEN).astype(BF16)


def _pe_operand(pe):
    lo = jnp.tile(pe[:CMP_STRIDE, None, :], (1, N_KV_HEADS, 1)).reshape(1, -1)
    hi = jnp.tile(pe[CMP_STRIDE:, None, :], (1, N_KV_HEADS, 1)).reshape(1, -1)
    return jnp.concatenate([jnp.tile(lo, (4, 1)), jnp.tile(hi, (4, 1))], axis=0).astype(BF16)


def _block_diag(blocks):
    g, a, b = blocks.shape
    eye = jnp.eye(g, dtype=blocks.dtype)
    return jnp.einsum('gab,gh->gahb', blocks, eye).reshape(g * a, g * b)


def _overlap_matrix(seq):
    n_cmp = seq // CMP_STRIDE
    n_sel = seq // SLC_BLOCK
    ci = np.arange(n_cmp)[None, :]
    sj = np.arange(n_sel)[:, None]
    ovl = (ci * CMP_STRIDE < (sj + 1) * SLC_BLOCK) & (ci * CMP_STRIDE + CMP_LEN > sj * SLC_BLOCK) & (ci < n_cmp - 1)
    return jnp.asarray(ovl.astype(np.float32), dtype=BF16)


def _gate_spread_matrix():
    e = np.zeros((N_BRANCH, N_KV_HEADS, GATE_PAD, GQA_GROUP * LANES), np.float32)
    for br in range(N_BRANCH):
        for g in range(N_KV_HEADS):
            for r in range(GQA_GROUP):
                e[br, g, br * N_Q_HEADS + g * GQA_GROUP + r, r * LANES:(r + 1) * LANES] = 1.0
    return jnp.asarray(e, dtype=BF16)


def _layer(x2d, p, batch, seq, tables, final_norm, norm_final):
    cos, sin, cos_c, sin_c, ovl, gexp = tables
    row = lambda v: v.reshape(1, -1).astype(F32)

    q, kc_raw, vc_raw, ksa, vsa, kwa, vwa, gates, u_ssm, u_conv = _in_proj(
        x2d, row(p['norm_mix']), _pack_w_in(p['w_in']), cos, sin, batch, seq)

    n_half = seq // CMP_STRIDE
    kh = kc_raw.reshape(batch, n_half, CMP_STRIDE * KV_WIDTH)
    vh = vc_raw.reshape(batch, n_half, CMP_STRIDE * KV_WIDTH)

    wk = _pack_cmp_w1(p['cmp_k_w1'])
    wv = _pack_cmp_w1(p['cmp_v_w1'])
    pek = _pe_operand(p['cmp_pe_k'])
    pev = _pe_operand(p['cmp_pe_v'])
    kca, vca = _compress(kh, vh, wk, wv, pek, pev, p['cmp_k_w2'].astype(BF16), p['cmp_v_w2'].astype(BF16),
                       cos_c, sin_c)

    y_attn = _attention(q, kca, vca, ksa, vsa, kwa, vwa, gates, ovl, gexp, batch, seq)

    lam_re, lam_im, bb_re, bb_im = _ssm_prep(
        p['ssm_a_re'], p['ssm_a_im'], p['ssm_log_dt'].reshape(-1, 1),
        p['ssm_b_re'].transpose(0, 2, 1), p['ssm_b_im'].transpose(0, 2, 1))
    wb = jnp.concatenate([_block_diag(bb_re), _block_diag(bb_im)], axis=1).astype(BF16)
    wc = jnp.concatenate([_block_diag(p['ssm_c_re'].transpose(0, 2, 1)),
                          -_block_diag(p['ssm_c_im'].transpose(0, 2, 1))], axis=0).astype(BF16)
    lam_re_b = jnp.broadcast_to(lam_re.reshape(1, -1), (batch, SSM_STATES_ALL))
    lam_im_b = jnp.broadcast_to(lam_im.reshape(1, -1), (batch, SSM_STATES_ALL))
    y_ssm = _ssm(u_ssm.reshape(seq * batch, SSM_WIDTH), wb, lam_re_b, lam_im_b, wc, row(p['ssm_d']),
                 p['ssm_w_glu'].astype(BF16), batch, seq)
    y_ssm_tm = y_ssm.reshape(seq, batch * SSM_WIDTH)

    wdw = jnp.pad(p['conv_w_dw'], ((0, CONV_PAD - CONV_K), (0, 0)))
    y_conv = _conv(u_conv, wdw, row(p['conv_b_dw']), row(p['conv_ln_g']), row(p['conv_ln_b']),
                   p['conv_w_pw'].astype(BF16), row(p['conv_b_pw']), batch, seq)

    return _out_ffn(x2d, y_attn, y_ssm_tm, y_conv, row(p['norm_out_attn']), row(p['norm_out_ssm']),
                    row(p['norm_out_conv']), p['w_out'].astype(BF16), row(p['norm_ffn']),
                    p['w_gate'].astype(BF16), p['w_up'].astype(BF16), p['w_down'].astype(BF16),
                    row(norm_final), batch, seq, final_norm)


_LAYER_KEYS = ('norm_mix', 'w_in', 'cmp_pe_k', 'cmp_k_w1', 'cmp_k_w2', 'cmp_pe_v', 'cmp_v_w1', 'cmp_v_w2',
               'ssm_a_re', 'ssm_a_im', 'ssm_log_dt', 'ssm_b_re', 'ssm_b_im', 'ssm_c_re', 'ssm_c_im', 'ssm_d',
               'ssm_w_glu', 'conv_w_dw', 'conv_b_dw', 'conv_ln_g', 'conv_ln_b', 'conv_w_pw', 'conv_b_pw',
               'norm_out_attn', 'norm_out_ssm', 'norm_out_conv', 'w_out', 'norm_ffn', 'w_gate', 'w_up', 'w_down')


def kernel(x, norm_mix, w_in, cmp_pe_k, cmp_k_w1, cmp_k_w2, cmp_pe_v, cmp_v_w1, cmp_v_w2, ssm_a_re, ssm_a_im, ssm_log_dt, ssm_b_re, ssm_b_im, ssm_c_re, ssm_c_im, ssm_d, ssm_w_glu, conv_w_dw, conv_b_dw, conv_ln_g, conv_ln_b, conv_w_pw, conv_b_pw, norm_out_attn, norm_out_ssm, norm_out_conv, w_out, norm_ffn, w_gate, w_up, w_down, norm_final):
    stacked = dict(zip(_LAYER_KEYS, (norm_mix, w_in, cmp_pe_k, cmp_k_w1, cmp_k_w2, cmp_pe_v, cmp_v_w1, cmp_v_w2,
                                     ssm_a_re, ssm_a_im, ssm_log_dt, ssm_b_re, ssm_b_im, ssm_c_re, ssm_c_im, ssm_d,
                                     ssm_w_glu, conv_w_dw, conv_b_dw, conv_ln_g, conv_ln_b, conv_w_pw, conv_b_pw,
                                     norm_out_attn, norm_out_ssm, norm_out_conv, w_out, norm_ffn, w_gate, w_up,
                                     w_down)))
    batch, seq, _ = x.shape
    depth = norm_mix.shape[0]
    cos, sin = _rope_tables(jnp.arange(seq))
    cos_c, sin_c = _rope_tables(jnp.arange(seq // CMP_STRIDE) * CMP_STRIDE + CMP_LEN - 1)
    tables = (cos, sin, cos_c, sin_c, _overlap_matrix(seq), _gate_spread_matrix())
    x2d = x.reshape(batch * seq, D_MODEL)
    for l in range(depth):
        p = {k: v[l] for k, v in stacked.items()}
        x2d = _layer(x2d, p, batch, seq, tables, l == depth - 1, norm_final)
    return x2d.reshape(batch, seq, D_MODEL)
```
